```python
import jax
import jax.numpy as jnp
from jax import lax
import numpy as np

D_MODEL = 1024
BATCH = 4
SEQ = 4096
DEPTH = 2

GRID_W = 64
CTX_LEN = 256
N_MIXERS = 2
NORM_EPS = 1e-6
FFN_HIDDEN = -(-8 * D_MODEL // (3 * 256)) * 256
LRU_WIDTH = D_MODEL
LRU_BLOCK = 256
LRU_BLOCKS = LRU_WIDTH // LRU_BLOCK
CONV_W = 4
CONV_LEFT = 1
RG_C = 8.0
MLA_HEADS = 8
Q_LORA = 384
KV_LORA = 256
QK_NOPE = 128
QK_ROPE = 64
V_HEAD = 128
SM_SCALE = (QK_NOPE + QK_ROPE) ** -0.5
Q_BLOCK = 128
ROPE_THETA = 10000.0

kernel_name = 'hybrid_rglru_mla_prefix_dit'


def _rmsnorm(x, g):
    xf = x.astype(jnp.float32)
    y = xf * lax.rsqrt(jnp.mean(xf * xf, axis=-1, keepdims=True) + NORM_EPS)
    return (y * g.astype(jnp.float32)).astype(x.dtype)


def _modulate(h, shift, scale):
    return h * (1.0 + scale) + shift


def _swiglu(h, w_in, w_out):
    g, u = jnp.split(h @ w_in, 2, axis=-1)
    return (jax.nn.silu(g) * u) @ w_out


def _axial_rope_tables(n_tokens, dtype):
    rows = n_tokens // GRID_W
    row_ids = jnp.repeat(jnp.arange(rows, dtype=jnp.float32), GRID_W)
    col_ids = jnp.tile(jnp.arange(GRID_W, dtype=jnp.float32), rows)
    axis_dim = QK_ROPE // 2
    inv_freq = 1.0 / (ROPE_THETA ** (jnp.arange(0, axis_dim, 2, dtype=jnp.float32) / axis_dim))
    ang_r = (row_ids[:, None] * inv_freq)[:, None, :]
    ang_c = (col_ids[:, None] * inv_freq)[:, None, :]
    return (jnp.cos(ang_r).astype(dtype), jnp.sin(ang_r).astype(dtype),
            jnp.cos(ang_c).astype(dtype), jnp.sin(ang_c).astype(dtype))


def _rotate_axis(x, cos, sin):
    x1, x2 = jnp.split(x, 2, axis=-1)
    return jnp.concatenate([x1 * cos - x2 * sin, x2 * cos + x1 * sin], axis=-1)


def _apply_rope_2d(x, tabs):
    cos_r, sin_r, cos_c, sin_c = tabs
    half = QK_ROPE // 2
    return jnp.concatenate([_rotate_axis(x[..., :half], cos_r, sin_r),
                            _rotate_axis(x[..., half:], cos_c, sin_c)], axis=-1)


def _dwconv_centred(x, w, b):
    t = x.shape[1]
    xp = jnp.pad(x, ((0, 0), (CONV_LEFT, CONV_W - 1 - CONV_LEFT), (0, 0)))
    y = b
    for k in range(CONV_W):
        y = y + xp[:, k:k + t] * w[k]
    return y


def _blockdiag(x, w):
    xb = x.reshape(x.shape[:-1] + (LRU_BLOCKS, LRU_BLOCK))
    return jnp.einsum('btnc,ncd->btnd', xb, w).reshape(x.shape)


def _rglru_coeffs(xc, gw, gb, lam):
    xf = xc.astype(jnp.float32)
    r = jax.nn.sigmoid(_blockdiag(xf, gw[0].astype(jnp.float32)) + gb[0].astype(jnp.float32))
    ig = jax.nn.sigmoid(_blockdiag(xf, gw[1].astype(jnp.float32)) + gb[1].astype(jnp.float32))
    log_a = -RG_C * r * jax.nn.softplus(-lam.astype(jnp.float32))
    a = jnp.exp(log_a)
    b = jnp.sqrt(-jnp.expm1(2.0 * log_a)) * (ig * xf)
    return a, b


def _combine(left, right):
    a1, b1 = left
    a2, b2 = right
    return a1 * a2, a2 * b1 + b2


def _linear_scan(a, b, h0, reverse):
    if reverse:
        a = jnp.flip(a, axis=1)
        b = jnp.flip(b, axis=1)
    b = b.at[:, 0].add(a[:, 0] * h0)
    _, h = lax.associative_scan(_combine, (a, b), axis=1)
    if reverse:
        h = jnp.flip(h, axis=1)
        return h, h[:, 0]
    return h, h[:, -1]


def _rglru_mixer(h_ctx, h_lat, w_in, conv_w, conv_b, gate_w, gate_b, lam, w_out, need_ctx):
    g_lat, u_lat = jnp.split(h_lat @ w_in, 2, axis=-1)
    g_ctx, u_ctx = jnp.split(h_ctx @ w_in, 2, axis=-1)
    u_lat = _dwconv_centred(u_lat, conv_w, conv_b)
    u_ctx = _dwconv_centred(u_ctx, conv_w, conv_b)
    h0 = jnp.zeros((h_lat.shape[0], LRU_WIDTH), jnp.float32)
    y_lat = []
    y_ctx = []
    for d, reverse in enumerate((False, True)):
        a_c, b_c = _rglru_coeffs(u_ctx, gate_w[d], gate_b[d], lam[d])
        hc, hc_final = _linear_scan(a_c, b_c, h0, reverse)
        a_l, b_l = _rglru_coeffs(u_lat, gate_w[d], gate_b[d], lam[d])
        hl, _ = _linear_scan(a_l, b_l, hc_final, reverse)
        y_lat.append(hl)
        y_ctx.append(hc)
    out_lat = ((y_lat[0] + y_lat[1]).astype(g_lat.dtype) * jax.nn.gelu(g_lat)) @ w_out
    out_ctx = None
    if need_ctx:
        out_ctx = ((y_ctx[0] + y_ctx[1]).astype(g_ctx.dtype) * jax.nn.gelu(g_ctx)) @ w_out
    return out_ctx, out_lat


def _attend(q, k, v):
    s = jnp.einsum('bqhd,bkhd->bhqk', q, k).astype(jnp.float32) * SM_SCALE
    p = jax.nn.softmax(s, axis=-1).astype(v.dtype)
    return jnp.einsum('bhqk,bkhd->bqhd', p, v)


def _blocked_attention(q, k, v):
    b, s, h, dq = q.shape
    nb = s // Q_BLOCK
    qb = jnp.moveaxis(q.reshape(b, nb, Q_BLOCK, h, dq), 1, 0)
    ob = lax.map(lambda qi: _attend(qi, k, v), qb)
    return jnp.moveaxis(ob, 0, 1).reshape(b, s, h, v.shape[-1])


def _mla_mixer(h_ctx, h_lat, w_in, q_norm_g, kv_norm_g, w_uq, w_ukv, w_o, tabs, need_ctx):
    n_ctx = h_ctx.shape[1]
    b, s, _ = h_lat.shape
    h_all = jnp.concatenate([h_ctx, h_lat], axis=1)
    t = h_all.shape[1]
    cq, ckv, kr = jnp.split(h_all @ w_in, [Q_LORA, Q_LORA + KV_LORA], axis=-1)
    q = (_rmsnorm(cq, q_norm_g) @ w_uq).reshape(b, t, MLA_HEADS, QK_NOPE + QK_ROPE)
    kv = (_rmsnorm(ckv, kv_norm_g) @ w_ukv).reshape(b, t, MLA_HEADS, QK_NOPE + V_HEAD)
    k_nope, v = jnp.split(kv, [QK_NOPE], axis=-1)
    kr = kr[:, :, None, :]
    k_rope = jnp.concatenate([kr[:, :n_ctx], _apply_rope_2d(kr[:, n_ctx:], tabs)], axis=1)
    k = jnp.concatenate([k_nope, jnp.broadcast_to(k_rope, (b, t, MLA_HEADS, QK_ROPE))], axis=-1)
    q_l = q[:, n_ctx:]
    q_lat = jnp.concatenate([q_l[..., :QK_NOPE], _apply_rope_2d(q_l[..., QK_NOPE:], tabs)], axis=-1)
    o_lat = _blocked_attention(q_lat, k, v)
    out_lat = o_lat.reshape(b, s, MLA_HEADS * V_HEAD) @ w_o
    out_ctx = None
    if need_ctx:
        o_ctx = _attend(q[:, :n_ctx], k[:, :n_ctx], v[:, :n_ctx])
        out_ctx = o_ctx.reshape(b, n_ctx, MLA_HEADS * V_HEAD) @ w_o
    return out_ctx, out_lat


def setup_inputs(seed: int = 0) -> dict:
    key = jax.random.key(seed)
    ks = jax.random.split(key, 24)
    f32 = jnp.float32
    n_lru = (DEPTH + 1) // 2
    n_mla = DEPTH // 2
    r = LRU_WIDTH

    def nrm(k, shape, fan_in):
        return jax.random.normal(k, shape, f32) * fan_in ** -0.5

    def gain(k, shape):
        return 1.0 + 0.05 * jax.random.normal(k, shape, f32)

    a0 = jax.random.uniform(ks[14], (n_lru, 2, r), f32, 0.9, 0.999)
    s = a0 ** (1.0 / RG_C)
    lam = jnp.log(s) - jnp.log1p(-s)
    return {
        'x': jax.random.normal(ks[0], (BATCH, SEQ, D_MODEL), f32),
        'c': jax.random.normal(ks[1], (BATCH, D_MODEL), f32),
        'ctx': jax.random.normal(ks[2], (BATCH, CTX_LEN, D_MODEL), f32),
        'c_ctx': jax.random.normal(ks[3], (D_MODEL,), f32),
        'ada_w': nrm(ks[4], (DEPTH, D_MODEL, 6 * D_MODEL), D_MODEL),
        'ada_b': 0.01 * jax.random.normal(ks[5], (DEPTH, 6 * D_MODEL), f32),
        'norm_mix_g': gain(ks[6], (DEPTH, D_MODEL)),
        'norm_ffn_g': gain(ks[7], (DEPTH, D_MODEL)),
        'ffn_w_in': nrm(ks[8], (DEPTH, D_MODEL, 2 * FFN_HIDDEN), D_MODEL),
        'ffn_w_out': nrm(ks[9], (DEPTH, FFN_HIDDEN, D_MODEL), FFN_HIDDEN),
        'lru_w_in': nrm(ks[10], (n_lru, D_MODEL, 2 * r), D_MODEL),
        'lru_conv_w': nrm(ks[11], (n_lru, CONV_W, r), CONV_W),
        'lru_conv_b': 0.01 * jax.random.normal(ks[12], (n_lru, r), f32),
        'lru_gate_w': nrm(ks[13], (n_lru, 2, 2, LRU_BLOCKS, LRU_BLOCK, LRU_BLOCK), LRU_BLOCK),
        'lru_gate_b': 0.01 * jax.random.normal(ks[15], (n_lru, 2, 2, r), f32),
        'lru_lambda': lam,
        'lru_w_out': nrm(ks[16], (n_lru, r, D_MODEL), r),
        'mla_w_in': nrm(ks[17], (n_mla, D_MODEL, Q_LORA + KV_LORA + QK_ROPE), D_MODEL),
        'mla_q_norm_g': gain(ks[18], (n_mla, Q_LORA)),
        'mla_kv_norm_g': gain(ks[19], (n_mla, KV_LORA)),
        'mla_w_uq': nrm(ks[20], (n_mla, Q_LORA, MLA_HEADS * (QK_NOPE + QK_ROPE)), Q_LORA),
        'mla_w_ukv': nrm(ks[21], (n_mla, KV_LORA, MLA_HEADS * (QK_NOPE + V_HEAD)), KV_LORA),
        'mla_w_o': nrm(ks[22], (n_mla, MLA_HEADS * V_HEAD, D_MODEL), MLA_HEADS * V_HEAD),
        'final_norm_g': gain(ks[23], (D_MODEL,)),
    }


def reference(x, c, ctx, c_ctx, ada_w, ada_b, norm_mix_g, norm_ffn_g, ffn_w_in, ffn_w_out,
              lru_w_in, lru_conv_w, lru_conv_b, lru_gate_w, lru_gate_b, lru_lambda, lru_w_out,
              mla_w_in, mla_q_norm_g, mla_kv_norm_g, mla_w_uq, mla_w_ukv, mla_w_o, final_norm_g):
    tabs = _axial_rope_tables(x.shape[1], x.dtype)
    silu_c = jax.nn.silu(c)
    silu_cc = jax.nn.silu(c_ctx)
    for i in range(DEPTH):
        need_ctx = i < DEPTH - 1
        j = i // N_MIXERS
        mod_lat = jnp.split((silu_c @ ada_w[i] + ada_b[i])[:, None, :], 6, axis=-1)
        mod_ctx = jnp.split(silu_cc @ ada_w[i] + ada_b[i], 6, axis=-1)
        h_lat = _modulate(_rmsnorm(x, norm_mix_g[i]), mod_lat[0], mod_lat[1])
        h_ctx = _modulate(_rmsnorm(ctx, norm_mix_g[i]), mod_ctx[0], mod_ctx[1])
        if i % N_MIXERS == 0:
            o_ctx, o_lat = _rglru_mixer(h_ctx, h_lat, lru_w_in[j], lru_conv_w[j], lru_conv_b[j],
                                        lru_gate_w[j], lru_gate_b[j], lru_lambda[j], lru_w_out[j],
                                        need_ctx)
        else:
            o_ctx, o_lat = _mla_mixer(h_ctx, h_lat, mla_w_in[j], mla_q_norm_g[j], mla_kv_norm_g[j],
                                      mla_w_uq[j], mla_w_ukv[j], mla_w_o[j], tabs, need_ctx)
        x = x + mod_lat[2] * o_lat
        x = x + mod_lat[5] * _swiglu(_modulate(_rmsnorm(x, norm_ffn_g[i]), mod_lat[3], mod_lat[4]),
                                     ffn_w_in[i], ffn_w_out[i])
        if need_ctx:
            ctx = ctx + mod_ctx[2] * o_ctx
            ctx = ctx + mod_ctx[5] * _swiglu(_modulate(_rmsnorm(ctx, norm_ffn_g[i]), mod_ctx[3], mod_ctx[4]),
                                             ffn_w_in[i], ffn_w_out[i])
    return _rmsnorm(x, final_norm_g)
```

```python
import functools

import numpy as np
import jax
import jax.numpy as jnp
from jax import lax
from jax.experimental import pallas as pl
from jax.experimental.pallas import tpu as pltpu

F32 = jnp.float32
BF16 = jnp.bfloat16

D_MODEL = 1024
GRID_W = 64
NORM_EPS = 1e-6
FFN_HIDDEN = 2816
LRU_WIDTH = D_MODEL
LRU_BLOCK = 256
LRU_BLOCKS = LRU_WIDTH // LRU_BLOCK
CONV_W = 4
RG_C = 8.0
MLA_HEADS = 8
Q_LORA = 384
KV_LORA = 256
QK_NOPE = 128
QK_ROPE = 64
V_HEAD = 128
SM_SCALE = (QK_NOPE + QK_ROPE) ** -0.5
ROPE_THETA = 10000.0

MOD_ROWS = 8
SUBLANES = 8
HEAD_PAD = 256
VMEM_LIMIT_MB = 56


def _cparams(sem, vmem_mb=VMEM_LIMIT_MB):
    return pltpu.CompilerParams(dimension_semantics=sem, vmem_limit_bytes=vmem_mb << 20)


def _const_spec(shape):
    nd = len(shape)
    return pl.BlockSpec(shape, lambda *_: (0,) * nd)


def _norm_mod(x, g, shift, scale):
    ms = jnp.mean(x * x, axis=-1, keepdims=True)
    y = (x * lax.rsqrt(ms + NORM_EPS)) * g
    return y * (1.0 + scale) + shift


def _rmsnorm(x, g):
    ms = jnp.mean(x * x, axis=-1, keepdims=True)
    return (x * lax.rsqrt(ms + NORM_EPS)) * g


def _ada_kernel(cv_ref, w_ref, b_ref, o_ref):
    cv = cv_ref[...]
    s = cv * jax.nn.sigmoid(cv)
    o_ref[0] = jnp.dot(s.astype(BF16), w_ref[0].astype(BF16), preferred_element_type=F32) + b_ref[0]


def _ada_mods(cv, ada_w, ada_b):
    depth, d, n = ada_w.shape
    tn = 1536
    return pl.pallas_call(
        _ada_kernel,
        grid=(depth, n // tn),
        in_specs=[
            pl.BlockSpec((MOD_ROWS, d), lambda l, j: (0, 0)),
            pl.BlockSpec((1, d, tn), lambda l, j: (l, 0, j)),
            pl.BlockSpec((1, 1, tn), lambda l, j: (l, 0, j)),
        ],
        out_specs=pl.BlockSpec((1, MOD_ROWS, tn), lambda l, j: (l, 0, j)),
        out_shape=jax.ShapeDtypeStruct((depth, MOD_ROWS, n), F32),
        compiler_params=_cparams(("arbitrary", "arbitrary")),
        name="ada_mods",
    )(cv, ada_w, ada_b.reshape(depth, 1, n))


def _lru_in_kernel(x_ref, g_ref, mod_ref, w_ref, gg_ref, u_ref):
    h = _norm_mod(x_ref[...], g_ref[...], mod_ref[0, 0:1, :], mod_ref[0, 1:2, :])
    y = jnp.dot(h.astype(BF16), w_ref[...], preferred_element_type=F32)
    r = LRU_WIDTH
    gg_ref[...] = jax.nn.gelu(y[:, :r])
    u_ref[...] = y[:, r:]


def _lru_in(xall, g, mods, w, *, nb, n_ctx_rows, seq, tm):
    n, d = xall.shape
    r = LRU_WIDTH
    nct = n_ctx_rows // tm
    tpb = seq // tm

    def mod_map(i):
        return (jnp.where(i < nct, nb, (i - nct) // tpb), 0, 0)

    return pl.pallas_call(
        _lru_in_kernel,
        grid=(n // tm,),
        in_specs=[
            pl.BlockSpec((tm, d), lambda i: (i, 0)),
            _const_spec((1, d)),
            pl.BlockSpec((1, 6, d), mod_map),
            _const_spec((d, 2 * r)),
        ],
        out_specs=[pl.BlockSpec((tm, r), lambda i: (i, 0)), pl.BlockSpec((tm, r), lambda i: (i, 0))],
        out_shape=[jax.ShapeDtypeStruct((n, r), F32), jax.ShapeDtypeStruct((n, r), F32)],
        compiler_params=_cparams(("arbitrary",)),
        name="lru_in",
    )(xall, g, mods, w)


def _scan_kernel(*refs, reverse, with_out, tc, nc, nl):
    if with_out:
        (ucur, uprev, unext, cw, cb, gw, gb, lam, hf, gg, x, mod, wout,
         out, ext, a_s, b_s, hc, hb_s) = refs
    else:
        (ucur, uprev, unext, cw, cb, gw, gb, lam, out, ext, a_s, b_s, hc) = refs
        hb_s = out
    r = LRU_WIDTH
    j = pl.program_id(1)
    in_ctx = j < nc
    pos = jnp.where(in_ctx, j, j - nc)
    n_stream = jnp.where(in_ctx, nc, nl)
    tpos = (n_stream - 1 - pos) if reverse else pos
    first = tpos == 0
    last = tpos == n_stream - 1

    @pl.when(j == 0)
    def _():
        hc[...] = jnp.zeros_like(hc)

    ext[0:SUBLANES, :] = jnp.where(first, 0.0, uprev[...])
    ext[SUBLANES:SUBLANES + tc, :] = ucur[...]
    ext[SUBLANES + tc:2 * SUBLANES + tc, :] = jnp.where(last, 0.0, unext[...])
    xc = cb[...]
    for k in range(CONV_W):
        xc = xc + ext[pl.ds(SUBLANES - 1 + k, tc), :] * cw[k:k + 1, :]

    xcb = xc.astype(BF16)
    lam_v = lam[...]
    neg = -lam_v
    sp = jnp.maximum(neg, 0.0) + jnp.log1p(jnp.exp(-jnp.abs(neg)))
    for n in range(LRU_BLOCKS):
        sl = slice(n * LRU_BLOCK, (n + 1) * LRU_BLOCK)
        pre = jnp.dot(xcb[:, sl], gw[n], preferred_element_type=F32)
        rg = jax.nn.sigmoid(pre[:, :LRU_BLOCK] + gb[0:1, sl])
        ig = jax.nn.sigmoid(pre[:, LRU_BLOCK:] + gb[1:2, sl])
        log_a = (-RG_C) * rg * sp[:, sl]
        a = jnp.exp(log_a)
        mult = jnp.sqrt(1.0 - jnp.exp(2.0 * log_a))
        a_s[:, sl] = a
        b_s[:, sl] = mult * (ig * xc[:, sl])

    row = lax.broadcasted_iota(jnp.int32, (SUBLANES, r), 0)
    ngroups = tc // SUBLANES

    def group(gi, hprev):
        g0 = (ngroups - 1 - gi) if reverse else gi
        rows = pl.ds(pl.multiple_of(g0 * SUBLANES, SUBLANES), SUBLANES)
        av = a_s[rows, :]
        bv = b_s[rows, :]
        for s in (1, 2, 4):
            if reverse:
                keep = row < (SUBLANES - s)
                shift = SUBLANES - s
            else:
                keep = row >= s
                shift = s
            a_sh = jnp.where(keep, pltpu.roll(av, shift, 0), 1.0)
            b_sh = jnp.where(keep, pltpu.roll(bv, shift, 0), 0.0)
            bv = av * b_sh + bv
            av = av * a_sh
        hrows = av * hprev + bv
        hb_s[rows, :] = hrows
        edge = hrows[0:1, :] if reverse else hrows[SUBLANES - 1:SUBLANES, :]
        return jnp.broadcast_to(edge, (SUBLANES, r))

    hc[...] = lax.fori_loop(0, ngroups, group, hc[...], unroll=2)

    if with_out:
        y = hf[...] + hb_s[...]
        z = (y * gg[...]).astype(BF16)
        o = jnp.dot(z, wout[...], preferred_element_type=F32)
        out[...] = x[...] + mod[0, 2:3, :] * o


def _lru_scan(u, conv_w, conv_b, gw, gb, lam, *, nb, ctx_len, seq, tc, reverse, extra=None):
    n, r = u.shape
    nc = ctx_len // tc
    nl = seq // tc
    blk8 = tc // SUBLANES

    def cur_blk(b, j):
        jc = (nc - 1 - j) if reverse else j
        jl = (nl - 1 - (j - nc)) if reverse else (j - nc)
        return jnp.where(j < nc, b * nc + jc, nb * nc + b * nl + jl)

    def cur_map(b, j):
        return (cur_blk(b, j), 0)

    def prev_map(b, j):
        return (jnp.maximum(cur_blk(b, j) * blk8 - 1, 0), 0)

    def next_map(b, j):
        return (jnp.minimum((cur_blk(b, j) + 1) * blk8, n // SUBLANES - 1), 0)

    in_specs = [
        pl.BlockSpec((tc, r), cur_map),
        pl.BlockSpec((SUBLANES, r), prev_map),
        pl.BlockSpec((SUBLANES, r), next_map),
        _const_spec((CONV_W, r)),
        _const_spec((1, r)),
        _const_spec((LRU_BLOCKS, LRU_BLOCK, 2 * LRU_BLOCK)),
        _const_spec((2, r)),
        _const_spec((1, r)),
    ]
    args = [u, u, u, conv_w, conv_b, gw, gb, lam]
    scratch = [
        pltpu.VMEM((tc + 2 * SUBLANES, r), F32),
        pltpu.VMEM((tc, r), F32),
        pltpu.VMEM((tc, r), F32),
        pltpu.VMEM((SUBLANES, r), F32),
    ]
    with_out = extra is not None
    if with_out:
        hf, gg, xall, mods, wout = extra
        d = xall.shape[1]
        in_specs += [
            pl.BlockSpec((tc, r), cur_map),
            pl.BlockSpec((tc, r), cur_map),
            pl.BlockSpec((tc, d), cur_map),
            pl.BlockSpec((1, 6, d), lambda b, j: (jnp.where(j < nc, nb, b), 0, 0)),
            _const_spec((r, d)),
        ]
        args += [hf, gg, xall, mods, wout]
        scratch.append(pltpu.VMEM((tc, r), F32))
        out_dim = d
    else:
        out_dim = r
    return pl.pallas_call(
        functools.partial(_scan_kernel, reverse=reverse, with_out=with_out, tc=tc, nc=nc, nl=nl),
        grid=(nb, nc + nl),
        in_specs=in_specs,
        out_specs=pl.BlockSpec((tc, out_dim), cur_map),
        out_shape=jax.ShapeDtypeStruct((n, out_dim), F32),
        scratch_shapes=scratch,
        compiler_params=_cparams(("arbitrary", "arbitrary")),
        name="lru_scan_bwd" if reverse else "lru_scan_fwd",
    )(*args)


def _ffn_kernel(*refs, with_attn, final_norm, n_chunks):
    if with_attn:
        x_ref, o_ref, wo_ref, mod_ref, g_ref, wg_ref, wu_ref, wd_ref, fg_ref, out_ref = refs
    else:
        x_ref, mod_ref, g_ref, wg_ref, wu_ref, wd_ref, fg_ref, out_ref = refs
    x = x_ref[...]
    if with_attn:
        x = x + mod_ref[0, 2:3, :] * jnp.dot(o_ref[...], wo_ref[...], preferred_element_type=F32)
    h = _norm_mod(x, g_ref[...], mod_ref[0, 3:4, :], mod_ref[0, 4:5, :]).astype(BF16)
    hid = wg_ref.shape[1]
    ck = hid // n_chunks
    acc = jnp.zeros(x.shape, F32)
    for c in range(n_chunks):
        sl = slice(c * ck, (c + 1) * ck)
        gate = jnp.dot(h, wg_ref[:, sl], preferred_element_type=F32)
        up = jnp.dot(h, wu_ref[:, sl], preferred_element_type=F32)
        act = ((gate * jax.nn.sigmoid(gate)) * up).astype(BF16)
        acc = acc + jnp.dot(act, wd_ref[sl, :], preferred_element_type=F32)
    x = x + mod_ref[0, 5:6, :] * acc
    if final_norm:
        x = _rmsnorm(x, fg_ref[...])
    out_ref[...] = x


def _ffn(xall, mods, g, wg, wu, wd, fg, *, nb, n_ctx_rows, seq, tm, lat_only, attn=None, final_norm=False):
    n, d = xall.shape
    hid = wg.shape[1]
    nct = n_ctx_rows // tm
    tpb = seq // tm
    off = nct if lat_only else 0
    n_out = n - n_ctx_rows if lat_only else n

    def mod_map(i):
        ii = i + off
        return (jnp.where(ii < nct, nb, (ii - nct) // tpb), 0, 0)

    in_specs = [pl.BlockSpec((tm, d), lambda i: (i + off, 0))]
    args = [xall]
    if attn is not None:
        o, wo = attn
        in_specs += [pl.BlockSpec((tm, o.shape[1]), lambda i: (i, 0)), _const_spec(wo.shape)]
        args += [o, wo]
    in_specs += [
        pl.BlockSpec((1, 6, d), mod_map),
        _const_spec((1, d)),
        pl.BlockSpec((d, hid), lambda i: (0, 0), pipeline_mode=pl.Buffered(1)),
        pl.BlockSpec((d, hid), lambda i: (0, 0), pipeline_mode=pl.Buffered(1)),
        pl.BlockSpec((hid, d), lambda i: (0, 0), pipeline_mode=pl.Buffered(1)),
        _const_spec((1, d)),
    ]
    args += [mods, g, wg, wu, wd, fg]
    return pl.pallas_call(
        functools.partial(_ffn_kernel, with_attn=attn is not None, final_norm=final_norm, n_chunks=2),
        grid=(n_out // tm,),
        in_specs=in_specs,
        out_specs=pl.BlockSpec((tm, d), lambda i: (i, 0)),
        out_shape=jax.ShapeDtypeStruct((n_out, d), F32),
        compiler_params=_cparams(("arbitrary",)),
        name="ffn_attn" if attn is not None else "ffn",
    )(*args)


def _mla_proj_kernel(x_ref, g_ref, mod_ref, win_ref, qg_ref, kvg_ref, wq_ref, wkv_ref, cos_ref, sin_ref,
                     q_ref, k_ref, v_ref, *, nct):
    i = pl.program_id(0)
    is_ctx = i < nct
    h = _norm_mod(x_ref[...], g_ref[...], mod_ref[0, 0:1, :], mod_ref[0, 1:2, :]).astype(BF16)
    c_all = jnp.dot(h, win_ref[...], preferred_element_type=F32)
    cosv = jnp.where(is_ctx, 1.0, cos_ref[...])
    sinv = jnp.where(is_ctx, 0.0, sin_ref[...])
    kv_lo = Q_LORA + KV_LORA
    k_rope = (c_all[:, kv_lo:kv_lo + 128] * cosv + c_all[:, kv_lo + 128:kv_lo + 256] * sinv).astype(BF16)
    ckv = _rmsnorm(c_all[:, Q_LORA:kv_lo], kvg_ref[...]).astype(BF16)
    kv = jnp.dot(ckv, wkv_ref[...], preferred_element_type=F32)
    for hd in range(MLA_HEADS):
        base = hd * (QK_NOPE + V_HEAD)
        k_ref[0, hd, :, 0:QK_NOPE] = kv[:, base:base + QK_NOPE].astype(BF16)
        k_ref[0, hd, :, QK_NOPE:HEAD_PAD] = k_rope
        v_ref[0, hd] = kv[:, base + QK_NOPE:base + QK_NOPE + V_HEAD].astype(BF16)

    @pl.when(jnp.logical_not(is_ctx))
    def _():
        cq = _rmsnorm(c_all[:, :Q_LORA], qg_ref[...]).astype(BF16)
        q_all = jnp.dot(cq, wq_ref[...], preferred_element_type=F32)
        hw = MLA_HEADS * 128
        for hd in range(MLA_HEADS):
            sl = slice(hd * 128, (hd + 1) * 128)
            q_ref[0, hd, :, 0:QK_NOPE] = q_all[:, sl].astype(BF16)
            q_rope = q_all[:, hw + hd * 128:hw + (hd + 1) * 128] * cos_ref[...] \
                + q_all[:, 2 * hw + hd * 128:2 * hw + (hd + 1) * 128] * sin_ref[...]
            q_ref[0, hd, :, QK_NOPE:HEAD_PAD] = q_rope.astype(BF16)


def _mla_proj(xall, g, mods, win, qg, kvg, wq, wkv, cos_t, sin_t, *, nb, ctx_len, seq, tm):
    n, d = xall.shape
    nct = nb * ctx_len // tm
    cpb = ctx_len // tm
    tpb = seq // tm
    t_all = ctx_len + seq

    def lat(i):
        return jnp.maximum(i - nct, 0)

    def mod_map(i):
        return (jnp.where(i < nct, nb, lat(i) // tpb), 0, 0)

    def kv_map(i):
        b = jnp.where(i < nct, i // cpb, lat(i) // tpb)
        t = jnp.where(i < nct, i % cpb, cpb + lat(i) % tpb)
        return (b, 0, t, 0)

    def q_map(i):
        return (lat(i) // tpb, 0, lat(i) % tpb, 0)

    def tab_map(i):
        return (lat(i) % tpb, 0)

    return pl.pallas_call(
        functools.partial(_mla_proj_kernel, nct=nct),
        grid=(n // tm,),
        in_specs=[
            pl.BlockSpec((tm, d), lambda i: (i, 0)),
            _const_spec((1, d)),
            pl.BlockSpec((1, 6, d), mod_map),
            _const_spec(win.shape),
            _const_spec((1, Q_LORA)),
            _const_spec((1, KV_LORA)),
            _const_spec(wq.shape),
            _const_spec(wkv.shape),
            pl.BlockSpec((tm, 128), tab_map),
            pl.BlockSpec((tm, 128), tab_map),
        ],
        out_specs=[
            pl.BlockSpec((1, MLA_HEADS, tm, HEAD_PAD), q_map),
            pl.BlockSpec((1, MLA_HEADS, tm, HEAD_PAD), kv_map),
            pl.BlockSpec((1, MLA_HEADS, tm, V_HEAD), kv_map),
        ],
        out_shape=[
            jax.ShapeDtypeStruct((nb, MLA_HEADS, seq, HEAD_PAD), BF16),
            jax.ShapeDtypeStruct((nb, MLA_HEADS, t_all, HEAD_PAD), BF16),
            jax.ShapeDtypeStruct((nb, MLA_HEADS, t_all, V_HEAD), BF16),
        ],
        compiler_params=_cparams(("arbitrary",)),
        name="mla_proj",
    )(xall, g, mods, win, qg, kvg, wq, wkv, cos_t, sin_t)


def _attn_kernel(q_ref, k_ref, v_ref, o_ref, *, tq):
    k = k_ref[0, 0]
    v = v_ref[0, 0]
    nq = q_ref.shape[2] // tq

    def body(qi, carry):
        rows = pl.ds(pl.multiple_of(qi * tq, tq), tq)
        q = q_ref[0, 0, rows, :]
        s = lax.dot_general(q, k, (((1,), (1,)), ((), ())), preferred_element_type=F32) * SM_SCALE
        m = jnp.max(s, axis=-1, keepdims=True)
        p = jnp.exp(s - m)
        l = jnp.sum(p, axis=-1, keepdims=True)
        o = jnp.dot(p.astype(BF16), v, preferred_element_type=F32)
        o_ref[0, rows, :] = (o / l).astype(o_ref.dtype)
        return carry

    lax.fori_loop(0, nq, body, 0)


def _attention(q, k, v, *, tq):
    nb, nh, seq, dp = q.shape
    t_all = k.shape[2]
    return pl.pallas_call(
        functools.partial(_attn_kernel, tq=tq),
        grid=(nb, nh),
        in_specs=[
            pl.BlockSpec((1, 1, seq, dp), lambda b, h: (b, h, 0, 0)),
            pl.BlockSpec((1, 1, t_all, dp), lambda b, h: (b, h, 0, 0)),
            pl.BlockSpec((1, 1, t_all, V_HEAD), lambda b, h: (b, h, 0, 0)),
        ],
        out_specs=pl.BlockSpec((1, seq, V_HEAD), lambda b, h: (b, 0, h)),
        out_shape=jax.ShapeDtypeStruct((nb, seq, nh * V_HEAD), BF16),
        compiler_params=_cparams(("arbitrary", "arbitrary")),
        name="mla_attention",
    )(q, k, v)


def _rope_perm():
    quarter = QK_ROPE // 4
    a_idx = np.concatenate([np.arange(0, quarter), np.arange(2 * quarter, 3 * quarter)])
    b_idx = np.concatenate([np.arange(quarter, 2 * quarter), np.arange(3 * quarter, 4 * quarter)])
    return a_idx, b_idx


def _rope_cols(w):
    a_idx, b_idx = _rope_perm()
    z = jnp.zeros((w.shape[0], 128 - QK_ROPE), w.dtype)
    plain = jnp.concatenate([w[:, a_idx], w[:, b_idx], z], axis=1)
    swapped = jnp.concatenate([-w[:, b_idx], w[:, a_idx], z], axis=1)
    return plain, swapped


def _rope_tables(seq):
    rows = seq // GRID_W
    row_ids = np.repeat(np.arange(rows, dtype=np.float32), GRID_W)
    col_ids = np.tile(np.arange(GRID_W, dtype=np.float32), rows)
    axis_dim = QK_ROPE // 2
    expo = (np.arange(0, axis_dim, 2, dtype=np.float32) / np.float32(axis_dim)).astype(np.float32)
    inv_freq = (np.float32(1.0) / np.power(np.float32(ROPE_THETA), expo)).astype(np.float32)
    ang_r = (row_ids[:, None] * inv_freq).astype(np.float32)
    ang_c = (col_ids[:, None] * inv_freq).astype(np.float32)
    pad = np.zeros((seq, 128 - QK_ROPE), np.float32)
    cos_t = np.concatenate([np.cos(ang_r), np.cos(ang_c), np.cos(ang_r), np.cos(ang_c), pad + 1.0], axis=1)
    sin_t = np.concatenate([np.sin(ang_r), np.sin(ang_c), np.sin(ang_r), np.sin(ang_c), pad], axis=1)
    return jnp.asarray(cos_t, F32), jnp.asarray(sin_t, F32)


def kernel(x, c, ctx, c_ctx, ada_w, ada_b, norm_mix_g, norm_ffn_g, ffn_w_in, ffn_w_out, lru_w_in, lru_conv_w,
           lru_conv_b, lru_gate_w, lru_gate_b, lru_lambda, lru_w_out, mla_w_in, mla_q_norm_g, mla_kv_norm_g,
           mla_w_uq, mla_w_ukv, mla_w_o, final_norm_g):
    nb, seq, d = x.shape
    ctx_len = ctx.shape[1]
    depth = ada_w.shape[0]
    assert depth == 2 and d == D_MODEL and nb + 1 <= MOD_ROWS
    n_ctx_rows = nb * ctx_len
    tile = lambda cap: int(np.gcd(np.gcd(cap, n_ctx_rows), seq))

    xall = jnp.concatenate([ctx.reshape(n_ctx_rows, d), x.reshape(nb * seq, d)], axis=0)

    cv = jnp.concatenate([c, c_ctx[None, :], jnp.zeros((MOD_ROWS - nb - 1, d), F32)], axis=0)
    mods = _ada_mods(cv, ada_w, ada_b).reshape(depth, MOD_ROWS, 6, d)

    tm = tile(1024)
    gg, u = _lru_in(xall, norm_mix_g[0][None], mods[0], lru_w_in[0].astype(BF16),
                    nb=nb, n_ctx_rows=n_ctx_rows, seq=seq, tm=tm)
    tc = int(np.gcd(np.gcd(256, ctx_len), seq))
    gw = lru_gate_w[0]
    gw = jnp.concatenate([gw[:, 0], gw[:, 1]], axis=-1).astype(BF16)
    scan_args = dict(nb=nb, ctx_len=ctx_len, seq=seq, tc=tc)
    conv_w, conv_b = lru_conv_w[0], lru_conv_b[0][None]
    hf = _lru_scan(u, conv_w, conv_b, gw[0], lru_gate_b[0, 0], lru_lambda[0, 0][None], reverse=False, **scan_args)
    x1 = _lru_scan(u, conv_w, conv_b, gw[1], lru_gate_b[0, 1], lru_lambda[0, 1][None], reverse=True,
                   extra=(hf, gg, xall, mods[0], lru_w_out[0].astype(BF16)), **scan_args)
    tf = tile(512)
    wg0, wu0 = ffn_w_in[0][:, :FFN_HIDDEN].astype(BF16), ffn_w_in[0][:, FFN_HIDDEN:].astype(BF16)
    x2 = _ffn(x1, mods[0], norm_ffn_g[0][None], wg0, wu0, ffn_w_out[0].astype(BF16), final_norm_g[None],
              nb=nb, n_ctx_rows=n_ctx_rows, seq=seq, tm=tf, lat_only=False)

    w_in = mla_w_in[0]
    kr_plain, kr_swap = _rope_cols(w_in[:, Q_LORA + KV_LORA:])
    win_p = jnp.concatenate([w_in[:, :Q_LORA + KV_LORA], kr_plain, kr_swap], axis=1).astype(BF16)
    wq3 = mla_w_uq[0].reshape(Q_LORA, MLA_HEADS, QK_NOPE + QK_ROPE)
    q_nope = wq3[:, :, :QK_NOPE].reshape(Q_LORA, MLA_HEADS * QK_NOPE)
    q_pairs = [_rope_cols(wq3[:, hd, QK_NOPE:]) for hd in range(MLA_HEADS)]
    wq_all = jnp.concatenate([q_nope] + [p[0] for p in q_pairs] + [p[1] for p in q_pairs], axis=1).astype(BF16)
    cos_t, sin_t = _rope_tables(seq)
    tp = int(np.gcd(np.gcd(256, ctx_len), seq))
    q, k, v = _mla_proj(x2, norm_mix_g[1][None], mods[1], win_p, mla_q_norm_g[0][None], mla_kv_norm_g[0][None],
                        wq_all, mla_w_ukv[0].astype(BF16), cos_t, sin_t, nb=nb, ctx_len=ctx_len, seq=seq, tm=tp)
    o = _attention(q, k, v, tq=int(np.gcd(256, seq)))
    wg1, wu1 = ffn_w_in[1][:, :FFN_HIDDEN].astype(BF16), ffn_w_in[1][:, FFN_HIDDEN:].astype(BF16)
    out = _ffn(x2, mods[1], norm_ffn_g[1][None], wg1, wu1, ffn_w_out[1].astype(BF16), final_norm_g[None],
               nb=nb, n_ctx_rows=n_ctx_rows, seq=seq, tm=tf, lat_only=True,
               attn=(o.reshape(nb * seq, MLA_HEADS * V_HEAD), mla_w_o[0].astype(BF16)), final_norm=True)
    return out.reshape(nb, seq, d)
```

```python
import functools

import numpy as np
import jax
import jax.numpy as jnp
from jax import lax
from jax.experimental import pallas as pl
from jax.experimental.pallas import tpu as pltpu

F32 = jnp.float32
BF16 = jnp.bfloat16

D_MODEL = 1024
GRID_W = 64
NORM_EPS = 1e-6
FFN_HIDDEN = 2816
LRU_WIDTH = D_MODEL
LRU_BLOCK = 256
LRU_BLOCKS = LRU_WIDTH // LRU_BLOCK
CONV_W = 4
RG_C = 8.0
MLA_HEADS = 8
Q_LORA = 384
KV_LORA = 256
QK_NOPE = 128
QK_ROPE = 64
V_HEAD = 128
SM_SCALE = (QK_NOPE + QK_ROPE) ** -0.5
Q_PRESCALE = SM_SCALE * float(np.log2(np.e))
ROPE_THETA = 10000.0

MOD_ROWS = 8
SUBLANES = 8
HEAD_PAD = 256
VMEM_LIMIT_MB = 56
QK_GROUPS = 4


def _cparams(sem, vmem_mb=VMEM_LIMIT_MB):
    return pltpu.CompilerParams(dimension_semantics=sem, vmem_limit_bytes=vmem_mb << 20)


def _const_spec(shape):
    nd = len(shape)
    return pl.BlockSpec(shape, lambda *_: (0,) * nd)


def _norm_mod(x, g, shift, scale):
    ms = jnp.mean(x * x, axis=-1, keepdims=True)
    y = (x * lax.rsqrt(ms + NORM_EPS)) * g
    return y * (1.0 + scale) + shift


def _rmsnorm(x, g):
    ms = jnp.mean(x * x, axis=-1, keepdims=True)
    return (x * lax.rsqrt(ms + NORM_EPS)) * g


def _ada_kernel(cv_ref, w_ref, b_ref, o_ref):
    cv = cv_ref[...]
    s = cv * jax.nn.sigmoid(cv)
    o_ref[0] = jnp.dot(s.astype(BF16), w_ref[0].astype(BF16), preferred_element_type=F32) + b_ref[0]


def _ada_mods(cv, ada_w, ada_b):
    depth, d, n = ada_w.shape
    tn = 1536
    return pl.pallas_call(
        _ada_kernel,
        grid=(depth, n // tn),
        in_specs=[
            pl.BlockSpec((MOD_ROWS, d), lambda l, j: (0, 0)),
            pl.BlockSpec((1, d, tn), lambda l, j: (l, 0, j)),
            pl.BlockSpec((1, 1, tn), lambda l, j: (l, 0, j)),
        ],
        out_specs=pl.BlockSpec((1, MOD_ROWS, tn), lambda l, j: (l, 0, j)),
        out_shape=jax.ShapeDtypeStruct((depth, MOD_ROWS, n), F32),
        compiler_params=_cparams(("arbitrary", "arbitrary")),
        name="ada_mods",
    )(cv, ada_w, ada_b.reshape(depth, 1, n))


def _lru_in_kernel(x_ref, g_ref, mod_ref, w_ref, gg_ref, u_ref):
    h = _norm_mod(x_ref[...], g_ref[...], mod_ref[0, 0:1, :], mod_ref[0, 1:2, :])
    y = jnp.dot(h.astype(BF16), w_ref[...], preferred_element_type=F32)
    r = LRU_WIDTH
    gg_ref[...] = jax.nn.gelu(y[:, :r])
    u_ref[...] = y[:, r:]


def _lru_in(xall, g, mods, w, *, nb, n_ctx_rows, seq, tm):
    n, d = xall.shape
    r = LRU_WIDTH
    nct = n_ctx_rows // tm
    tpb = seq // tm

    def mod_map(i):
        return (jnp.where(i < nct, nb, (i - nct) // tpb), 0, 0)

    return pl.pallas_call(
        _lru_in_kernel,
        grid=(n // tm,),
        in_specs=[
            pl.BlockSpec((tm, d), lambda i: (i, 0)),
            _const_spec((1, d)),
            pl.BlockSpec((1, 6, d), mod_map),
            _const_spec((d, 2 * r)),
        ],
        out_specs=[pl.BlockSpec((tm, r), lambda i: (i, 0)), pl.BlockSpec((tm, r), lambda i: (i, 0))],
        out_shape=[jax.ShapeDtypeStruct((n, r), F32), jax.ShapeDtypeStruct((n, r), F32)],
        compiler_params=_cparams(("arbitrary",)),
        name="lru_in",
    )(xall, g, mods, w)


def _scan_kernel(*refs, reverse, with_out, tc, nc, nl):
    if with_out:
        (ucur, uprev, unext, cw, cb, gw, gb, lam, hf, gg, x, mod, wout,
         out, ext, a_s, b_s, hc, hb_s) = refs
    else:
        (ucur, uprev, unext, cw, cb, gw, gb, lam, out, ext, a_s, b_s, hc) = refs
        hb_s = out
    r = LRU_WIDTH
    j = pl.program_id(1)
    in_ctx = j < nc
    pos = jnp.where(in_ctx, j, j - nc)
    n_stream = jnp.where(in_ctx, nc, nl)
    tpos = (n_stream - 1 - pos) if reverse else pos
    first = tpos == 0
    last = tpos == n_stream - 1

    @pl.when(j == 0)
    def _():
        hc[...] = jnp.zeros_like(hc)

    ext[0:SUBLANES, :] = jnp.where(first, 0.0, uprev[...])
    ext[SUBLANES:SUBLANES + tc, :] = ucur[...]
    ext[SUBLANES + tc:2 * SUBLANES + tc, :] = jnp.where(last, 0.0, unext[...])
    xc = cb[...]
    for k in range(CONV_W):
        xc = xc + ext[pl.ds(SUBLANES - 1 + k, tc), :] * cw[k:k + 1, :]

    xcb = xc.astype(BF16)
    lam_v = lam[...]
    neg = -lam_v
    sp = jnp.maximum(neg, 0.0) + jnp.log1p(jnp.exp(-jnp.abs(neg)))
    for n in range(LRU_BLOCKS):
        sl = slice(n * LRU_BLOCK, (n + 1) * LRU_BLOCK)
        pre = jnp.dot(xcb[:, sl], gw[n], preferred_element_type=F32)
        rg = jax.nn.sigmoid(pre[:, :LRU_BLOCK] + gb[0:1, sl])
        ig = jax.nn.sigmoid(pre[:, LRU_BLOCK:] + gb[1:2, sl])
        log_a = (-RG_C) * rg * sp[:, sl]
        a = jnp.exp(log_a)
        mult = jnp.sqrt(1.0 - jnp.exp(2.0 * log_a))
        a_s[:, sl] = a
        b_s[:, sl] = mult * (ig * xc[:, sl])

    row = lax.broadcasted_iota(jnp.int32, (SUBLANES, r), 0)
    ngroups = tc // SUBLANES

    def group(gi, hprev):
        g0 = (ngroups - 1 - gi) if reverse else gi
        rows = pl.ds(pl.multiple_of(g0 * SUBLANES, SUBLANES), SUBLANES)
        av = a_s[rows, :]
        bv = b_s[rows, :]
        for s in (1, 2, 4):
            if reverse:
                keep = row < (SUBLANES - s)
                shift = SUBLANES - s
            else:
                keep = row >= s
                shift = s
            a_sh = jnp.where(keep, pltpu.roll(av, shift, 0), 1.0)
            b_sh = jnp.where(keep, pltpu.roll(bv, shift, 0), 0.0)
            bv = av * b_sh + bv
            av = av * a_sh
        hrows = av * hprev + bv
        hb_s[rows, :] = hrows
        edge = hrows[0:1, :] if reverse else hrows[SUBLANES - 1:SUBLANES, :]
        return jnp.broadcast_to(edge, (SUBLANES, r))

    hc[...] = lax.fori_loop(0, ngroups, group, hc[...], unroll=2)

    if with_out:
        y = hf[...] + hb_s[...]
        z = (y * gg[...]).astype(BF16)
        o = jnp.dot(z, wout[...], preferred_element_type=F32)
        out[...] = x[...] + mod[0, 2:3, :] * o


def _lru_scan(u, conv_w, conv_b, gw, gb, lam, *, nb, ctx_len, seq, tc, reverse, extra=None):
    n, r = u.shape
    nc = ctx_len // tc
    nl = seq // tc
    blk8 = tc // SUBLANES

    def cur_blk(b, j):
        jc = (nc - 1 - j) if reverse else j
        jl = (nl - 1 - (j - nc)) if reverse else (j - nc)
        return jnp.where(j < nc, b * nc + jc, nb * nc + b * nl + jl)

    def cur_map(b, j):
        return (cur_blk(b, j), 0)

    def prev_map(b, j):
        return (jnp.maximum(cur_blk(b, j) * blk8 - 1, 0), 0)

    def next_map(b, j):
        return (jnp.minimum((cur_blk(b, j) + 1) * blk8, n // SUBLANES - 1), 0)

    in_specs = [
        pl.BlockSpec((tc, r), cur_map),
        pl.BlockSpec((SUBLANES, r), prev_map),
        pl.BlockSpec((SUBLANES, r), next_map),
        _const_spec((CONV_W, r)),
        _const_spec((1, r)),
        _const_spec((LRU_BLOCKS, LRU_BLOCK, 2 * LRU_BLOCK)),
        _const_spec((2, r)),
        _const_spec((1, r)),
    ]
    args = [u, u, u, conv_w, conv_b, gw, gb, lam]
    scratch = [
        pltpu.VMEM((tc + 2 * SUBLANES, r), F32),
        pltpu.VMEM((tc, r), F32),
        pltpu.VMEM((tc, r), F32),
        pltpu.VMEM((SUBLANES, r), F32),
    ]
    with_out = extra is not None
    if with_out:
        hf, gg, xall, mods, wout = extra
        d = xall.shape[1]
        in_specs += [
            pl.BlockSpec((tc, r), cur_map),
            pl.BlockSpec((tc, r), cur_map),
            pl.BlockSpec((tc, d), cur_map),
            pl.BlockSpec((1, 6, d), lambda b, j: (jnp.where(j < nc, nb, b), 0, 0)),
            _const_spec((r, d)),
        ]
        args += [hf, gg, xall, mods, wout]
        scratch.append(pltpu.VMEM((tc, r), F32))
        out_dim = d
    else:
        out_dim = r
    return pl.pallas_call(
        functools.partial(_scan_kernel, reverse=reverse, with_out=with_out, tc=tc, nc=nc, nl=nl),
        grid=(nb, nc + nl),
        in_specs=in_specs,
        out_specs=pl.BlockSpec((tc, out_dim), cur_map),
        out_shape=jax.ShapeDtypeStruct((n, out_dim), F32),
        scratch_shapes=scratch,
        compiler_params=_cparams(("arbitrary", "arbitrary")),
        name="lru_scan_bwd" if reverse else "lru_scan_fwd",
    )(*args)


def _ffn_kernel(*refs, with_attn, final_norm, n_chunks):
    if with_attn:
        x_ref, o_ref, wo_ref, mod_ref, g_ref, wg_ref, wu_ref, wd_ref, fg_ref, out_ref = refs
    else:
        x_ref, mod_ref, g_ref, wg_ref, wu_ref, wd_ref, fg_ref, out_ref = refs
    x = x_ref[...]
    if with_attn:
        x = x + mod_ref[0, 2:3, :] * jnp.dot(o_ref[...], wo_ref[...], preferred_element_type=F32)
    h = _norm_mod(x, g_ref[...], mod_ref[0, 3:4, :], mod_ref[0, 4:5, :]).astype(BF16)
    hid = wg_ref.shape[1]
    ck = hid // n_chunks
    acc = jnp.zeros(x.shape, F32)
    for c in range(n_chunks):
        sl = slice(c * ck, (c + 1) * ck)
        gate = jnp.dot(h, wg_ref[:, sl], preferred_element_type=F32)
        up = jnp.dot(h, wu_ref[:, sl], preferred_element_type=F32)
        act = ((gate * jax.nn.sigmoid(gate)) * up).astype(BF16)
        acc = acc + jnp.dot(act, wd_ref[sl, :], preferred_element_type=F32)
    x = x + mod_ref[0, 5:6, :] * acc
    if final_norm:
        x = _rmsnorm(x, fg_ref[...])
    out_ref[...] = x


def _ffn(xall, mods, g, wg, wu, wd, fg, *, nb, n_ctx_rows, seq, tm, lat_only, attn=None, final_norm=False):
    n, d = xall.shape
    hid = wg.shape[1]
    nct = n_ctx_rows // tm
    tpb = seq // tm
    off = nct if lat_only else 0
    n_out = n - n_ctx_rows if lat_only else n

    def mod_map(i):
        ii = i + off
        return (jnp.where(ii < nct, nb, (ii - nct) // tpb), 0, 0)

    in_specs = [pl.BlockSpec((tm, d), lambda i: (i + off, 0))]
    args = [xall]
    if attn is not None:
        o, wo = attn
        in_specs += [pl.BlockSpec((tm, o.shape[1]), lambda i: (i, 0)), _const_spec(wo.shape)]
        args += [o, wo]
    in_specs += [
        pl.BlockSpec((1, 6, d), mod_map),
        _const_spec((1, d)),
        pl.BlockSpec((d, hid), lambda i: (0, 0), pipeline_mode=pl.Buffered(1)),
        pl.BlockSpec((d, hid), lambda i: (0, 0), pipeline_mode=pl.Buffered(1)),
        pl.BlockSpec((hid, d), lambda i: (0, 0), pipeline_mode=pl.Buffered(1)),
        _const_spec((1, d)),
    ]
    args += [mods, g, wg, wu, wd, fg]
    return pl.pallas_call(
        functools.partial(_ffn_kernel, with_attn=attn is not None, final_norm=final_norm, n_chunks=2),
        grid=(n_out // tm,),
        in_specs=in_specs,
        out_specs=pl.BlockSpec((tm, d), lambda i: (i, 0)),
        out_shape=jax.ShapeDtypeStruct((n_out, d), F32),
        compiler_params=_cparams(("arbitrary",)),
        name="ffn_attn" if attn is not None else "ffn",
    )(*args)


def _mla_proj_kernel(x_ref, g_ref, mod_ref, win_ref, qg_ref, kvg_ref, wq_ref, wk_ref, wvt_ref, cos_ref, sin_ref,
                     q_ref, k_ref, v_ref, *, nct):
    i = pl.program_id(0)
    is_ctx = i < nct
    h = _norm_mod(x_ref[...], g_ref[...], mod_ref[0, 0:1, :], mod_ref[0, 1:2, :]).astype(BF16)
    c_all = jnp.dot(h, win_ref[...], preferred_element_type=F32)
    cosv = jnp.where(is_ctx, 1.0, cos_ref[...])
    sinv = jnp.where(is_ctx, 0.0, sin_ref[...])
    kv_lo = Q_LORA + KV_LORA
    k_rope = (c_all[:, kv_lo:kv_lo + 128] * cosv + c_all[:, kv_lo + 128:kv_lo + 256] * sinv).astype(BF16)
    ckv = _rmsnorm(c_all[:, Q_LORA:kv_lo], kvg_ref[...])
    k_nope = jnp.dot(ckv.astype(BF16), wk_ref[...], preferred_element_type=F32)
    v_t = jnp.dot(wvt_ref[...], ckv.T.astype(BF16), preferred_element_type=F32)
    for hd in range(MLA_HEADS):
        sl = slice(hd * 128, (hd + 1) * 128)
        k_ref[0, hd, :, 0:QK_NOPE] = k_nope[:, sl].astype(BF16)
        k_ref[0, hd, :, QK_NOPE:HEAD_PAD] = k_rope
        v_ref[0, hd] = v_t[sl, :].astype(BF16)

    @pl.when(jnp.logical_not(is_ctx))
    def _():
        cq = _rmsnorm(c_all[:, :Q_LORA], qg_ref[...]).astype(BF16)
        q_all = jnp.dot(cq, wq_ref[...], preferred_element_type=F32) * Q_PRESCALE
        hw = MLA_HEADS * 128
        for hd in range(MLA_HEADS):
            sl = slice(hd * 128, (hd + 1) * 128)
            q_ref[0, hd, :, 0:QK_NOPE] = q_all[:, sl].astype(BF16)
            q_rope = q_all[:, hw + hd * 128:hw + (hd + 1) * 128] * cos_ref[...] \
                + q_all[:, 2 * hw + hd * 128:2 * hw + (hd + 1) * 128] * sin_ref[...]
            q_ref[0, hd, :, QK_NOPE:HEAD_PAD] = q_rope.astype(BF16)


def _mla_proj(xall, g, mods, win, qg, kvg, wq, wk, wvt, cos_t, sin_t, *, nb, ctx_len, seq, tm):
    n, d = xall.shape
    nct = nb * ctx_len // tm
    cpb = ctx_len // tm
    tpb = seq // tm
    t_all = ctx_len + seq

    def lat(i):
        return jnp.maximum(i - nct, 0)

    def mod_map(i):
        return (jnp.where(i < nct, nb, lat(i) // tpb), 0, 0)

    def kv_map(i):
        b = jnp.where(i < nct, i // cpb, lat(i) // tpb)
        t = jnp.where(i < nct, i % cpb, cpb + lat(i) % tpb)
        return (b, 0, t, 0)

    def vt_map(i):
        b, _, t, _ = kv_map(i)
        return (b, 0, 0, t)

    def q_map(i):
        return (lat(i) // tpb, 0, lat(i) % tpb, 0)

    def tab_map(i):
        return (lat(i) % tpb, 0)

    return pl.pallas_call(
        functools.partial(_mla_proj_kernel, nct=nct),
        grid=(n // tm,),
        in_specs=[
            pl.BlockSpec((tm, d), lambda i: (i, 0)),
            _const_spec((1, d)),
            pl.BlockSpec((1, 6, d), mod_map),
            _const_spec(win.shape),
            _const_spec((1, Q_LORA)),
            _const_spec((1, KV_LORA)),
            _const_spec(wq.shape),
            _const_spec(wk.shape),
            _const_spec(wvt.shape),
            pl.BlockSpec((tm, 128), tab_map),
            pl.BlockSpec((tm, 128), tab_map),
        ],
        out_specs=[
            pl.BlockSpec((1, MLA_HEADS, tm, HEAD_PAD), q_map),
            pl.BlockSpec((1, MLA_HEADS, tm, HEAD_PAD), kv_map),
            pl.BlockSpec((1, MLA_HEADS, V_HEAD, tm), vt_map),
        ],
        out_shape=[
            jax.ShapeDtypeStruct((nb, MLA_HEADS, seq, HEAD_PAD), BF16),
            jax.ShapeDtypeStruct((nb, MLA_HEADS, t_all, HEAD_PAD), BF16),
            jax.ShapeDtypeStruct((nb, MLA_HEADS, V_HEAD, t_all), BF16),
        ],
        compiler_params=_cparams(("arbitrary",)),
        name="mla_proj",
    )(xall, g, mods, win, qg, kvg, wq, wk, wvt, cos_t, sin_t)


def _attn_kernel(q_ref, k_ref, vt_ref, o_ref, s0_ref, s1_ref, *, tq, kc):
    t_all = k_ref.shape[2]
    nkc = t_all // kc
    nq = q_ref.shape[2] // tq
    s_refs = (s0_ref, s1_ref)

    def group_reduce(x, op):
        return op(x.reshape(kc // SUBLANES, SUBLANES, tq), axis=0)

    bounds = [round(g * nkc / QK_GROUPS) for g in range(QK_GROUPS + 1)]
    groups = [(a, b) for a, b in zip(bounds[:-1], bounds[1:]) if b > a]

    def scores(qv, c0, c1, s_ref):
        st = lax.dot_general(k_ref[0, 0, c0 * kc:c1 * kc, :], qv, (((1,), (1,)), ((), ())),
                             preferred_element_type=F32)
        s_ref[c0 * kc:c1 * kc, :] = st
        return jnp.max(st.reshape((c1 - c0) * kc // SUBLANES, SUBLANES, tq), axis=0)

    def q_tile(i):
        return q_ref[0, 0, pl.ds(pl.multiple_of(i * tq, tq), tq), :]

    def tile_step(i, slot, m8, with_next):
        s_cur, s_nxt = s_refs[slot], s_refs[1 - slot]
        m_next = jnp.full((SUBLANES, tq), -jnp.inf, F32)
        if with_next:
            q_next = q_tile(i + 1)
        m = jnp.max(m8, axis=0, keepdims=True)
        l8 = jnp.zeros((SUBLANES, tq), F32)
        acc = jnp.zeros((V_HEAD, tq), F32)

        def pv(acc, c, pb):
            return acc + jnp.dot(vt_ref[0, 0, :, c * kc:(c + 1) * kc], pb, preferred_element_type=F32)

        p_prev = None
        for c0, c1 in groups:
            if with_next:
                m_next = jnp.maximum(m_next, scores(q_next, c0, c1, s_nxt))
            for c in range(c0, c1):
                if p_prev is not None:
                    acc = pv(acc, c - 1, p_prev)
                p = jnp.exp2(s_cur[c * kc:(c + 1) * kc, :] - m)
                l8 = l8 + group_reduce(p, jnp.sum)
                p_prev = p.astype(BF16)
        acc = pv(acc, nkc - 1, p_prev)
        l = jnp.sum(l8, axis=0, keepdims=True)
        rows = pl.ds(pl.multiple_of(i * tq, tq), tq)
        o_ref[0, rows, :] = (acc / l).T.astype(o_ref.dtype)
        return m_next

    m8 = scores(q_tile(0), 0, nkc, s0_ref)

    def pair(j, m8):
        m8 = tile_step(2 * j, 0, m8, True)
        return tile_step(2 * j + 1, 1, m8, True)

    n_pairs = (nq - 1) // 2
    m8 = lax.fori_loop(0, n_pairs, pair, m8)
    if (nq - 1) % 2:
        m8 = tile_step(nq - 2, 0, m8, True)
    tile_step(nq - 1, (nq - 1) % 2, m8, False)


def _attention(q, k, v, *, tq):
    nb, nh, seq, dp = q.shape
    t_all = k.shape[2]
    kc = int(np.gcd(256, t_all))
    return pl.pallas_call(
        functools.partial(_attn_kernel, tq=tq, kc=kc),
        grid=(nb, nh),
        scratch_shapes=[pltpu.VMEM((t_all, tq), F32), pltpu.VMEM((t_all, tq), F32)],
        in_specs=[
            pl.BlockSpec((1, 1, seq, dp), lambda b, h: (b, h, 0, 0)),
            pl.BlockSpec((1, 1, t_all, dp), lambda b, h: (b, h, 0, 0)),
            pl.BlockSpec((1, 1, V_HEAD, t_all), lambda b, h: (b, h, 0, 0)),
        ],
        out_specs=pl.BlockSpec((1, seq, V_HEAD), lambda b, h: (b, 0, h)),
        out_shape=jax.ShapeDtypeStruct((nb, seq, nh * V_HEAD), BF16),
        compiler_params=_cparams(("arbitrary", "arbitrary")),
        name="mla_attention",
    )(q, k, v)


def _rope_perm():
    quarter = QK_ROPE // 4
    a_idx = np.concatenate([np.arange(0, quarter), np.arange(2 * quarter, 3 * quarter)])
    b_idx = np.concatenate([np.arange(quarter, 2 * quarter), np.arange(3 * quarter, 4 * quarter)])
    return a_idx, b_idx


def _rope_cols(w):
    a_idx, b_idx = _rope_perm()
    z = jnp.zeros((w.shape[0], 128 - QK_ROPE), w.dtype)
    plain = jnp.concatenate([w[:, a_idx], w[:, b_idx], z], axis=1)
    swapped = jnp.concatenate([-w[:, b_idx], w[:, a_idx], z], axis=1)
    return plain, swapped


def _rope_tables(seq):
    rows = seq // GRID_W
    row_ids = np.repeat(np.arange(rows, dtype=np.float32), GRID_W)
    col_ids = np.tile(np.arange(GRID_W, dtype=np.float32), rows)
    axis_dim = QK_ROPE // 2
    expo = (np.arange(0, axis_dim, 2, dtype=np.float32) / np.float32(axis_dim)).astype(np.float32)
    inv_freq = (np.float32(1.0) / np.power(np.float32(ROPE_THETA), expo)).astype(np.float32)
    ang_r = (row_ids[:, None] * inv_freq).astype(np.float32)
    ang_c = (col_ids[:, None] * inv_freq).astype(np.float32)
    pad = np.zeros((seq, 128 - QK_ROPE), np.float32)
    cos_t = np.concatenate([np.cos(ang_r), np.cos(ang_c), np.cos(ang_r), np.cos(ang_c), pad + 1.0], axis=1)
    sin_t = np.concatenate([np.sin(ang_r), np.sin(ang_c), np.sin(ang_r), np.sin(ang_c), pad], axis=1)
    return jnp.asarray(cos_t, F32), jnp.asarray(sin_t, F32)


def kernel(x, c, ctx, c_ctx, ada_w, ada_b, norm_mix_g, norm_ffn_g, ffn_w_in, ffn_w_out, lru_w_in, lru_conv_w,
           lru_conv_b, lru_gate_w, lru_gate_b, lru_lambda, lru_w_out, mla_w_in, mla_q_norm_g, mla_kv_norm_g,
           mla_w_uq, mla_w_ukv, mla_w_o, final_norm_g):
    nb, seq, d = x.shape
    ctx_len = ctx.shape[1]
    depth = ada_w.shape[0]
    assert depth == 2 and d == D_MODEL and nb + 1 <= MOD_ROWS
    n_ctx_rows = nb * ctx_len
    tile = lambda cap: int(np.gcd(np.gcd(cap, n_ctx_rows), seq))

    xall = jnp.concatenate([ctx.reshape(n_ctx_rows, d), x.reshape(nb * seq, d)], axis=0)

    cv = jnp.concatenate([c, c_ctx[None, :], jnp.zeros((MOD_ROWS - nb - 1, d), F32)], axis=0)
    mods = _ada_mods(cv, ada_w, ada_b).reshape(depth, MOD_ROWS, 6, d)

    tm = tile(1024)
    gg, u = _lru_in(xall, norm_mix_g[0][None], mods[0], lru_w_in[0].astype(BF16),
                    nb=nb, n_ctx_rows=n_ctx_rows, seq=seq, tm=tm)
    tc = int(np.gcd(np.gcd(256, ctx_len), seq))
    gw = lru_gate_w[0]
    gw = jnp.concatenate([gw[:, 0], gw[:, 1]], axis=-1).astype(BF16)
    scan_args = dict(nb=nb, ctx_len=ctx_len, seq=seq, tc=tc)
    conv_w, conv_b = lru_conv_w[0], lru_conv_b[0][None]
    hf = _lru_scan(u, conv_w, conv_b, gw[0], lru_gate_b[0, 0], lru_lambda[0, 0][None], reverse=False, **scan_args)
    x1 = _lru_scan(u, conv_w, conv_b, gw[1], lru_gate_b[0, 1], lru_lambda[0, 1][None], reverse=True,
                   extra=(hf, gg, xall, mods[0], lru_w_out[0].astype(BF16)), **scan_args)
    tf = tile(512)
    wg0, wu0 = ffn_w_in[0][:, :FFN_HIDDEN].astype(BF16), ffn_w_in[0][:, FFN_HIDDEN:].astype(BF16)
    x2 = _ffn(x1, mods[0], norm_ffn_g[0][None], wg0, wu0, ffn_w_out[0].astype(BF16), final_norm_g[None],
              nb=nb, n_ctx_rows=n_ctx_rows, seq=seq, tm=tf, lat_only=False)

    w_in = mla_w_in[0]
    kr_plain, kr_swap = _rope_cols(w_in[:, Q_LORA + KV_LORA:])
    win_p = jnp.concatenate([w_in[:, :Q_LORA + KV_LORA], kr_plain, kr_swap], axis=1).astype(BF16)
    wq3 = mla_w_uq[0].reshape(Q_LORA, MLA_HEADS, QK_NOPE + QK_ROPE)
    q_nope = wq3[:, :, :QK_NOPE].reshape(Q_LORA, MLA_HEADS * QK_NOPE)
    q_pairs = [_rope_cols(wq3[:, hd, QK_NOPE:]) for hd in range(MLA_HEADS)]
    wq_all = jnp.concatenate([q_nope] + [p[0] for p in q_pairs] + [p[1] for p in q_pairs], axis=1).astype(BF16)
    cos_t, sin_t = _rope_tables(seq)
    tp = int(np.gcd(np.gcd(256, ctx_len), seq))
    wkv3 = mla_w_ukv[0].reshape(KV_LORA, MLA_HEADS, QK_NOPE + V_HEAD)
    wk = wkv3[:, :, :QK_NOPE].reshape(KV_LORA, MLA_HEADS * QK_NOPE).astype(BF16)
    wvt = wkv3[:, :, QK_NOPE:].reshape(KV_LORA, MLA_HEADS * V_HEAD).T.astype(BF16)
    q, k, vt = _mla_proj(x2, norm_mix_g[1][None], mods[1], win_p, mla_q_norm_g[0][None], mla_kv_norm_g[0][None],
                         wq_all, wk, wvt, cos_t, sin_t, nb=nb, ctx_len=ctx_len, seq=seq, tm=tp)
    o = _attention(q, k, vt, tq=int(np.gcd(512, seq)))
    wg1, wu1 = ffn_w_in[1][:, :FFN_HIDDEN].astype(BF16), ffn_w_in[1][:, FFN_HIDDEN:].astype(BF16)
    out = _ffn(x2, mods[1], norm_ffn_g[1][None], wg1, wu1, ffn_w_out[1].astype(BF16), final_norm_g[None],
               nb=nb, n_ctx_rows=n_ctx_rows, seq=seq, tm=tf, lat_only=True,
               attn=(o.reshape(nb * seq, MLA_HEADS * V_HEAD), mla_w_o[0].astype(BF16)), final_norm=True)
    return out.reshape(nb, seq, d)
```

```python
import functools

import numpy as np
import jax
import jax.numpy as jnp
from jax import lax
from jax.experimental import pallas as pl
from jax.experimental.pallas import tpu as pltpu

F32 = jnp.float32
BF16 = jnp.bfloat16

D_MODEL = 1024
GRID_W = 64
NORM_EPS = 1e-6
FFN_HIDDEN = 2816
LRU_WIDTH = D_MODEL
LRU_BLOCK = 256
LRU_BLOCKS = LRU_WIDTH // LRU_BLOCK
CONV_W = 4
RG_C = 8.0
MLA_HEADS = 8
Q_LORA = 384
KV_LORA = 256
QK_NOPE = 128
QK_ROPE = 64
V_HEAD = 128
SM_SCALE = (QK_NOPE + QK_ROPE) ** -0.5
Q_PRESCALE = SM_SCALE * float(np.log2(np.e))
ROPE_THETA = 10000.0

MOD_ROWS = 8
SUBLANES = 8
MXU_DIM = 256
HEAD_PAD = 256
VMEM_LIMIT_MB = 56
QK_GROUPS = 4
TINY = 1e-30


def _cparams(sem, vmem_mb=VMEM_LIMIT_MB):
    return pltpu.CompilerParams(dimension_semantics=sem, vmem_limit_bytes=vmem_mb << 20)


def _const_spec(shape):
    nd = len(shape)
    return pl.BlockSpec(shape, lambda *_: (0,) * nd)


def _norm_mod(x, g, shift, scale):
    ms = jnp.mean(x * x, axis=-1, keepdims=True)
    y = (x * lax.rsqrt(ms + NORM_EPS)) * g
    return y * (1.0 + scale) + shift


def _rmsnorm(x, g):
    ms = jnp.mean(x * x, axis=-1, keepdims=True)
    return (x * lax.rsqrt(ms + NORM_EPS)) * g


def _ada_kernel(cv_ref, w_ref, b_ref, o_ref):
    cv = cv_ref[...]
    s = cv * jax.nn.sigmoid(cv)
    o_ref[0] = jnp.dot(s.astype(BF16), w_ref[0].astype(BF16), preferred_element_type=F32) + b_ref[0]


def _ada_mods(cv, ada_w, ada_b):
    depth, d, n = ada_w.shape
    tn = 1536
    return pl.pallas_call(
        _ada_kernel,
        grid=(depth, n // tn),
        in_specs=[
            pl.BlockSpec((MOD_ROWS, d), lambda l, j: (0, 0)),
            pl.BlockSpec((1, d, tn), lambda l, j: (l, 0, j)),
            pl.BlockSpec((1, 1, tn), lambda l, j: (l, 0, j)),
        ],
        out_specs=pl.BlockSpec((1, MOD_ROWS, tn), lambda l, j: (l, 0, j)),
        out_shape=jax.ShapeDtypeStruct((depth, MOD_ROWS, n), F32),
        compiler_params=_cparams(("arbitrary", "arbitrary")),
        name="ada_mods",
    )(cv, ada_w, ada_b.reshape(depth, 1, n))


def _lru_in_kernel(x_ref, g_ref, mod_ref, w_ref, gg_ref, u_ref):
    h = _norm_mod(x_ref[...], g_ref[...], mod_ref[0, 0:1, :], mod_ref[0, 1:2, :])
    y = jnp.dot(h.astype(BF16), w_ref[...], preferred_element_type=F32)
    r = LRU_WIDTH
    gg_ref[...] = jax.nn.gelu(y[:, :r])
    u_ref[...] = y[:, r:]


def _lru_in(xall, g, mods, w, *, nb, n_ctx_rows, seq, tm):
    n, d = xall.shape
    r = LRU_WIDTH
    nct = n_ctx_rows // tm
    tpb = seq // tm

    def mod_map(i):
        return (jnp.where(i < nct, nb, (i - nct) // tpb), 0, 0)

    return pl.pallas_call(
        _lru_in_kernel,
        grid=(n // tm,),
        in_specs=[
            pl.BlockSpec((tm, d), lambda i: (i, 0)),
            _const_spec((1, d)),
            pl.BlockSpec((1, 6, d), mod_map),
            _const_spec((d, 2 * r)),
        ],
        out_specs=[pl.BlockSpec((tm, r), lambda i: (i, 0)), pl.BlockSpec((tm, r), lambda i: (i, 0))],
        out_shape=[jax.ShapeDtypeStruct((n, r), F32), jax.ShapeDtypeStruct((n, r), F32)],
        compiler_params=_cparams(("arbitrary",)),
        name="lru_in",
    )(xall, g, mods, w)


def _lru_coeffs(xc, gw, gbh, lam, a_s, b_s):
    xcb = xc.astype(BF16)
    neg = -lam[...]
    sp = jnp.maximum(neg, 0.0) + jnp.log1p(jnp.exp(-jnp.abs(neg)))
    c2 = (-0.5 * RG_C) * sp
    for n in range(LRU_BLOCKS):
        sl = slice(n * LRU_BLOCK, (n + 1) * LRU_BLOCK)
        pre = jnp.dot(xcb[:, sl], gw[n], preferred_element_type=F32)
        t_r = jnp.tanh(pre[:, :LRU_BLOCK] + gbh[0:1, sl])
        t_i = jnp.tanh(pre[:, LRU_BLOCK:] + gbh[1:2, sl])
        a = jnp.exp(c2[:, sl] * t_r + c2[:, sl])
        gap = 1.0 - a * a
        mult = gap * lax.rsqrt(jnp.maximum(gap, TINY))
        a_s[:, sl] = a
        b_s[:, sl] = (mult * (0.5 * t_i + 0.5)) * xc[:, sl]


def _lru_chunk_scan(a_s, b_s, h_s, hc, *, reverse, tc):
    r = LRU_WIDTH
    row = lax.broadcasted_iota(jnp.int32, (SUBLANES, r), 0)
    ngroups = tc // SUBLANES

    def group(gi, hprev):
        g0 = (ngroups - 1 - gi) if reverse else gi
        rows = pl.ds(pl.multiple_of(g0 * SUBLANES, SUBLANES), SUBLANES)
        av = a_s[rows, :]
        bv = b_s[rows, :]
        for s in (1, 2, 4):
            if reverse:
                keep = row < (SUBLANES - s)
                shift = SUBLANES - s
            else:
                keep = row >= s
                shift = s
            a_sh = jnp.where(keep, pltpu.roll(av, shift, 0), 1.0)
            b_sh = jnp.where(keep, pltpu.roll(bv, shift, 0), 0.0)
            bv = av * b_sh + bv
            av = av * a_sh
        hrows = av * hprev + bv
        h_s[rows, :] = hrows
        edge = hrows[0:1, :] if reverse else hrows[SUBLANES - 1:SUBLANES, :]
        return jnp.broadcast_to(edge, (SUBLANES, r))

    hc[...] = lax.fori_loop(0, ngroups, group, hc[...], unroll=2)


def _scan_fwd_kernel(ucur, uprev, unext, cw, cb, gw, gbh, lam, hf, xc_out, ext, a_s, b_s, hc, *, tc, nc, nl):
    j = pl.program_id(1)
    in_ctx = j < nc
    pos = jnp.where(in_ctx, j, j - nc)
    first = pos == 0
    last = pos == jnp.where(in_ctx, nc, nl) - 1

    @pl.when(j == 0)
    def _():
        hc[...] = jnp.zeros_like(hc)

    n_ext = tc + 2 * SUBLANES
    ext[0:SUBLANES, :] = jnp.where(first, 0.0, uprev[...])
    ext[SUBLANES:SUBLANES + tc, :] = ucur[...]
    ext[SUBLANES + tc:n_ext, :] = jnp.where(last, 0.0, unext[...])
    e = ext[...]
    mid = slice(SUBLANES, SUBLANES + tc)
    xc = cb[...] + cw[0:1, :] * pltpu.roll(e, 1, 0)[mid] + cw[1:2, :] * e[mid] \
        + cw[2:3, :] * pltpu.roll(e, n_ext - 1, 0)[mid] + cw[3:4, :] * pltpu.roll(e, n_ext - 2, 0)[mid]
    xc_out[...] = xc
    _lru_coeffs(xc, gw, gbh, lam, a_s, b_s)
    _lru_chunk_scan(a_s, b_s, hf, hc, reverse=False, tc=tc)


def _scan_bwd_kernel(xc, gw, gbh, lam, hf, gg, x, mod, wout, out, a_s, b_s, hc, hb_s, *, tc):
    @pl.when(pl.program_id(1) == 0)
    def _():
        hc[...] = jnp.zeros_like(hc)

    _lru_coeffs(xc[...], gw, gbh, lam, a_s, b_s)
    _lru_chunk_scan(a_s, b_s, hb_s, hc, reverse=True, tc=tc)
    z = ((hf[...] + hb_s[...]) * gg[...]).astype(BF16)
    out[...] = x[...] + mod[0, 2:3, :] * jnp.dot(z, wout[...], preferred_element_type=F32)


def _lru_scans(u, conv_w, conv_b, gw, gbh, lam, gg, xall, mods, wout, *, nb, ctx_len, seq, tc):
    n, r = u.shape
    d = xall.shape[1]
    nc = ctx_len // tc
    nl = seq // tc
    blk8 = tc // SUBLANES

    def cur_blk(b, j, reverse):
        jc = (nc - 1 - j) if reverse else j
        jl = (nl - 1 - (j - nc)) if reverse else (j - nc)
        return jnp.where(j < nc, b * nc + jc, nb * nc + b * nl + jl)

    fwd_map = lambda b, j: (cur_blk(b, j, False), 0)
    bwd_map = lambda b, j: (cur_blk(b, j, True), 0)
    prev_map = lambda b, j: (jnp.maximum(cur_blk(b, j, False) * blk8 - 1, 0), 0)
    next_map = lambda b, j: (jnp.minimum((cur_blk(b, j, False) + 1) * blk8, n // SUBLANES - 1), 0)
    gate_specs = [
        _const_spec((LRU_BLOCKS, LRU_BLOCK, 2 * LRU_BLOCK)),
        _const_spec((2, r)),
        _const_spec((1, r)),
    ]
    chunk = pltpu.VMEM((tc, r), F32)
    carry = pltpu.VMEM((SUBLANES, r), F32)
    hf, xc = pl.pallas_call(
        functools.partial(_scan_fwd_kernel, tc=tc, nc=nc, nl=nl),
        grid=(nb, nc + nl),
        in_specs=[
            pl.BlockSpec((tc, r), fwd_map),
            pl.BlockSpec((SUBLANES, r), prev_map),
            pl.BlockSpec((SUBLANES, r), next_map),
            _const_spec((CONV_W, r)),
            _const_spec((1, r)),
        ] + gate_specs,
        out_specs=[pl.BlockSpec((tc, r), fwd_map), pl.BlockSpec((tc, r), fwd_map)],
        out_shape=[jax.ShapeDtypeStruct((n, r), F32), jax.ShapeDtypeStruct((n, r), F32)],
        scratch_shapes=[pltpu.VMEM((tc + 2 * SUBLANES, r), F32), chunk, chunk, carry],
        compiler_params=_cparams(("arbitrary", "arbitrary")),
        name="lru_scan_fwd",
    )(u, u, u, conv_w, conv_b, gw[0], gbh[0], lam[0])
    return pl.pallas_call(
        functools.partial(_scan_bwd_kernel, tc=tc),
        grid=(nb, nc + nl),
        in_specs=[pl.BlockSpec((tc, r), bwd_map)] + gate_specs + [
            pl.BlockSpec((tc, r), bwd_map),
            pl.BlockSpec((tc, r), bwd_map),
            pl.BlockSpec((tc, d), bwd_map),
            pl.BlockSpec((1, 6, d), lambda b, j: (jnp.where(j < nc, nb, b), 0, 0)),
            _const_spec((r, d)),
        ],
        out_specs=pl.BlockSpec((tc, d), bwd_map),
        out_shape=jax.ShapeDtypeStruct((n, d), F32),
        scratch_shapes=[chunk, chunk, carry, chunk],
        compiler_params=_cparams(("arbitrary", "arbitrary")),
        name="lru_scan_bwd",
    )(xc, gw[1], gbh[1], lam[1], hf, gg, xall, mods, wout)


def _ffn_kernel(*refs, with_attn, final_norm, n_chunks):
    if with_attn:
        x_ref, o_ref, wo_ref, mod_ref, g_ref, wg_ref, wu_ref, wd_ref, fg_ref, out_ref = refs
    else:
        x_ref, mod_ref, g_ref, wg_ref, wu_ref, wd_ref, fg_ref, out_ref = refs
    x = x_ref[...]
    if with_attn:
        x = x + mod_ref[0, 2:3, :] * jnp.dot(o_ref[...], wo_ref[...], preferred_element_type=F32)
    h = _norm_mod(x, g_ref[...], mod_ref[0, 3:4, :], mod_ref[0, 4:5, :]).astype(BF16)
    hid = wg_ref.shape[1]
    n_mxu = hid // MXU_DIM
    edges = [round(c * n_mxu / n_chunks) * MXU_DIM for c in range(n_chunks)] + [hid]
    acc = jnp.zeros(x.shape, F32)
    for c in range(n_chunks):
        sl = slice(edges[c], edges[c + 1])
        gate = jnp.dot(h, wg_ref[:, sl], preferred_element_type=F32)
        up = jnp.dot(h, wu_ref[:, sl], preferred_element_type=F32)
        act = ((gate * jax.nn.sigmoid(gate)) * up).astype(BF16)
        acc = acc + jnp.dot(act, wd_ref[sl, :], preferred_element_type=F32)
    x = x + mod_ref[0, 5:6, :] * acc
    if final_norm:
        x = _rmsnorm(x, fg_ref[...])
    out_ref[...] = x


def _ffn(xall, mods, g, wg, wu, wd, fg, *, nb, n_ctx_rows, seq, tm, lat_only, attn=None, final_norm=False):
    n, d = xall.shape
    hid = wg.shape[1]
    nct = n_ctx_rows // tm
    tpb = seq // tm
    off = nct if lat_only else 0
    n_out = n - n_ctx_rows if lat_only else n

    def mod_map(i):
        ii = i + off
        return (jnp.where(ii < nct, nb, (ii - nct) // tpb), 0, 0)

    in_specs = [pl.BlockSpec((tm, d), lambda i: (i + off, 0))]
    args = [xall]
    if attn is not None:
        o, wo = attn
        in_specs += [pl.BlockSpec((tm, o.shape[1]), lambda i: (i, 0)), _const_spec(wo.shape)]
        args += [o, wo]
    in_specs += [
        pl.BlockSpec((1, 6, d), mod_map),
        _const_spec((1, d)),
        pl.BlockSpec((d, hid), lambda i: (0, 0), pipeline_mode=pl.Buffered(1)),
        pl.BlockSpec((d, hid), lambda i: (0, 0), pipeline_mode=pl.Buffered(1)),
        pl.BlockSpec((hid, d), lambda i: (0, 0), pipeline_mode=pl.Buffered(1)),
        _const_spec((1, d)),
    ]
    args += [mods, g, wg, wu, wd, fg]
    return pl.pallas_call(
        functools.partial(_ffn_kernel, with_attn=attn is not None, final_norm=final_norm, n_chunks=2),
        grid=(n_out // tm,),
        in_specs=in_specs,
        out_specs=pl.BlockSpec((tm, d), lambda i: (i, 0)),
        out_shape=jax.ShapeDtypeStruct((n_out, d), F32),
        compiler_params=_cparams(("arbitrary",)),
        name="ffn_attn" if attn is not None else "ffn",
    )(*args)


def _mla_proj_kernel(x_ref, g_ref, mod_ref, win_ref, qg_ref, kvg_ref, wq_ref, wk_ref, wvt_ref, cos_ref, sin_ref,
                     q_ref, k_ref, v_ref, *, nct):
    i = pl.program_id(0)
    is_ctx = i < nct
    h = _norm_mod(x_ref[...], g_ref[...], mod_ref[0, 0:1, :], mod_ref[0, 1:2, :]).astype(BF16)
    c_all = jnp.dot(h, win_ref[...], preferred_element_type=F32)
    cosv = jnp.where(is_ctx, 1.0, cos_ref[...])
    sinv = jnp.where(is_ctx, 0.0, sin_ref[...])
    kv_lo = Q_LORA + KV_LORA
    k_rope = (c_all[:, kv_lo:kv_lo + 128] * cosv + c_all[:, kv_lo + 128:kv_lo + 256] * sinv).astype(BF16)
    ckv = _rmsnorm(c_all[:, Q_LORA:kv_lo], kvg_ref[...])
    k_nope = jnp.dot(ckv.astype(BF16), wk_ref[...], preferred_element_type=F32)
    v_t = jnp.dot(wvt_ref[...], ckv.T.astype(BF16), preferred_element_type=F32)
    for hd in range(MLA_HEADS):
        sl = slice(hd * 128, (hd + 1) * 128)
        k_ref[0, hd, :, 0:QK_NOPE] = k_nope[:, sl].astype(BF16)
        k_ref[0, hd, :, QK_NOPE:HEAD_PAD] = k_rope
        v_ref[0, hd] = v_t[sl, :].astype(BF16)

    @pl.when(jnp.logical_not(is_ctx))
    def _():
        cq = _rmsnorm(c_all[:, :Q_LORA], qg_ref[...]).astype(BF16)
        q_all = jnp.dot(cq, wq_ref[...], preferred_element_type=F32) * Q_PRESCALE
        hw = MLA_HEADS * 128
        for hd in range(MLA_HEADS):
            sl = slice(hd * 128, (hd + 1) * 128)
            q_ref[0, hd, :, 0:QK_NOPE] = q_all[:, sl].astype(BF16)
            q_rope = q_all[:, hw + hd * 128:hw + (hd + 1) * 128] * cos_ref[...] \
                + q_all[:, 2 * hw + hd * 128:2 * hw + (hd + 1) * 128] * sin_ref[...]
            q_ref[0, hd, :, QK_NOPE:HEAD_PAD] = q_rope.astype(BF16)


def _mla_proj(xall, g, mods, win, qg, kvg, wq, wk, wvt, cos_t, sin_t, *, nb, ctx_len, seq, tm):
    n, d = xall.shape
    nct = nb * ctx_len // tm
    cpb = ctx_len // tm
    tpb = seq // tm
    t_all = ctx_len + seq

    def lat(i):
        return jnp.maximum(i - nct, 0)

    def mod_map(i):
        return (jnp.where(i < nct, nb, lat(i) // tpb), 0, 0)

    def kv_map(i):
        b = jnp.where(i < nct, i // cpb, lat(i) // tpb)
        t = jnp.where(i < nct, i % cpb, cpb + lat(i) % tpb)
        return (b, 0, t, 0)

    def vt_map(i):
        b, _, t, _ = kv_map(i)
        return (b, 0, 0, t)

    def q_map(i):
        return (lat(i) // tpb, 0, lat(i) % tpb, 0)

    def tab_map(i):
        return (lat(i) % tpb, 0)

    return pl.pallas_call(
        functools.partial(_mla_proj_kernel, nct=nct),
        grid=(n // tm,),
        in_specs=[
            pl.BlockSpec((tm, d), lambda i: (i, 0)),
            _const_spec((1, d)),
            pl.BlockSpec((1, 6, d), mod_map),
            _const_spec(win.shape),
            _const_spec((1, Q_LORA)),
            _const_spec((1, KV_LORA)),
            _const_spec(wq.shape),
            _const_spec(wk.shape),
            _const_spec(wvt.shape),
            pl.BlockSpec((tm, 128), tab_map),
            pl.BlockSpec((tm, 128), tab_map),
        ],
        out_specs=[
            pl.BlockSpec((1, MLA_HEADS, tm, HEAD_PAD), q_map),
            pl.BlockSpec((1, MLA_HEADS, tm, HEAD_PAD), kv_map),
            pl.BlockSpec((1, MLA_HEADS, V_HEAD, tm), vt_map),
        ],
        out_shape=[
            jax.ShapeDtypeStruct((nb, MLA_HEADS, seq, HEAD_PAD), BF16),
            jax.ShapeDtypeStruct((nb, MLA_HEADS, t_all, HEAD_PAD), BF16),
            jax.ShapeDtypeStruct((nb, MLA_HEADS, V_HEAD, t_all), BF16),
        ],
        compiler_params=_cparams(("arbitrary",)),
        name="mla_proj",
    )(xall, g, mods, win, qg, kvg, wq, wk, wvt, cos_t, sin_t)


def _attn_kernel(q_ref, k_ref, vt_ref, o_ref, s0_ref, s1_ref, *, tq, kc):
    t_all = k_ref.shape[2]
    nkc = t_all // kc
    nq = q_ref.shape[2] // tq
    s_refs = (s0_ref, s1_ref)

    def group_reduce(x, op):
        return op(x.reshape(kc // SUBLANES, SUBLANES, tq), axis=0)

    bounds = [round(g * nkc / QK_GROUPS) for g in range(QK_GROUPS + 1)]
    groups = [(a, b) for a, b in zip(bounds[:-1], bounds[1:]) if b > a]

    def scores(qv, c0, c1, s_ref):
        st = lax.dot_general(k_ref[0, 0, c0 * kc:c1 * kc, :], qv, (((1,), (1,)), ((), ())),
                             preferred_element_type=F32)
        s_ref[c0 * kc:c1 * kc, :] = st
        return jnp.max(st.reshape((c1 - c0) * kc // SUBLANES, SUBLANES, tq), axis=0)

    def q_tile(i):
        return q_ref[0, 0, pl.ds(pl.multiple_of(i * tq, tq), tq), :]

    def tile_step(i, slot, m8, with_next):
        s_cur, s_nxt = s_refs[slot], s_refs[1 - slot]
        m_next = jnp.full((SUBLANES, tq), -jnp.inf, F32)
        if with_next:
            q_next = q_tile(i + 1)
        m = jnp.max(m8, axis=0, keepdims=True)
        l8 = jnp.zeros((SUBLANES, tq), F32)
        acc = jnp.zeros((V_HEAD, tq), F32)

        def pv(acc, c, pb):
            return acc + jnp.dot(vt_ref[0, 0, :, c * kc:(c + 1) * kc], pb, preferred_element_type=F32)

        p_prev = None
        for c0, c1 in groups:
            if with_next:
                m_next = jnp.maximum(m_next, scores(q_next, c0, c1, s_nxt))
            for c in range(c0, c1):
                if p_prev is not None:
                    acc = pv(acc, c - 1, p_prev)
                p = jnp.exp2(s_cur[c * kc:(c + 1) * kc, :] - m)
                l8 = l8 + group_reduce(p, jnp.sum)
                p_prev = p.astype(BF16)
        acc = pv(acc, nkc - 1, p_prev)
        l = jnp.sum(l8, axis=0, keepdims=True)
        rows = pl.ds(pl.multiple_of(i * tq, tq), tq)
        o_ref[0, rows, :] = (acc / l).T.astype(o_ref.dtype)
        return m_next

    m8 = scores(q_tile(0), 0, nkc, s0_ref)

    def pair(j, m8):
        m8 = tile_step(2 * j, 0, m8, True)
        return tile_step(2 * j + 1, 1, m8, True)

    n_pairs = (nq - 1) // 2
    m8 = lax.fori_loop(0, n_pairs, pair, m8)
    if (nq - 1) % 2:
        m8 = tile_step(nq - 2, 0, m8, True)
    tile_step(nq - 1, (nq - 1) % 2, m8, False)


def _attention(q, k, v, *, tq):
    nb, nh, seq, dp = q.shape
    t_all = k.shape[2]
    kc = int(np.gcd(256, t_all))
    return pl.pallas_call(
        functools.partial(_attn_kernel, tq=tq, kc=kc),
        grid=(nb, nh),
        scratch_shapes=[pltpu.VMEM((t_all, tq), F32), pltpu.VMEM((t_all, tq), F32)],
        in_specs=[
            pl.BlockSpec((1, 1, seq, dp), lambda b, h: (b, h, 0, 0)),
            pl.BlockSpec((1, 1, t_all, dp), lambda b, h: (b, h, 0, 0)),
            pl.BlockSpec((1, 1, V_HEAD, t_all), lambda b, h: (b, h, 0, 0)),
        ],
        out_specs=pl.BlockSpec((1, seq, V_HEAD), lambda b, h: (b, 0, h)),
        out_shape=jax.ShapeDtypeStruct((nb, seq, nh * V_HEAD), BF16),
        compiler_params=_cparams(("arbitrary", "arbitrary")),
        name="mla_attention",
    )(q, k, v)


def _rope_perm():
    quarter = QK_ROPE // 4
    a_idx = np.concatenate([np.arange(0, quarter), np.arange(2 * quarter, 3 * quarter)])
    b_idx = np.concatenate([np.arange(quarter, 2 * quarter), np.arange(3 * quarter, 4 * quarter)])
    return a_idx, b_idx


def _rope_cols(w):
    a_idx, b_idx = _rope_perm()
    z = jnp.zeros((w.shape[0], 128 - QK_ROPE), w.dtype)
    plain = jnp.concatenate([w[:, a_idx], w[:, b_idx], z], axis=1)
    swapped = jnp.concatenate([-w[:, b_idx], w[:, a_idx], z], axis=1)
    return plain, swapped


def _rope_tables(seq):
    rows = seq // GRID_W
    row_ids = np.repeat(np.arange(rows, dtype=np.float32), GRID_W)
    col_ids = np.tile(np.arange(GRID_W, dtype=np.float32), rows)
    axis_dim = QK_ROPE // 2
    expo = (np.arange(0, axis_dim, 2, dtype=np.float32) / np.float32(axis_dim)).astype(np.float32)
    inv_freq = (np.float32(1.0) / np.power(np.float32(ROPE_THETA), expo)).astype(np.float32)
    ang_r = (row_ids[:, None] * inv_freq).astype(np.float32)
    ang_c = (col_ids[:, None] * inv_freq).astype(np.float32)
    pad = np.zeros((seq, 128 - QK_ROPE), np.float32)
    cos_t = np.concatenate([np.cos(ang_r), np.cos(ang_c), np.cos(ang_r), np.cos(ang_c), pad + 1.0], axis=1)
    sin_t = np.concatenate([np.sin(ang_r), np.sin(ang_c), np.sin(ang_r), np.sin(ang_c), pad], axis=1)
    return jnp.asarray(cos_t, F32), jnp.asarray(sin_t, F32)


def kernel(x, c, ctx, c_ctx, ada_w, ada_b, norm_mix_g, norm_ffn_g, ffn_w_in, ffn_w_out, lru_w_in, lru_conv_w,
           lru_conv_b, lru_gate_w, lru_gate_b, lru_lambda, lru_w_out, mla_w_in, mla_q_norm_g, mla_kv_norm_g,
           mla_w_uq, mla_w_ukv, mla_w_o, final_norm_g):
    nb, seq, d = x.shape
    ctx_len = ctx.shape[1]
    depth = ada_w.shape[0]
    assert depth == 2 and d == D_MODEL and nb + 1 <= MOD_ROWS
    n_ctx_rows = nb * ctx_len
    tile = lambda cap: int(np.gcd(np.gcd(cap, n_ctx_rows), seq))

    xall = jnp.concatenate([ctx.reshape(n_ctx_rows, d), x.reshape(nb * seq, d)], axis=0)

    cv = jnp.concatenate([c, c_ctx[None, :], jnp.zeros((MOD_ROWS - nb - 1, d), F32)], axis=0)
    mods = _ada_mods(cv, ada_w, ada_b).reshape(depth, MOD_ROWS, 6, d)

    tm = tile(1024)
    gg, u = _lru_in(xall, norm_mix_g[0][None], mods[0], lru_w_in[0].astype(BF16),
                    nb=nb, n_ctx_rows=n_ctx_rows, seq=seq, tm=tm)
    tc = int(np.gcd(np.gcd(256, ctx_len), seq))
    gw = lru_gate_w[0]
    gw = (0.5 * jnp.concatenate([gw[:, 0], gw[:, 1]], axis=-1)).astype(BF16)
    x1 = _lru_scans(u, lru_conv_w[0], lru_conv_b[0][None], gw, 0.5 * lru_gate_b[0], lru_lambda[0][:, None, :],
                    gg, xall, mods[0], lru_w_out[0].astype(BF16), nb=nb, ctx_len=ctx_len, seq=seq, tc=tc)
    tf = tile(512)
    wg0, wu0 = ffn_w_in[0][:, :FFN_HIDDEN].astype(BF16), ffn_w_in[0][:, FFN_HIDDEN:].astype(BF16)
    x2 = _ffn(x1, mods[0], norm_ffn_g[0][None], wg0, wu0, ffn_w_out[0].astype(BF16), final_norm_g[None],
              nb=nb, n_ctx_rows=n_ctx_rows, seq=seq, tm=tf, lat_only=False)

    w_in = mla_w_in[0]
    kr_plain, kr_swap = _rope_cols(w_in[:, Q_LORA + KV_LORA:])
    win_p = jnp.concatenate([w_in[:, :Q_LORA + KV_LORA], kr_plain, kr_swap], axis=1).astype(BF16)
    wq3 = mla_w_uq[0].reshape(Q_LORA, MLA_HEADS, QK_NOPE + QK_ROPE)
    q_nope = wq3[:, :, :QK_NOPE].reshape(Q_LORA, MLA_HEADS * QK_NOPE)
    q_pairs = [_rope_cols(wq3[:, hd, QK_NOPE:]) for hd in range(MLA_HEADS)]
    wq_all = jnp.concatenate([q_nope] + [p[0] for p in q_pairs] + [p[1] for p in q_pairs], axis=1).astype(BF16)
    cos_t, sin_t = _rope_tables(seq)
    tp = int(np.gcd(np.gcd(256, ctx_len), seq))
    wkv3 = mla_w_ukv[0].reshape(KV_LORA, MLA_HEADS, QK_NOPE + V_HEAD)
    wk = wkv3[:, :, :QK_NOPE].reshape(KV_LORA, MLA_HEADS * QK_NOPE).astype(BF16)
    wvt = wkv3[:, :, QK_NOPE:].reshape(KV_LORA, MLA_HEADS * V_HEAD).T.astype(BF16)
    q, k, vt = _mla_proj(x2, norm_mix_g[1][None], mods[1], win_p, mla_q_norm_g[0][None], mla_kv_norm_g[0][None],
                         wq_all, wk, wvt, cos_t, sin_t, nb=nb, ctx_len=ctx_len, seq=seq, tm=tp)
    o = _attention(q, k, vt, tq=int(np.gcd(512, seq)))
    wg1, wu1 = ffn_w_in[1][:, :FFN_HIDDEN].astype(BF16), ffn_w_in[1][:, FFN_HIDDEN:].astype(BF16)
    out = _ffn(x2, mods[1], norm_ffn_g[1][None], wg1, wu1, ffn_w_out[1].astype(BF16), final_norm_g[None],
               nb=nb, n_ctx_rows=n_ctx_rows, seq=seq, tm=tf, lat_only=True,
               attn=(o.reshape(nb * seq, MLA_HEADS * V_HEAD), mla_w_o[0].astype(BF16)), final_norm=True)
    return out.reshape(nb, seq, d)
```

```python
import functools

import numpy as np
import jax
import jax.numpy as jnp
from jax import lax
from jax.experimental import pallas as pl
from jax.experimental.pallas import tpu as pltpu

F32 = jnp.float32
BF16 = jnp.bfloat16

D_MODEL = 1024
GRID_W = 64
NORM_EPS = 1e-6
LRU_WIDTH = D_MODEL
LRU_BLOCK = 256
LRU_BLOCKS = LRU_WIDTH // LRU_BLOCK
CONV_W = 4
RG_C = 8.0
MLA_HEADS = 8
Q_LORA = 384
KV_LORA = 256
QK_NOPE = 128
QK_ROPE = 64
V_HEAD = 128
SM_SCALE = (QK_NOPE + QK_ROPE) ** -0.5
Q_PRESCALE = SM_SCALE * float(np.log2(np.e))
ROPE_THETA = 10000.0

MOD_ROWS = 8
SUBLANES = 8
MXU_DIM = 256
HEAD_PAD = 256
VMEM_LIMIT_MB = 56
QK_GROUPS = 1
TINY = 1e-30


def _cparams(sem, vmem_mb=VMEM_LIMIT_MB):
    return pltpu.CompilerParams(dimension_semantics=sem, vmem_limit_bytes=vmem_mb << 20)


def _const_spec(shape):
    nd = len(shape)
    return pl.BlockSpec(shape, lambda *_: (0,) * nd)


def _norm_mod(x, g, shift, scale):
    ms = jnp.mean(x * x, axis=-1, keepdims=True)
    y = (x * lax.rsqrt(ms + NORM_EPS)) * g
    return y * (1.0 + scale) + shift


def _rmsnorm(x, g):
    ms = jnp.mean(x * x, axis=-1, keepdims=True)
    return (x * lax.rsqrt(ms + NORM_EPS)) * g


def _ada_kernel(cv_ref, w_ref, b_ref, o_ref):
    cv = cv_ref[...]
    s = cv * jax.nn.sigmoid(cv)
    o_ref[0] = jnp.dot(s.astype(BF16), w_ref[0].astype(BF16), preferred_element_type=F32) + b_ref[0]


def _ada_mods(cv, ada_w, ada_b):
    depth, d, n = ada_w.shape
    tn = 1536
    return pl.pallas_call(
        _ada_kernel,
        grid=(depth, n // tn),
        in_specs=[
            pl.BlockSpec((MOD_ROWS, d), lambda l, j: (0, 0)),
            pl.BlockSpec((1, d, tn), lambda l, j: (l, 0, j)),
            pl.BlockSpec((1, 1, tn), lambda l, j: (l, 0, j)),
        ],
        out_specs=pl.BlockSpec((1, MOD_ROWS, tn), lambda l, j: (l, 0, j)),
        out_shape=jax.ShapeDtypeStruct((depth, MOD_ROWS, n), F32),
        compiler_params=_cparams(("arbitrary", "arbitrary")),
        name="ada_mods",
    )(cv, ada_w, ada_b.reshape(depth, 1, n))


def _lru_in_kernel(ctx_ref, x_ref, g_ref, mod_ref, w_ref, gg_ref, u_ref, *, nct):
    xin = jnp.where(pl.program_id(0) < nct, ctx_ref[...], x_ref[...])
    h = _norm_mod(xin, g_ref[...], mod_ref[0, 0:1, :], mod_ref[0, 1:2, :])
    y = jnp.dot(h.astype(BF16), w_ref[...], preferred_element_type=F32)
    r = LRU_WIDTH
    gg_ref[...] = jax.nn.gelu(y[:, :r])
    u_ref[...] = y[:, r:]


def _lru_in(ctx2d, x2d, g, mods, w, *, nb, seq, tm):
    d = x2d.shape[1]
    n = ctx2d.shape[0] + x2d.shape[0]
    r = LRU_WIDTH
    nct = ctx2d.shape[0] // tm
    tpb = seq // tm

    def mod_map(i):
        return (jnp.where(i < nct, nb, (i - nct) // tpb), 0, 0)

    return pl.pallas_call(
        functools.partial(_lru_in_kernel, nct=nct),
        grid=(n // tm,),
        in_specs=[
            pl.BlockSpec((tm, d), lambda i: (jnp.minimum(i, nct - 1), 0)),
            pl.BlockSpec((tm, d), lambda i: (jnp.maximum(i - nct, 0), 0)),
            _const_spec((1, d)),
            pl.BlockSpec((1, 6, d), mod_map),
            _const_spec((d, 2 * r)),
        ],
        out_specs=[pl.BlockSpec((tm, r), lambda i: (i, 0)), pl.BlockSpec((tm, r), lambda i: (i, 0))],
        out_shape=[jax.ShapeDtypeStruct((n, r), F32), jax.ShapeDtypeStruct((n, r), F32)],
        compiler_params=_cparams(("arbitrary",)),
        name="lru_in",
    )(ctx2d, x2d, g, mods, w)


def _lru_coeffs(xc, gw, gbh, lam, a_s, b_s):
    xcb = xc.astype(BF16)
    neg = -lam[...]
    sp = jnp.maximum(neg, 0.0) + jnp.log1p(jnp.exp(-jnp.abs(neg)))
    c2 = (-0.5 * RG_C) * sp
    for n in range(LRU_BLOCKS):
        sl = slice(n * LRU_BLOCK, (n + 1) * LRU_BLOCK)
        pre = jnp.dot(xcb[:, sl], gw[n], preferred_element_type=F32)
        t_r = jnp.tanh(pre[:, :LRU_BLOCK] + gbh[0:1, sl])
        t_i = jnp.tanh(pre[:, LRU_BLOCK:] + gbh[1:2, sl])
        a = jnp.exp(c2[:, sl] * t_r + c2[:, sl])
        gap = 1.0 - a * a
        mult = gap * lax.rsqrt(jnp.maximum(gap, TINY))
        a_s[:, sl] = a
        b_s[:, sl] = (mult * (0.5 * t_i + 0.5)) * xc[:, sl]


def _lru_chunk_scan(a_s, b_s, h_s, hc, *, reverse, tc):
    r = LRU_WIDTH
    row = lax.broadcasted_iota(jnp.int32, (SUBLANES, r), 0)
    ngroups = tc // SUBLANES

    def group(gi, hprev):
        g0 = (ngroups - 1 - gi) if reverse else gi
        rows = pl.ds(pl.multiple_of(g0 * SUBLANES, SUBLANES), SUBLANES)
        av = a_s[rows, :]
        bv = b_s[rows, :]
        for s in (1, 2, 4):
            if reverse:
                keep = row < (SUBLANES - s)
                shift = SUBLANES - s
            else:
                keep = row >= s
                shift = s
            a_sh = jnp.where(keep, pltpu.roll(av, shift, 0), 1.0)
            b_sh = jnp.where(keep, pltpu.roll(bv, shift, 0), 0.0)
            bv = av * b_sh + bv
            av = av * a_sh
        hrows = av * hprev + bv
        h_s[rows, :] = hrows
        edge = hrows[0:1, :] if reverse else hrows[SUBLANES - 1:SUBLANES, :]
        return jnp.broadcast_to(edge, (SUBLANES, r))

    hc[...] = lax.fori_loop(0, ngroups, group, hc[...], unroll=2)


def _scan_fwd_kernel(ucur, uprev, unext, cw, cb, gw, gbh, lam, hf, xc_out, ext, a_s, b_s, hc, *, tc, nc, nl):
    j = pl.program_id(1)
    in_ctx = j < nc
    pos = jnp.where(in_ctx, j, j - nc)
    first = pos == 0
    last = pos == jnp.where(in_ctx, nc, nl) - 1

    @pl.when(j == 0)
    def _():
        hc[...] = jnp.zeros_like(hc)

    n_ext = tc + 2 * SUBLANES
    ext[0:SUBLANES, :] = jnp.where(first, 0.0, uprev[...])
    ext[SUBLANES:SUBLANES + tc, :] = ucur[...]
    ext[SUBLANES + tc:n_ext, :] = jnp.where(last, 0.0, unext[...])
    e = ext[...]
    mid = slice(SUBLANES, SUBLANES + tc)
    xc = cb[...] + cw[0:1, :] * pltpu.roll(e, 1, 0)[mid] + cw[1:2, :] * e[mid] \
        + cw[2:3, :] * pltpu.roll(e, n_ext - 1, 0)[mid] + cw[3:4, :] * pltpu.roll(e, n_ext - 2, 0)[mid]
    xc_out[...] = xc
    _lru_coeffs(xc, gw, gbh, lam, a_s, b_s)
    _lru_chunk_scan(a_s, b_s, hf, hc, reverse=False, tc=tc)


def _scan_bwd_kernel(xc, gw, gbh, lam, hf, gg, ctx_res, x_res, mod, wout, out, a_s, b_s, hc, hb_s, *, tc, nc):
    j = pl.program_id(1)

    @pl.when(j == 0)
    def _():
        hc[...] = jnp.zeros_like(hc)

    _lru_coeffs(xc[...], gw, gbh, lam, a_s, b_s)
    _lru_chunk_scan(a_s, b_s, hb_s, hc, reverse=True, tc=tc)
    z = ((hf[...] + hb_s[...]) * gg[...]).astype(BF16)
    res = jnp.where(j < nc, ctx_res[...], x_res[...])
    out[...] = res + mod[0, 2:3, :] * jnp.dot(z, wout[...], preferred_element_type=F32)


def _lru_scans(u, conv_w, conv_b, gw, gbh, lam, gg, ctx2d, x2d, mods, wout, *, nb, ctx_len, seq, tc):
    n, r = u.shape
    d = x2d.shape[1]
    nc = ctx_len // tc
    nl = seq // tc
    blk8 = tc // SUBLANES

    def cur_blk(b, j, reverse):
        jc = (nc - 1 - j) if reverse else j
        jl = (nl - 1 - (j - nc)) if reverse else (j - nc)
        return jnp.where(j < nc, b * nc + jc, nb * nc + b * nl + jl)

    fwd_map = lambda b, j: (cur_blk(b, j, False), 0)
    bwd_map = lambda b, j: (cur_blk(b, j, True), 0)
    prev_map = lambda b, j: (jnp.maximum(cur_blk(b, j, False) * blk8 - 1, 0), 0)
    next_map = lambda b, j: (jnp.minimum((cur_blk(b, j, False) + 1) * blk8, n // SUBLANES - 1), 0)
    gate_specs = [
        _const_spec((LRU_BLOCKS, LRU_BLOCK, 2 * LRU_BLOCK)),
        _const_spec((2, r)),
        _const_spec((1, r)),
    ]
    chunk = pltpu.VMEM((tc, r), F32)
    carry = pltpu.VMEM((SUBLANES, r), F32)
    hf, xc = pl.pallas_call(
        functools.partial(_scan_fwd_kernel, tc=tc, nc=nc, nl=nl),
        grid=(nb, nc + nl),
        in_specs=[
            pl.BlockSpec((tc, r), fwd_map),
            pl.BlockSpec((SUBLANES, r), prev_map),
            pl.BlockSpec((SUBLANES, r), next_map),
            _const_spec((CONV_W, r)),
            _const_spec((1, r)),
        ] + gate_specs,
        out_specs=[pl.BlockSpec((tc, r), fwd_map), pl.BlockSpec((tc, r), fwd_map)],
        out_shape=[jax.ShapeDtypeStruct((n, r), F32), jax.ShapeDtypeStruct((n, r), F32)],
        scratch_shapes=[pltpu.VMEM((tc + 2 * SUBLANES, r), F32), chunk, chunk, carry],
        compiler_params=_cparams(("arbitrary", "arbitrary")),
        name="lru_scan_fwd",
    )(u, u, u, conv_w, conv_b, gw[0], gbh[0], lam[0])
    ctx_res_map = lambda b, j: (b * nc + jnp.maximum(nc - 1 - j, 0), 0)
    x_res_map = lambda b, j: (b * nl + nl - 1 - jnp.maximum(j - nc, 0), 0)
    return pl.pallas_call(
        functools.partial(_scan_bwd_kernel, tc=tc, nc=nc),
        grid=(nb, nc + nl),
        in_specs=[pl.BlockSpec((tc, r), bwd_map)] + gate_specs + [
            pl.BlockSpec((tc, r), bwd_map),
            pl.BlockSpec((tc, r), bwd_map),
            pl.BlockSpec((tc, d), ctx_res_map),
            pl.BlockSpec((tc, d), x_res_map),
            pl.BlockSpec((1, 6, d), lambda b, j: (jnp.where(j < nc, nb, b), 0, 0)),
            _const_spec((r, d)),
        ],
        out_specs=pl.BlockSpec((tc, d), bwd_map),
        out_shape=jax.ShapeDtypeStruct((n, d), F32),
        scratch_shapes=[chunk, chunk, carry, chunk],
        compiler_params=_cparams(("arbitrary", "arbitrary")),
        name="lru_scan_bwd",
    )(xc, gw[1], gbh[1], lam[1], hf, gg, ctx2d, x2d, mods, wout)


def _ffn_kernel(*refs, with_attn, final_norm, n_chunks):
    if with_attn:
        x_ref, o_ref, wo_ref, mod_ref, g_ref, wi_ref, wd_ref, fg_ref, out_ref = refs
    else:
        x_ref, mod_ref, g_ref, wi_ref, wd_ref, fg_ref, out_ref = refs
    x = x_ref[...]
    if with_attn:
        x = x + mod_ref[0, 2:3, :] * jnp.dot(o_ref[...], wo_ref[...], preferred_element_type=F32)
    h = _norm_mod(x, g_ref[...], mod_ref[0, 3:4, :], mod_ref[0, 4:5, :]).astype(BF16)
    hid = wd_ref.shape[1]
    n_mxu = hid // MXU_DIM
    edges = [round(c * n_mxu / n_chunks) * MXU_DIM for c in range(n_chunks)] + [hid]
    acc = jnp.zeros(x.shape, F32)
    for c in range(n_chunks):
        gate = jnp.dot(h, wi_ref[0, :, edges[c]:edges[c + 1]], preferred_element_type=F32)
        up = jnp.dot(h, wi_ref[0, :, hid + edges[c]:hid + edges[c + 1]], preferred_element_type=F32)
        act = ((gate * jax.nn.sigmoid(gate)) * up).astype(BF16)
        acc = acc + jnp.dot(act, wd_ref[0, edges[c]:edges[c + 1], :], preferred_element_type=F32)
    x = x + mod_ref[0, 5:6, :] * acc
    if final_norm:
        x = _rmsnorm(x, fg_ref[...])
    out_ref[...] = x


def _ffn(xall, mods, g, wi, wd, fg, *, layer, nb, n_ctx_rows, seq, tm, lat_only, attn=None, final_norm=False):
    n, d = xall.shape
    hid = wd.shape[1]
    nct = n_ctx_rows // tm
    tpb = seq // tm
    off = nct if lat_only else 0
    n_out = n - n_ctx_rows if lat_only else n

    def mod_map(i):
        ii = i + off
        return (jnp.where(ii < nct, nb, (ii - nct) // tpb), 0, 0)

    in_specs = [pl.BlockSpec((tm, d), lambda i: (i + off, 0))]
    args = [xall]
    if attn is not None:
        o, wo = attn
        in_specs += [pl.BlockSpec((tm, o.shape[1]), lambda i: (i, 0)), _const_spec(wo.shape)]
        args += [o, wo]
    in_specs += [
        pl.BlockSpec((1, 6, d), mod_map),
        _const_spec((1, d)),
        pl.BlockSpec((1, d, 2 * hid), lambda i: (layer, 0, 0), pipeline_mode=pl.Buffered(1)),
        pl.BlockSpec((1, hid, d), lambda i: (layer, 0, 0), pipeline_mode=pl.Buffered(1)),
        _const_spec((1, d)),
    ]
    args += [mods, g, wi, wd, fg]
    return pl.pallas_call(
        functools.partial(_ffn_kernel, with_attn=attn is not None, final_norm=final_norm, n_chunks=2),
        grid=(n_out // tm,),
        in_specs=in_specs,
        out_specs=pl.BlockSpec((tm, d), lambda i: (i, 0)),
        out_shape=jax.ShapeDtypeStruct((n_out, d), F32),
        compiler_params=_cparams(("arbitrary",)),
        name="ffn_attn" if attn is not None else "ffn",
    )(*args)


def _mla_proj_kernel(x_ref, g_ref, mod_ref, win_ref, qg_ref, kvg_ref, wq_ref, wk_ref, wvt_ref, cos_ref, sin_ref,
                     q_ref, k_ref, v_ref, *, nct):
    i = pl.program_id(0)
    is_ctx = i < nct
    h = _norm_mod(x_ref[...], g_ref[...], mod_ref[0, 0:1, :], mod_ref[0, 1:2, :]).astype(BF16)
    c_all = jnp.dot(h, win_ref[...], preferred_element_type=F32)
    cosv = jnp.where(is_ctx, 1.0, cos_ref[...])
    sinv = jnp.where(is_ctx, 0.0, sin_ref[...])
    kv_lo = Q_LORA + KV_LORA
    lane = lax.broadcasted_iota(jnp.int32, (1, 128), 1)
    low = lane < QK_ROPE
    kt = c_all[:, kv_lo:kv_lo + 128] * jnp.where(low, cosv, sinv)
    kt = kt + pltpu.roll(kt, QK_ROPE, 1)
    k_rope = (jnp.where(low, kt, 0.0).astype(BF16), jnp.where(low, 0.0, kt).astype(BF16))
    ckv = _rmsnorm(c_all[:, Q_LORA:kv_lo], kvg_ref[...])
    k_nope = jnp.dot(ckv.astype(BF16), wk_ref[...], preferred_element_type=F32)
    v_t = jnp.dot(wvt_ref[...], ckv.T.astype(BF16), preferred_element_type=F32)
    for hd in range(MLA_HEADS):
        sl = slice(hd * 128, (hd + 1) * 128)
        k_ref[0, hd, :, 0:QK_NOPE] = k_nope[:, sl].astype(BF16)
        k_ref[0, hd, :, QK_NOPE:HEAD_PAD] = k_rope[hd % 2]
        v_ref[0, hd] = v_t[sl, :].astype(BF16)

    @pl.when(jnp.logical_not(is_ctx))
    def _():
        cq = _rmsnorm(c_all[:, :Q_LORA], qg_ref[...]).astype(BF16)
        q_all = jnp.dot(cq, wq_ref[...], preferred_element_type=F32) * Q_PRESCALE
        hw = MLA_HEADS * QK_NOPE
        hr = MLA_HEADS * QK_ROPE
        for pair in range(MLA_HEADS // 2):
            ps = slice(hw + pair * 128, hw + (pair + 1) * 128)
            ss = slice(hw + hr + pair * 128, hw + hr + (pair + 1) * 128)
            q_rope = (q_all[:, ps] * cos_ref[...] + q_all[:, ss] * sin_ref[...]).astype(BF16)
            for hd in (2 * pair, 2 * pair + 1):
                q_ref[0, hd, :, 0:QK_NOPE] = q_all[:, hd * 128:(hd + 1) * 128].astype(BF16)
                q_ref[0, hd, :, QK_NOPE:HEAD_PAD] = q_rope


def _mla_proj(xall, g, mods, win, qg, kvg, wq, wk, wvt, cos_t, sin_t, *, nb, ctx_len, seq, tm):
    n, d = xall.shape
    nct = nb * ctx_len // tm
    cpb = ctx_len // tm
    tpb = seq // tm
    t_all = ctx_len + seq

    def lat(i):
        return jnp.maximum(i - nct, 0)

    def mod_map(i):
        return (jnp.where(i < nct, nb, lat(i) // tpb), 0, 0)

    def kv_map(i):
        b = jnp.where(i < nct, i // cpb, lat(i) // tpb)
        t = jnp.where(i < nct, i % cpb, cpb + lat(i) % tpb)
        return (b, 0, t, 0)

    def vt_map(i):
        b, _, t, _ = kv_map(i)
        return (b, 0, 0, t)

    def q_map(i):
        return (lat(i) // tpb, 0, lat(i) % tpb, 0)

    def tab_map(i):
        return (lat(i) % tpb, 0)

    return pl.pallas_call(
        functools.partial(_mla_proj_kernel, nct=nct),
        grid=(n // tm,),
        in_specs=[
            pl.BlockSpec((tm, d), lambda i: (i, 0)),
            _const_spec((1, d)),
            pl.BlockSpec((1, 6, d), mod_map),
            _const_spec(win.shape),
            _const_spec((1, Q_LORA)),
            _const_spec((1, KV_LORA)),
            _const_spec(wq.shape),
            _const_spec(wk.shape),
            _const_spec(wvt.shape),
            pl.BlockSpec((tm, 128), tab_map),
            pl.BlockSpec((tm, 128), tab_map),
        ],
        out_specs=[
            pl.BlockSpec((1, MLA_HEADS, tm, HEAD_PAD), q_map),
            pl.BlockSpec((1, MLA_HEADS, tm, HEAD_PAD), kv_map),
            pl.BlockSpec((1, MLA_HEADS, V_HEAD, tm), vt_map),
        ],
        out_shape=[
            jax.ShapeDtypeStruct((nb, MLA_HEADS, seq, HEAD_PAD), BF16),
            jax.ShapeDtypeStruct((nb, MLA_HEADS, t_all, HEAD_PAD), BF16),
            jax.ShapeDtypeStruct((nb, MLA_HEADS, V_HEAD, t_all), BF16),
        ],
        compiler_params=_cparams(("arbitrary",)),
        name="mla_proj",
    )(xall, g, mods, win, qg, kvg, wq, wk, wvt, cos_t, sin_t)


def _attn_kernel(q_ref, k_ref, vt_ref, o_ref, s0_ref, s1_ref, *, tq, kc):
    t_all = k_ref.shape[2]
    nkc = t_all // kc
    nq = q_ref.shape[2] // tq
    s_refs = (s0_ref, s1_ref)

    def group_reduce(x, op):
        return op(x.reshape(kc // SUBLANES, SUBLANES, tq), axis=0)

    bounds = [round(g * nkc / QK_GROUPS) for g in range(QK_GROUPS + 1)]
    groups = [(a, b) for a, b in zip(bounds[:-1], bounds[1:]) if b > a]

    def scores(qv, c0, c1, s_ref):
        st = lax.dot_general(k_ref[0, 0, c0 * kc:c1 * kc, :], qv, (((1,), (1,)), ((), ())),
                             preferred_element_type=F32)
        s_ref[c0 * kc:c1 * kc, :] = st
        return jnp.max(st.reshape((c1 - c0) * kc // SUBLANES, SUBLANES, tq), axis=0)

    def q_tile(i):
        return q_ref[0, 0, pl.ds(pl.multiple_of(i * tq, tq), tq), :]

    def tile_step(i, slot, m8, with_next):
        s_cur, s_nxt = s_refs[slot], s_refs[1 - slot]
        m_next = jnp.full((SUBLANES, tq), -jnp.inf, F32)
        if with_next:
            q_next = q_tile(i + 1)
        m = jnp.max(m8, axis=0, keepdims=True)
        l8 = jnp.zeros((SUBLANES, tq), F32)
        acc = jnp.zeros((V_HEAD, tq), F32)

        def pv(acc, c, pb):
            return acc + jnp.dot(vt_ref[0, 0, :, c * kc:(c + 1) * kc], pb, preferred_element_type=F32)

        p_prev = None
        for c0, c1 in groups:
            if with_next:
                m_next = jnp.maximum(m_next, scores(q_next, c0, c1, s_nxt))
            for c in range(c0, c1):
                if p_prev is not None:
                    acc = pv(acc, c - 1, p_prev)
                p = jnp.exp2(s_cur[c * kc:(c + 1) * kc, :] - m)
                l8 = l8 + group_reduce(p, jnp.sum)
                p_prev = p.astype(BF16)
        acc = pv(acc, nkc - 1, p_prev)
        l = jnp.sum(l8, axis=0, keepdims=True)
        rows = pl.ds(pl.multiple_of(i * tq, tq), tq)
        o_ref[0, rows, :] = (acc / l).T.astype(o_ref.dtype)
        return m_next

    m8 = scores(q_tile(0), 0, nkc, s0_ref)

    def pair(j, m8):
        m8 = tile_step(2 * j, 0, m8, True)
        return tile_step(2 * j + 1, 1, m8, True)

    n_pairs = (nq - 1) // 2
    m8 = lax.fori_loop(0, n_pairs, pair, m8)
    if (nq - 1) % 2:
        m8 = tile_step(nq - 2, 0, m8, True)
    tile_step(nq - 1, (nq - 1) % 2, m8, False)


def _attention(q, k, v, *, tq):
    nb, nh, seq, dp = q.shape
    t_all = k.shape[2]
    kc = int(np.gcd(256, t_all))
    return pl.pallas_call(
        functools.partial(_attn_kernel, tq=tq, kc=kc),
        grid=(nb, nh),
        scratch_shapes=[pltpu.VMEM((t_all, tq), F32), pltpu.VMEM((t_all, tq), F32)],
        in_specs=[
            pl.BlockSpec((1, 1, seq, dp), lambda b, h: (b, h, 0, 0)),
            pl.BlockSpec((1, 1, t_all, dp), lambda b, h: (b, h, 0, 0)),
            pl.BlockSpec((1, 1, V_HEAD, t_all), lambda b, h: (b, h, 0, 0)),
        ],
        out_specs=pl.BlockSpec((1, seq, V_HEAD), lambda b, h: (b, 0, h)),
        out_shape=jax.ShapeDtypeStruct((nb, seq, nh * V_HEAD), BF16),
        compiler_params=_cparams(("arbitrary", "arbitrary")),
        name="mla_attention",
    )(q, k, v)


def _rope_cols(w):
    qd = QK_ROPE // 4
    r0, r1, r2, r3 = (w[..., i * qd:(i + 1) * qd] for i in range(4))
    return jnp.concatenate([r0, r2, r1, r3], axis=-1), jnp.concatenate([-r1, -r3, r0, r2], axis=-1)


def _rope_tables(seq):
    rows = seq // GRID_W
    row_ids = np.repeat(np.arange(rows, dtype=np.float32), GRID_W)
    col_ids = np.tile(np.arange(GRID_W, dtype=np.float32), rows)
    axis_dim = QK_ROPE // 2
    expo = (np.arange(0, axis_dim, 2, dtype=np.float32) / np.float32(axis_dim)).astype(np.float32)
    inv_freq = (np.float32(1.0) / np.power(np.float32(ROPE_THETA), expo)).astype(np.float32)
    ang_r = (row_ids[:, None] * inv_freq).astype(np.float32)
    ang_c = (col_ids[:, None] * inv_freq).astype(np.float32)
    reps = 128 // (QK_ROPE // 2)
    cos_t = np.tile(np.concatenate([np.cos(ang_r), np.cos(ang_c)], axis=1), (1, reps))
    sin_t = np.tile(np.concatenate([np.sin(ang_r), np.sin(ang_c)], axis=1), (1, reps))
    return jnp.asarray(cos_t, F32), jnp.asarray(sin_t, F32)


def kernel(x, c, ctx, c_ctx, ada_w, ada_b, norm_mix_g, norm_ffn_g, ffn_w_in, ffn_w_out, lru_w_in, lru_conv_w,
           lru_conv_b, lru_gate_w, lru_gate_b, lru_lambda, lru_w_out, mla_w_in, mla_q_norm_g, mla_kv_norm_g,
           mla_w_uq, mla_w_ukv, mla_w_o, final_norm_g):
    nb, seq, d = x.shape
    ctx_len = ctx.shape[1]
    depth = ada_w.shape[0]
    assert depth == 2 and d == D_MODEL and nb + 1 <= MOD_ROWS
    n_ctx_rows = nb * ctx_len
    tile = lambda cap: int(np.gcd(np.gcd(cap, n_ctx_rows), seq))

    ctx2d = ctx.reshape(n_ctx_rows, d)
    x2d = x.reshape(nb * seq, d)
    ffn_wi = ffn_w_in.astype(BF16)
    ffn_wd = ffn_w_out.astype(BF16)

    cv = jnp.concatenate([c, c_ctx[None, :], jnp.zeros((MOD_ROWS - nb - 1, d), F32)], axis=0)
    mods = _ada_mods(cv, ada_w, ada_b).reshape(depth, MOD_ROWS, 6, d)

    tm = tile(1024)
    gg, u = _lru_in(ctx2d, x2d, norm_mix_g[0][None], mods[0], lru_w_in[0].astype(BF16), nb=nb, seq=seq, tm=tm)
    tc = int(np.gcd(np.gcd(256, ctx_len), seq))
    gw = lru_gate_w[0]
    gw = (0.5 * jnp.concatenate([gw[:, 0], gw[:, 1]], axis=-1)).astype(BF16)
    x1 = _lru_scans(u, lru_conv_w[0], lru_conv_b[0][None], gw, 0.5 * lru_gate_b[0], lru_lambda[0][:, None, :],
                    gg, ctx2d, x2d, mods[0], lru_w_out[0].astype(BF16), nb=nb, ctx_len=ctx_len, seq=seq, tc=tc)
    tf = tile(512)
    x2 = _ffn(x1, mods[0], norm_ffn_g[0][None], ffn_wi, ffn_wd, final_norm_g[None], layer=0,
              nb=nb, n_ctx_rows=n_ctx_rows, seq=seq, tm=tf, lat_only=False)

    w_in = mla_w_in[0]
    win_p = jnp.concatenate((w_in[:, :Q_LORA + KV_LORA],) + _rope_cols(w_in[:, Q_LORA + KV_LORA:]),
                            axis=1).astype(BF16)
    wq3 = mla_w_uq[0].reshape(Q_LORA, MLA_HEADS, QK_NOPE + QK_ROPE)
    wq_all = jnp.concatenate([part.reshape(Q_LORA, -1)
                              for part in (wq3[:, :, :QK_NOPE],) + _rope_cols(wq3[:, :, QK_NOPE:])],
                             axis=1).astype(BF16)
    cos_t, sin_t = _rope_tables(seq)
    tp = int(np.gcd(np.gcd(256, ctx_len), seq))
    wkv3 = mla_w_ukv[0].reshape(KV_LORA, MLA_HEADS, QK_NOPE + V_HEAD)
    wk = wkv3[:, :, :QK_NOPE].reshape(KV_LORA, MLA_HEADS * QK_NOPE).astype(BF16)
    wvt = wkv3[:, :, QK_NOPE:].reshape(KV_LORA, MLA_HEADS * V_HEAD).T.astype(BF16)
    q, k, vt = _mla_proj(x2, norm_mix_g[1][None], mods[1], win_p, mla_q_norm_g[0][None], mla_kv_norm_g[0][None],
                         wq_all, wk, wvt, cos_t, sin_t, nb=nb, ctx_len=ctx_len, seq=seq, tm=tp)
    o = _attention(q, k, vt, tq=int(np.gcd(512, seq)))
    out = _ffn(x2, mods[1], norm_ffn_g[1][None], ffn_wi, ffn_wd, final_norm_g[None], layer=1,
               nb=nb, n_ctx_rows=n_ctx_rows, seq=seq, tm=tf, lat_only=True,
               attn=(o.reshape(nb * seq, MLA_HEADS * V_HEAD), mla_w_o[0].astype(BF16)), final_norm=True)
    return out.reshape(nb, seq, d)
```

```python
import functools

import numpy as np
import jax
import jax.numpy as jnp
from jax import lax
from jax.experimental import pallas as pl
from jax.experimental.pallas import tpu as pltpu

F32 = jnp.float32
BF16 = jnp.bfloat16

D_MODEL = 1024
GRID_W = 64
NORM_EPS = 1e-6
LRU_WIDTH = D_MODEL
LRU_BLOCK = 256
LRU_BLOCKS = LRU_WIDTH // LRU_BLOCK
CONV_W = 4
RG_C = 8.0
MLA_HEADS = 8
Q_LORA = 384
KV_LORA = 256
QK_NOPE = 128
QK_ROPE = 64
V_HEAD = 128
SM_SCALE = (QK_NOPE + QK_ROPE) ** -0.5
Q_PRESCALE = SM_SCALE * float(np.log2(np.e))
ROPE_THETA = 10000.0

MOD_ROWS = 8
SUBLANES = 8
MXU_DIM = 256
HEAD_PAD = 256
VMEM_LIMIT_MB = 56
QK_GROUPS = 1
FFN_LAG = 2
TINY = 1e-30


def _cparams(sem, vmem_mb=VMEM_LIMIT_MB):
    return pltpu.CompilerParams(dimension_semantics=sem, vmem_limit_bytes=vmem_mb << 20)


def _const_spec(shape):
    nd = len(shape)
    return pl.BlockSpec(shape, lambda *_: (0,) * nd)


def _norm_mod(x, g, shift, scale):
    ms = jnp.mean(x * x, axis=-1, keepdims=True)
    y = (x * lax.rsqrt(ms + NORM_EPS)) * g
    return y * (1.0 + scale) + shift


def _rmsnorm(x, g):
    ms = jnp.mean(x * x, axis=-1, keepdims=True)
    return (x * lax.rsqrt(ms + NORM_EPS)) * g


def _ada_kernel(cv_ref, w_ref, b_ref, o_ref):
    cv = cv_ref[...]
    s = cv * jax.nn.sigmoid(cv)
    o_ref[0] = jnp.dot(s.astype(BF16), w_ref[0].astype(BF16), preferred_element_type=F32) + b_ref[0]


def _ada_mods(cv, ada_w, ada_b):
    depth, d, n = ada_w.shape
    tn = 1536
    return pl.pallas_call(
        _ada_kernel,
        grid=(depth, n // tn),
        in_specs=[
            pl.BlockSpec((MOD_ROWS, d), lambda l, j: (0, 0)),
            pl.BlockSpec((1, d, tn), lambda l, j: (l, 0, j)),
            pl.BlockSpec((1, 1, tn), lambda l, j: (l, 0, j)),
        ],
        out_specs=pl.BlockSpec((1, MOD_ROWS, tn), lambda l, j: (l, 0, j)),
        out_shape=jax.ShapeDtypeStruct((depth, MOD_ROWS, n), F32),
        compiler_params=_cparams(("arbitrary", "arbitrary")),
        name="ada_mods",
    )(cv, ada_w, ada_b.reshape(depth, 1, n))


def _lru_in_kernel(ctx_ref, x_ref, g_ref, mod_ref, w_ref, gg_ref, u_ref, *, nct):
    xin = jnp.where(pl.program_id(0) < nct, ctx_ref[...], x_ref[...])
    h = _norm_mod(xin, g_ref[...], mod_ref[0, 0:1, :], mod_ref[0, 1:2, :])
    y = jnp.dot(h.astype(BF16), w_ref[...], preferred_element_type=F32)
    r = LRU_WIDTH
    gg_ref[...] = jax.nn.gelu(y[:, :r])
    u_ref[...] = y[:, r:]


def _lru_in(ctx2d, x2d, g, mods, w, *, nb, seq, tm):
    d = x2d.shape[1]
    n = ctx2d.shape[0] + x2d.shape[0]
    r = LRU_WIDTH
    nct = ctx2d.shape[0] // tm
    tpb = seq // tm

    def mod_map(i):
        return (jnp.where(i < nct, nb, (i - nct) // tpb), 0, 0)

    return pl.pallas_call(
        functools.partial(_lru_in_kernel, nct=nct),
        grid=(n // tm,),
        in_specs=[
            pl.BlockSpec((tm, d), lambda i: (jnp.minimum(i, nct - 1), 0)),
            pl.BlockSpec((tm, d), lambda i: (jnp.maximum(i - nct, 0), 0)),
            _const_spec((1, d)),
            pl.BlockSpec((1, 6, d), mod_map),
            _const_spec((d, 2 * r)),
        ],
        out_specs=[pl.BlockSpec((tm, r), lambda i: (i, 0)), pl.BlockSpec((tm, r), lambda i: (i, 0))],
        out_shape=[jax.ShapeDtypeStruct((n, r), F32), jax.ShapeDtypeStruct((n, r), F32)],
        compiler_params=_cparams(("arbitrary",)),
        name="lru_in",
    )(ctx2d, x2d, g, mods, w)


def _lru_coeffs(xc, gw, gbh, lam, a_s, b_s, between=None):
    xcb = xc.astype(BF16)
    neg = -lam[...]
    sp = jnp.maximum(neg, 0.0) + jnp.log1p(jnp.exp(-jnp.abs(neg)))
    c2 = (-0.5 * RG_C) * sp
    for n in range(LRU_BLOCKS):
        sl = slice(n * LRU_BLOCK, (n + 1) * LRU_BLOCK)
        pre = jnp.dot(xcb[:, sl], gw[n], preferred_element_type=F32)
        t_r = jnp.tanh(pre[:, :LRU_BLOCK] + gbh[0:1, sl])
        t_i = jnp.tanh(pre[:, LRU_BLOCK:] + gbh[1:2, sl])
        a = jnp.exp(c2[:, sl] * t_r + c2[:, sl])
        gap = 1.0 - a * a
        mult = gap * lax.rsqrt(jnp.maximum(gap, TINY))
        a_s[:, sl] = a
        b_s[:, sl] = (mult * (0.5 * t_i + 0.5)) * xc[:, sl]
        if between is not None:
            between()


def _lru_chunk_scan(a_s, b_s, h_s, hc, *, reverse, tc, between=None):
    r = LRU_WIDTH
    row = lax.broadcasted_iota(jnp.int32, (SUBLANES, r), 0)
    ngroups = tc // SUBLANES

    def group(gi, hprev):
        g0 = (ngroups - 1 - gi) if reverse else gi
        start = g0 * SUBLANES
        rows = pl.ds(start if isinstance(start, int) else pl.multiple_of(start, SUBLANES), SUBLANES)
        av = a_s[rows, :]
        bv = b_s[rows, :]
        for s in (1, 2, 4):
            if reverse:
                keep = row < (SUBLANES - s)
                shift = SUBLANES - s
            else:
                keep = row >= s
                shift = s
            a_sh = jnp.where(keep, pltpu.roll(av, shift, 0), 1.0)
            b_sh = jnp.where(keep, pltpu.roll(bv, shift, 0), 0.0)
            bv = av * b_sh + bv
            av = av * a_sh
        hrows = av * hprev + bv
        h_s[rows, :] = hrows
        edge = hrows[0:1, :] if reverse else hrows[SUBLANES - 1:SUBLANES, :]
        return jnp.broadcast_to(edge, (SUBLANES, r))

    if between is None:
        hc[...] = lax.fori_loop(0, ngroups, group, hc[...], unroll=2)
    else:
        h = hc[...]
        for gi in range(ngroups):
            h = group(gi, h)
            between(gi)
        hc[...] = h


def _scan_fwd_kernel(ucur, uprev, unext, cw, cb, gw, gbh, lam, hf, xc_out, ext, a_s, b_s, hc, *, tc, nc, nl):
    j = pl.program_id(1)
    in_ctx = j < nc
    pos = jnp.where(in_ctx, j, j - nc)
    first = pos == 0
    last = pos == jnp.where(in_ctx, nc, nl) - 1

    @pl.when(j == 0)
    def _():
        hc[...] = jnp.zeros_like(hc)

    n_ext = tc + 2 * SUBLANES
    ext[0:SUBLANES, :] = jnp.where(first, 0.0, uprev[...])
    ext[SUBLANES:SUBLANES + tc, :] = ucur[...]
    ext[SUBLANES + tc:n_ext, :] = jnp.where(last, 0.0, unext[...])
    e = ext[...]
    mid = slice(SUBLANES, SUBLANES + tc)
    xc = cb[...] + cw[0:1, :] * pltpu.roll(e, 1, 0)[mid] + cw[1:2, :] * e[mid] \
        + cw[2:3, :] * pltpu.roll(e, n_ext - 1, 0)[mid] + cw[3:4, :] * pltpu.roll(e, n_ext - 2, 0)[mid]
    xc_out[...] = xc
    _lru_coeffs(xc, gw, gbh, lam, a_s, b_s)
    _lru_chunk_scan(a_s, b_s, hf, hc, reverse=False, tc=tc)


def _scan_bwd_ffn_kernel(xc, gw, gbh, lam, hf, gg, ctx_res, x_res, mod, wout, mod_prev, fg, wi, wd,
                         out, a_s, b_s, hc, hb_s, x1_prev, *, tc, nc, per_batch, nsteps):
    s = pl.program_id(0)
    j = jnp.minimum(s, nsteps - 1) % per_batch

    @pl.when(s == 0)
    def _():
        x1_prev[...] = jnp.zeros_like(x1_prev)

    @pl.when(j == 0)
    def _():
        hc[...] = jnp.zeros_like(hc)

    pieces, ffn_result = _ffn_pieces(x1_prev[...], mod_prev, fg, wi, wd, n_chunks=wd.shape[1] // MXU_DIM)
    todo = list(pieces)
    run_next = lambda: todo.pop(0)() if todo else None
    run_two = lambda: (run_next(), run_next())
    run_two()
    _lru_coeffs(xc[...], gw, gbh, lam, a_s, b_s, between=run_two)
    ngroups = tc // SUBLANES
    n_scan = len(todo)

    def between(gi):
        while todo and (n_scan - len(todo)) * ngroups < (gi + 1) * n_scan:
            run_next()

    _lru_chunk_scan(a_s, b_s, hb_s, hc, reverse=True, tc=tc, between=between)
    out[...] = ffn_result()
    z = ((hf[...] + hb_s[...]) * gg[...]).astype(BF16)
    res = jnp.where(j < nc, ctx_res[...], x_res[...])
    x1_prev[...] = res + mod[0, 2:3, :] * jnp.dot(z, wout[...], preferred_element_type=F32)


def _lru_scans_ffn(u, conv_w, conv_b, gw, gbh, lam, gg, ctx2d, x2d, mods, wout, ffn_g, ffn_wi, ffn_wd, *,
                   layer, nb, ctx_len, seq, tc):
    n, r = u.shape
    d = x2d.shape[1]
    nc = ctx_len // tc
    nl = seq // tc
    blk8 = tc // SUBLANES

    def cur_blk(b, j, reverse):
        jc = (nc - 1 - j) if reverse else j
        jl = (nl - 1 - (j - nc)) if reverse else (j - nc)
        return jnp.where(j < nc, b * nc + jc, nb * nc + b * nl + jl)

    fwd_map = lambda b, j: (cur_blk(b, j, False), 0)
    bwd_map = lambda b, j: (cur_blk(b, j, True), 0)
    prev_map = lambda b, j: (jnp.maximum(cur_blk(b, j, False) * blk8 - 1, 0), 0)
    next_map = lambda b, j: (jnp.minimum((cur_blk(b, j, False) + 1) * blk8, n // SUBLANES - 1), 0)
    gate_specs = [
        _const_spec((LRU_BLOCKS, LRU_BLOCK, 2 * LRU_BLOCK)),
        _const_spec((2, r)),
        _const_spec((1, r)),
    ]
    chunk = pltpu.VMEM((tc, r), F32)
    carry = pltpu.VMEM((SUBLANES, r), F32)
    hf, xc = pl.pallas_call(
        functools.partial(_scan_fwd_kernel, tc=tc, nc=nc, nl=nl),
        grid=(nb, nc + nl),
        in_specs=[
            pl.BlockSpec((tc, r), fwd_map),
            pl.BlockSpec((SUBLANES, r), prev_map),
            pl.BlockSpec((SUBLANES, r), next_map),
            _const_spec((CONV_W, r)),
            _const_spec((1, r)),
        ] + gate_specs,
        out_specs=[pl.BlockSpec((tc, r), fwd_map), pl.BlockSpec((tc, r), fwd_map)],
        out_shape=[jax.ShapeDtypeStruct((n, r), F32), jax.ShapeDtypeStruct((n, r), F32)],
        scratch_shapes=[pltpu.VMEM((tc + 2 * SUBLANES, r), F32), chunk, chunk, carry],
        compiler_params=_cparams(("arbitrary", "arbitrary")),
        name="lru_scan_fwd",
    )(u, u, u, conv_w, conv_b, gw[0], gbh[0], lam[0])
    per_batch = nc + nl
    nsteps = nb * per_batch
    hid = ffn_wd.shape[1]

    def at(step_map, lag):
        def index_map(s):
            sc = jnp.clip(s - lag, 0, nsteps - 1)
            return step_map(sc // per_batch, sc % per_batch)
        return index_map

    ctx_res_map = lambda b, j: (b * nc + jnp.maximum(nc - 1 - j, 0), 0)
    x_res_map = lambda b, j: (b * nl + nl - 1 - jnp.maximum(j - nc, 0), 0)
    mod_map = lambda b, j: (jnp.where(j < nc, nb, b), 0, 0)
    return pl.pallas_call(
        functools.partial(_scan_bwd_ffn_kernel, tc=tc, nc=nc, per_batch=per_batch, nsteps=nsteps),
        grid=(nsteps + 1,),
        in_specs=[pl.BlockSpec((tc, r), at(bwd_map, 0))] + gate_specs + [
            pl.BlockSpec((tc, r), at(bwd_map, 0)),
            pl.BlockSpec((tc, r), at(bwd_map, 0)),
            pl.BlockSpec((tc, d), at(ctx_res_map, 0)),
            pl.BlockSpec((tc, d), at(x_res_map, 0)),
            pl.BlockSpec((1, 6, d), at(mod_map, 0)),
            _const_spec((r, d)),
            pl.BlockSpec((1, 6, d), at(mod_map, 1)),
            _const_spec((1, d)),
            pl.BlockSpec((1, d, 2 * hid), lambda s: (layer, 0, 0), pipeline_mode=pl.Buffered(1)),
            pl.BlockSpec((1, hid, d), lambda s: (layer, 0, 0), pipeline_mode=pl.Buffered(1)),
        ],
        out_specs=pl.BlockSpec((tc, d), at(bwd_map, 1)),
        out_shape=jax.ShapeDtypeStruct((n, d), F32),
        scratch_shapes=[chunk, chunk, carry, chunk, pltpu.VMEM((tc, d), F32)],
        compiler_params=_cparams(("arbitrary",)),
        name="lru_scan_bwd_ffn",
    )(xc, gw[1], gbh[1], lam[1], hf, gg, ctx2d, x2d, mods, wout, mods, ffn_g, ffn_wi, ffn_wd)


def _ffn_pieces(x, mod_ref, g_ref, wi_ref, wd_ref, *, n_chunks):
    h = _norm_mod(x, g_ref[...], mod_ref[0, 3:4, :], mod_ref[0, 4:5, :]).astype(BF16)
    hid = wd_ref.shape[1]
    n_mxu = hid // MXU_DIM
    edges = [round(c * n_mxu / n_chunks) * MXU_DIM for c in range(n_chunks)] + [hid]
    acc = [jnp.zeros(x.shape, F32)]
    pending = []

    def down():
        act, e0, e1 = pending.pop(0)
        acc[0] = acc[0] + jnp.dot(act, wd_ref[0, e0:e1, :], preferred_element_type=F32)

    def piece(e0, e1):
        def run():
            gate = jnp.dot(h, wi_ref[0, :, e0:e1], preferred_element_type=F32)
            up = jnp.dot(h, wi_ref[0, :, hid + e0:hid + e1], preferred_element_type=F32)
            if len(pending) >= FFN_LAG:
                down()
            pending.append((((gate * jax.nn.sigmoid(gate)) * up).astype(BF16), e0, e1))
        return run

    def result():
        while pending:
            down()
        return x + mod_ref[0, 5:6, :] * acc[0]

    return [piece(e0, e1) for e0, e1 in zip(edges[:-1], edges[1:])], result


def _ffn_kernel(*refs, with_attn, final_norm, n_chunks):
    if with_attn:
        x_ref, o_ref, wo_ref, mod_ref, g_ref, wi_ref, wd_ref, fg_ref, out_ref = refs
    else:
        x_ref, mod_ref, g_ref, wi_ref, wd_ref, fg_ref, out_ref = refs
    x = x_ref[...]
    if with_attn:
        x = x + mod_ref[0, 2:3, :] * jnp.dot(o_ref[...], wo_ref[...], preferred_element_type=F32)
    pieces, result = _ffn_pieces(x, mod_ref, g_ref, wi_ref, wd_ref, n_chunks=n_chunks)
    for run in pieces:
        run()
    x = result()
    if final_norm:
        x = _rmsnorm(x, fg_ref[...])
    out_ref[...] = x


def _ffn(xall, mods, g, wi, wd, fg, *, layer, nb, n_ctx_rows, seq, tm, lat_only, attn=None, final_norm=False):
    n, d = xall.shape
    hid = wd.shape[1]
    nct = n_ctx_rows // tm
    tpb = seq // tm
    off = nct if lat_only else 0
    n_out = n - n_ctx_rows if lat_only else n

    def mod_map(i):
        ii = i + off
        return (jnp.where(ii < nct, nb, (ii - nct) // tpb), 0, 0)

    in_specs = [pl.BlockSpec((tm, d), lambda i: (i + off, 0))]
    args = [xall]
    if attn is not None:
        o, wo = attn
        in_specs += [pl.BlockSpec((tm, o.shape[1]), lambda i: (i, 0)), _const_spec(wo.shape)]
        args += [o, wo]
    in_specs += [
        pl.BlockSpec((1, 6, d), mod_map),
        _const_spec((1, d)),
        pl.BlockSpec((1, d, 2 * hid), lambda i: (layer, 0, 0), pipeline_mode=pl.Buffered(1)),
        pl.BlockSpec((1, hid, d), lambda i: (layer, 0, 0), pipeline_mode=pl.Buffered(1)),
        _const_spec((1, d)),
    ]
    args += [mods, g, wi, wd, fg]
    return pl.pallas_call(
        functools.partial(_ffn_kernel, with_attn=attn is not None, final_norm=final_norm, n_chunks=2),
        grid=(n_out // tm,),
        in_specs=in_specs,
        out_specs=pl.BlockSpec((tm, d), lambda i: (i, 0)),
        out_shape=jax.ShapeDtypeStruct((n_out, d), F32),
        compiler_params=_cparams(("arbitrary",)),
        name="ffn_attn" if attn is not None else "ffn",
    )(*args)


def _mla_proj_kernel(x_ref, g_ref, mod_ref, win_ref, qg_ref, kvg_ref, wq_ref, wk_ref, wvt_ref, cos_ref, sin_ref,
                     q_ref, k_ref, v_ref, *, nct):
    i = pl.program_id(0)
    is_ctx = i < nct
    h = _norm_mod(x_ref[...], g_ref[...], mod_ref[0, 0:1, :], mod_ref[0, 1:2, :]).astype(BF16)
    c_all = jnp.dot(h, win_ref[...], preferred_element_type=F32)
    cosv = jnp.where(is_ctx, 1.0, cos_ref[...])
    sinv = jnp.where(is_ctx, 0.0, sin_ref[...])
    kv_lo = Q_LORA + KV_LORA
    lane = lax.broadcasted_iota(jnp.int32, (1, 128), 1)
    low = lane < QK_ROPE
    kt = c_all[:, kv_lo:kv_lo + 128] * jnp.where(low, cosv, sinv)
    kt = kt + pltpu.roll(kt, QK_ROPE, 1)
    k_rope = (jnp.where(low, kt, 0.0).astype(BF16), jnp.where(low, 0.0, kt).astype(BF16))
    ckv = _rmsnorm(c_all[:, Q_LORA:kv_lo], kvg_ref[...])
    k_nope = jnp.dot(ckv.astype(BF16), wk_ref[...], preferred_element_type=F32)
    v_t = jnp.dot(wvt_ref[...], ckv.T.astype(BF16), preferred_element_type=F32)
    for hd in range(MLA_HEADS):
        sl = slice(hd * 128, (hd + 1) * 128)
        k_ref[0, hd, :, 0:QK_NOPE] = k_nope[:, sl].astype(BF16)
        k_ref[0, hd, :, QK_NOPE:HEAD_PAD] = k_rope[hd % 2]
        v_ref[0, hd] = v_t[sl, :].astype(BF16)

    @pl.when(jnp.logical_not(is_ctx))
    def _():
        cq = _rmsnorm(c_all[:, :Q_LORA], qg_ref[...]).astype(BF16)
        q_all = jnp.dot(cq, wq_ref[...], preferred_element_type=F32) * Q_PRESCALE
        hw = MLA_HEADS * QK_NOPE
        hr = MLA_HEADS * QK_ROPE
        for pair in range(MLA_HEADS // 2):
            ps = slice(hw + pair * 128, hw + (pair + 1) * 128)
            ss = slice(hw + hr + pair * 128, hw + hr + (pair + 1) * 128)
            q_rope = (q_all[:, ps] * cos_ref[...] + q_all[:, ss] * sin_ref[...]).astype(BF16)
            for hd in (2 * pair, 2 * pair + 1):
                q_ref[0, hd, :, 0:QK_NOPE] = q_all[:, hd * 128:(hd + 1) * 128].astype(BF16)
                q_ref[0, hd, :, QK_NOPE:HEAD_PAD] = q_rope


def _mla_proj(xall, g, mods, win, qg, kvg, wq, wk, wvt, cos_t, sin_t, *, nb, ctx_len, seq, tm):
    n, d = xall.shape
    nct = nb * ctx_len // tm
    cpb = ctx_len // tm
    tpb = seq // tm
    t_all = ctx_len + seq

    def lat(i):
        return jnp.maximum(i - nct, 0)

    def mod_map(i):
        return (jnp.where(i < nct, nb, lat(i) // tpb), 0, 0)

    def kv_map(i):
        b = jnp.where(i < nct, i // cpb, lat(i) // tpb)
        t = jnp.where(i < nct, i % cpb, cpb + lat(i) % tpb)
        return (b, 0, t, 0)

    def vt_map(i):
        b, _, t, _ = kv_map(i)
        return (b, 0, 0, t)

    def q_map(i):
        return (lat(i) // tpb, 0, lat(i) % tpb, 0)

    def tab_map(i):
        return (lat(i) % tpb, 0)

    return pl.pallas_call(
        functools.partial(_mla_proj_kernel, nct=nct),
        grid=(n // tm,),
        in_specs=[
            pl.BlockSpec((tm, d), lambda i: (i, 0)),
            _const_spec((1, d)),
            pl.BlockSpec((1, 6, d), mod_map),
            _const_spec(win.shape),
            _const_spec((1, Q_LORA)),
            _const_spec((1, KV_LORA)),
            _const_spec(wq.shape),
            _const_spec(wk.shape),
            _const_spec(wvt.shape),
            pl.BlockSpec((tm, 128), tab_map),
            pl.BlockSpec((tm, 128), tab_map),
        ],
        out_specs=[
            pl.BlockSpec((1, MLA_HEADS, tm, HEAD_PAD), q_map),
            pl.BlockSpec((1, MLA_HEADS, tm, HEAD_PAD), kv_map),
            pl.BlockSpec((1, MLA_HEADS, V_HEAD, tm), vt_map),
        ],
        out_shape=[
            jax.ShapeDtypeStruct((nb, MLA_HEADS, seq, HEAD_PAD), BF16),
            jax.ShapeDtypeStruct((nb, MLA_HEADS, t_all, HEAD_PAD), BF16),
            jax.ShapeDtypeStruct((nb, MLA_HEADS, V_HEAD, t_all), BF16),
        ],
        compiler_params=_cparams(("arbitrary",)),
        name="mla_proj",
    )(xall, g, mods, win, qg, kvg, wq, wk, wvt, cos_t, sin_t)


def _attn_kernel(q_ref, k_ref, vt_ref, o_ref, s0_ref, s1_ref, *, tq, kc):
    t_all = k_ref.shape[2]
    nkc = t_all // kc
    nq = q_ref.shape[2] // tq
    s_refs = (s0_ref, s1_ref)

    def group_reduce(x, op):
        return op(x.reshape(kc // SUBLANES, SUBLANES, tq), axis=0)

    bounds = [round(g * nkc / QK_GROUPS) for g in range(QK_GROUPS + 1)]
    groups = [(a, b) for a, b in zip(bounds[:-1], bounds[1:]) if b > a]

    def scores(qv, c0, c1, s_ref):
        st = lax.dot_general(k_ref[0, 0, c0 * kc:c1 * kc, :], qv, (((1,), (1,)), ((), ())),
                             preferred_element_type=F32)
        s_ref[c0 * kc:c1 * kc, :] = st
        return jnp.max(st.reshape((c1 - c0) * kc // SUBLANES, SUBLANES, tq), axis=0)

    def q_tile(i):
        return q_ref[0, 0, pl.ds(pl.multiple_of(i * tq, tq), tq), :]

    def tile_step(i, slot, m8, with_next):
        s_cur, s_nxt = s_refs[slot], s_refs[1 - slot]
        m_next = jnp.full((SUBLANES, tq), -jnp.inf, F32)
        if with_next:
            q_next = q_tile(i + 1)
        m = jnp.max(m8, axis=0, keepdims=True)
        l8 = jnp.zeros((SUBLANES, tq), F32)
        acc = jnp.zeros((V_HEAD, tq), F32)

        def pv(acc, c, pb):
            return acc + jnp.dot(vt_ref[0, 0, :, c * kc:(c + 1) * kc], pb, preferred_element_type=F32)

        p_prev = None
        for c0, c1 in groups:
            if with_next:
                m_next = jnp.maximum(m_next, scores(q_next, c0, c1, s_nxt))
            for c in range(c0, c1):
                if p_prev is not None:
                    acc = pv(acc, c - 1, p_prev)
                p = jnp.exp2(s_cur[c * kc:(c + 1) * kc, :] - m)
                l8 = l8 + group_reduce(p, jnp.sum)
                p_prev = p.astype(BF16)
        acc = pv(acc, nkc - 1, p_prev)
        l = jnp.sum(l8, axis=0, keepdims=True)
        rows = pl.ds(pl.multiple_of(i * tq, tq), tq)
        o_ref[0, rows, :] = (acc / l).T.astype(o_ref.dtype)
        return m_next

    m8 = scores(q_tile(0), 0, nkc, s0_ref)

    def pair(j, m8):
        m8 = tile_step(2 * j, 0, m8, True)
        return tile_step(2 * j + 1, 1, m8, True)

    n_pairs = (nq - 1) // 2
    m8 = lax.fori_loop(0, n_pairs, pair, m8)
    if (nq - 1) % 2:
        m8 = tile_step(nq - 2, 0, m8, True)
    tile_step(nq - 1, (nq - 1) % 2, m8, False)


def _attention(q, k, v, *, tq):
    nb, nh, seq, dp = q.shape
    t_all = k.shape[2]
    kc = int(np.gcd(256, t_all))
    return pl.pallas_call(
        functools.partial(_attn_kernel, tq=tq, kc=kc),
        grid=(nb, nh),
        scratch_shapes=[pltpu.VMEM((t_all, tq), F32), pltpu.VMEM((t_all, tq), F32)],
        in_specs=[
            pl.BlockSpec((1, 1, seq, dp), lambda b, h: (b, h, 0, 0)),
            pl.BlockSpec((1, 1, t_all, dp), lambda b, h: (b, h, 0, 0)),
            pl.BlockSpec((1, 1, V_HEAD, t_all), lambda b, h: (b, h, 0, 0)),
        ],
        out_specs=pl.BlockSpec((1, seq, V_HEAD), lambda b, h: (b, 0, h)),
        out_shape=jax.ShapeDtypeStruct((nb, seq, nh * V_HEAD), BF16),
        compiler_params=_cparams(("arbitrary", "arbitrary")),
        name="mla_attention",
    )(q, k, v)


def _rope_cols(w):
    qd = QK_ROPE // 4
    r0, r1, r2, r3 = (w[..., i * qd:(i + 1) * qd] for i in range(4))
    return jnp.concatenate([r0, r2, r1, r3], axis=-1), jnp.concatenate([-r1, -r3, r0, r2], axis=-1)


def _rope_tables(seq):
    rows = seq // GRID_W
    row_ids = np.repeat(np.arange(rows, dtype=np.float32), GRID_W)
    col_ids = np.tile(np.arange(GRID_W, dtype=np.float32), rows)
    axis_dim = QK_ROPE // 2
    expo = (np.arange(0, axis_dim, 2, dtype=np.float32) / np.float32(axis_dim)).astype(np.float32)
    inv_freq = (np.float32(1.0) / np.power(np.float32(ROPE_THETA), expo)).astype(np.float32)
    ang_r = (row_ids[:, None] * inv_freq).astype(np.float32)
    ang_c = (col_ids[:, None] * inv_freq).astype(np.float32)
    reps = 128 // (QK_ROPE // 2)
    cos_t = np.tile(np.concatenate([np.cos(ang_r), np.cos(ang_c)], axis=1), (1, reps))
    sin_t = np.tile(np.concatenate([np.sin(ang_r), np.sin(ang_c)], axis=1), (1, reps))
    return jnp.asarray(cos_t, F32), jnp.asarray(sin_t, F32)


def kernel(x, c, ctx, c_ctx, ada_w, ada_b, norm_mix_g, norm_ffn_g, ffn_w_in, ffn_w_out, lru_w_in, lru_conv_w,
           lru_conv_b, lru_gate_w, lru_gate_b, lru_lambda, lru_w_out, mla_w_in, mla_q_norm_g, mla_kv_norm_g,
           mla_w_uq, mla_w_ukv, mla_w_o, final_norm_g):
    nb, seq, d = x.shape
    ctx_len = ctx.shape[1]
    depth = ada_w.shape[0]
    assert depth == 2 and d == D_MODEL and nb + 1 <= MOD_ROWS
    n_ctx_rows = nb * ctx_len
    tile = lambda cap: int(np.gcd(np.gcd(cap, n_ctx_rows), seq))

    ctx2d = ctx.reshape(n_ctx_rows, d)
    x2d = x.reshape(nb * seq, d)
    ffn_wi = ffn_w_in.astype(BF16)
    ffn_wd = ffn_w_out.astype(BF16)

    cv = jnp.concatenate([c, c_ctx[None, :], jnp.zeros((MOD_ROWS - nb - 1, d), F32)], axis=0)
    mods = _ada_mods(cv, ada_w, ada_b).reshape(depth, MOD_ROWS, 6, d)

    tm = tile(1024)
    gg, u = _lru_in(ctx2d, x2d, norm_mix_g[0][None], mods[0], lru_w_in[0].astype(BF16), nb=nb, seq=seq, tm=tm)
    tc = int(np.gcd(np.gcd(256, ctx_len), seq))
    gw = lru_gate_w[0]
    gw = (0.5 * jnp.concatenate([gw[:, 0], gw[:, 1]], axis=-1)).astype(BF16)
    x2 = _lru_scans_ffn(u, lru_conv_w[0], lru_conv_b[0][None], gw, 0.5 * lru_gate_b[0], lru_lambda[0][:, None, :],
                        gg, ctx2d, x2d, mods[0], lru_w_out[0].astype(BF16), norm_ffn_g[0][None], ffn_wi, ffn_wd,
                        layer=0, nb=nb, ctx_len=ctx_len, seq=seq, tc=tc)
    tf = tile(512)

    w_in = mla_w_in[0]
    win_p = jnp.concatenate((w_in[:, :Q_LORA + KV_LORA],) + _rope_cols(w_in[:, Q_LORA + KV_LORA:]),
                            axis=1).astype(BF16)
    wq3 = mla_w_uq[0].reshape(Q_LORA, MLA_HEADS, QK_NOPE + QK_ROPE)
    wq_all = jnp.concatenate([part.reshape(Q_LORA, -1)
                              for part in (wq3[:, :, :QK_NOPE],) + _rope_cols(wq3[:, :, QK_NOPE:])],
                             axis=1).astype(BF16)
    cos_t, sin_t = _rope_tables(seq)
    tp = int(np.gcd(np.gcd(256, ctx_len), seq))
    wkv3 = mla_w_ukv[0].reshape(KV_LORA, MLA_HEADS, QK_NOPE + V_HEAD)
    wk = wkv3[:, :, :QK_NOPE].reshape(KV_LORA, MLA_HEADS * QK_NOPE).astype(BF16)
    wvt = wkv3[:, :, QK_NOPE:].reshape(KV_LORA, MLA_HEADS * V_HEAD).T.astype(BF16)
    q, k, vt = _mla_proj(x2, norm_mix_g[1][None], mods[1], win_p, mla_q_norm_g[0][None], mla_kv_norm_g[0][None],
                         wq_all, wk, wvt, cos_t, sin_t, nb=nb, ctx_len=ctx_len, seq=seq, tm=tp)
    o = _attention(q, k, vt, tq=int(np.gcd(512, seq)))
    out = _ffn(x2, mods[1], norm_ffn_g[1][None], ffn_wi, ffn_wd, final_norm_g[None], layer=1,
               nb=nb, n_ctx_rows=n_ctx_rows, seq=seq, tm=tf, lat_only=True,
               attn=(o.reshape(nb * seq, MLA_HEADS * V_HEAD), mla_w_o[0].astype(BF16)), final_norm=True)
    return out.reshape(nb, seq, d)
```

```python
import functools

import numpy as np
import jax
import jax.numpy as jnp
from jax import lax
from jax.experimental import pallas as pl
from jax.experimental.pallas import tpu as pltpu

F32 = jnp.float32
BF16 = jnp.bfloat16

D_MODEL = 1024
GRID_W = 64
NORM_EPS = 1e-6
LRU_WIDTH = D_MODEL
LRU_BLOCK = 256
LRU_BLOCKS = LRU_WIDTH // LRU_BLOCK
CONV_W = 4
RG_C = 8.0
MLA_HEADS = 8
Q_LORA = 384
KV_LORA = 256
QK_NOPE = 128
QK_ROPE = 64
V_HEAD = 128
SM_SCALE = (QK_NOPE + QK_ROPE) ** -0.5
Q_PRESCALE = SM_SCALE * float(np.log2(np.e))
ROPE_THETA = 10000.0

MOD_ROWS = 8
SUBLANES = 8
MXU_DIM = 256
HEAD_PAD = 256
VMEM_LIMIT_MB = 56
FFN_LAG = 2
TINY = 1e-30


def _cparams(sem, vmem_mb=VMEM_LIMIT_MB):
    return pltpu.CompilerParams(dimension_semantics=sem, vmem_limit_bytes=vmem_mb << 20)


def _const_spec(shape):
    nd = len(shape)
    return pl.BlockSpec(shape, lambda *_: (0,) * nd)


def _norm_mod(x, g, shift, scale):
    ms = jnp.mean(x * x, axis=-1, keepdims=True)
    y = (x * lax.rsqrt(ms + NORM_EPS)) * g
    return y * (1.0 + scale) + shift


def _rmsnorm(x, g):
    ms = jnp.mean(x * x, axis=-1, keepdims=True)
    return (x * lax.rsqrt(ms + NORM_EPS)) * g


def _ada_kernel(cv_ref, w_ref, b_ref, o_ref):
    cv = cv_ref[...]
    s = cv * jax.nn.sigmoid(cv)
    o_ref[0] = jnp.dot(s.astype(BF16), w_ref[0].astype(BF16), preferred_element_type=F32) + b_ref[0]


def _ada_mods(cv, ada_w, ada_b):
    depth, d, n = ada_w.shape
    tn = 1536
    return pl.pallas_call(
        _ada_kernel,
        grid=(depth, n // tn),
        in_specs=[
            pl.BlockSpec((MOD_ROWS, d), lambda l, j: (0, 0)),
            pl.BlockSpec((1, d, tn), lambda l, j: (l, 0, j)),
            pl.BlockSpec((1, 1, tn), lambda l, j: (l, 0, j)),
        ],
        out_specs=pl.BlockSpec((1, MOD_ROWS, tn), lambda l, j: (l, 0, j)),
        out_shape=jax.ShapeDtypeStruct((depth, MOD_ROWS, n), F32),
        compiler_params=_cparams(("arbitrary", "arbitrary")),
        name="ada_mods",
    )(cv, ada_w, ada_b.reshape(depth, 1, n))


def _lru_in_kernel(ctx_ref, x_ref, g_ref, mod_ref, w_ref, gg_ref, u_ref, *, nct):
    xin = jnp.where(pl.program_id(0) < nct, ctx_ref[...], x_ref[...])
    h = _norm_mod(xin, g_ref[...], mod_ref[0, 0:1, :], mod_ref[0, 1:2, :])
    y = jnp.dot(h.astype(BF16), w_ref[...], preferred_element_type=F32)
    r = LRU_WIDTH
    gg_ref[...] = jax.nn.gelu(y[:, :r])
    u_ref[...] = y[:, r:]


def _lru_in(ctx2d, x2d, g, mods, w, *, nb, seq, tm):
    d = x2d.shape[1]
    n = ctx2d.shape[0] + x2d.shape[0]
    r = LRU_WIDTH
    nct = ctx2d.shape[0] // tm
    tpb = seq // tm

    def mod_map(i):
        return (jnp.where(i < nct, nb, (i - nct) // tpb), 0, 0)

    return pl.pallas_call(
        functools.partial(_lru_in_kernel, nct=nct),
        grid=(n // tm,),
        in_specs=[
            pl.BlockSpec((tm, d), lambda i: (jnp.minimum(i, nct - 1), 0)),
            pl.BlockSpec((tm, d), lambda i: (jnp.maximum(i - nct, 0), 0)),
            _const_spec((1, d)),
            pl.BlockSpec((1, 6, d), mod_map),
            _const_spec((d, 2 * r)),
        ],
        out_specs=[pl.BlockSpec((tm, r), lambda i: (i, 0)), pl.BlockSpec((tm, r), lambda i: (i, 0))],
        out_shape=[jax.ShapeDtypeStruct((n, r), F32), jax.ShapeDtypeStruct((n, r), F32)],
        compiler_params=_cparams(("arbitrary",)),
        name="lru_in",
    )(ctx2d, x2d, g, mods, w)


def _lru_coeffs(xc, gw, gbh, lam, a_s, b_s, between=None):
    xcb = xc.astype(BF16)
    neg = -lam[...]
    sp = jnp.maximum(neg, 0.0) + jnp.log1p(jnp.exp(-jnp.abs(neg)))
    c2 = (-0.5 * RG_C) * sp
    for n in range(LRU_BLOCKS):
        sl = slice(n * LRU_BLOCK, (n + 1) * LRU_BLOCK)
        pre = jnp.dot(xcb[:, sl], gw[n], preferred_element_type=F32)
        t_r = jnp.tanh(pre[:, :LRU_BLOCK] + gbh[0:1, sl])
        t_i = jnp.tanh(pre[:, LRU_BLOCK:] + gbh[1:2, sl])
        a = jnp.exp(c2[:, sl] * t_r + c2[:, sl])
        gap = 1.0 - a * a
        mult = gap * lax.rsqrt(jnp.maximum(gap, TINY))
        a_s[:, sl] = a
        b_s[:, sl] = (mult * (0.5 * t_i + 0.5)) * xc[:, sl]
        if between is not None:
            between()


def _lru_chunk_scan(a_s, b_s, h_s, hc, *, reverse, tc, between=None):
    r = LRU_WIDTH
    row = lax.broadcasted_iota(jnp.int32, (SUBLANES, r), 0)
    ngroups = tc // SUBLANES

    def group(gi, hprev):
        g0 = (ngroups - 1 - gi) if reverse else gi
        start = g0 * SUBLANES
        rows = pl.ds(start if isinstance(start, int) else pl.multiple_of(start, SUBLANES), SUBLANES)
        av = a_s[rows, :]
        bv = b_s[rows, :]
        for s in (1, 2, 4):
            if reverse:
                keep = row < (SUBLANES - s)
                shift = SUBLANES - s
            else:
                keep = row >= s
                shift = s
            a_sh = jnp.where(keep, pltpu.roll(av, shift, 0), 1.0)
            b_sh = jnp.where(keep, pltpu.roll(bv, shift, 0), 0.0)
            bv = av * b_sh + bv
            av = av * a_sh
        hrows = av * hprev + bv
        h_s[rows, :] = hrows
        edge = hrows[0:1, :] if reverse else hrows[SUBLANES - 1:SUBLANES, :]
        return jnp.broadcast_to(edge, (SUBLANES, r))

    if between is None:
        hc[...] = lax.fori_loop(0, ngroups, group, hc[...], unroll=2)
    else:
        h = hc[...]
        for gi in range(ngroups):
            h = group(gi, h)
            between(gi)
        hc[...] = h


def _scan_fwd_kernel(ucur, uprev, unext, cw, cb, gw, gbh, lam, hf, xc_out, ext, a_s, b_s, hc, *, tc, nc, nl):
    j = pl.program_id(1)
    in_ctx = j < nc
    pos = jnp.where(in_ctx, j, j - nc)
    first = pos == 0
    last = pos == jnp.where(in_ctx, nc, nl) - 1

    @pl.when(j == 0)
    def _():
        hc[...] = jnp.zeros_like(hc)

    n_ext = tc + 2 * SUBLANES
    ext[0:SUBLANES, :] = jnp.where(first, 0.0, uprev[...])
    ext[SUBLANES:SUBLANES + tc, :] = ucur[...]
    ext[SUBLANES + tc:n_ext, :] = jnp.where(last, 0.0, unext[...])
    e = ext[...]
    mid = slice(SUBLANES, SUBLANES + tc)
    xc = cb[...] + cw[0:1, :] * pltpu.roll(e, 1, 0)[mid] + cw[1:2, :] * e[mid] \
        + cw[2:3, :] * pltpu.roll(e, n_ext - 1, 0)[mid] + cw[3:4, :] * pltpu.roll(e, n_ext - 2, 0)[mid]
    xc_out[...] = xc
    _lru_coeffs(xc, gw, gbh, lam, a_s, b_s)
    _lru_chunk_scan(a_s, b_s, hf, hc, reverse=False, tc=tc)


def _scan_bwd_ffn_kernel(xc, gw, gbh, lam, hf, gg, ctx_res, x_res, mod, wout, mod_prev, fg, wi, wd,
                         out, a_s, b_s, hc, hb_s, x1_prev, *, tc, nc, per_batch, nsteps):
    s = pl.program_id(0)
    j = jnp.minimum(s, nsteps - 1) % per_batch

    @pl.when(s == 0)
    def _():
        x1_prev[...] = jnp.zeros_like(x1_prev)

    @pl.when(j == 0)
    def _():
        hc[...] = jnp.zeros_like(hc)

    pieces, ffn_result = _ffn_pieces(x1_prev[...], mod_prev, fg, wi, wd, n_chunks=wd.shape[1] // MXU_DIM)
    todo = list(pieces)
    run_next = lambda: todo.pop(0)() if todo else None
    run_two = lambda: (run_next(), run_next())
    run_two()
    _lru_coeffs(xc[...], gw, gbh, lam, a_s, b_s, between=run_two)
    ngroups = tc // SUBLANES
    n_scan = len(todo)

    def between(gi):
        while todo and (n_scan - len(todo)) * ngroups < (gi + 1) * n_scan:
            run_next()

    _lru_chunk_scan(a_s, b_s, hb_s, hc, reverse=True, tc=tc, between=between)
    out[...] = ffn_result()
    z = ((hf[...] + hb_s[...]) * gg[...]).astype(BF16)
    res = jnp.where(j < nc, ctx_res[...], x_res[...])
    x1_prev[...] = res + mod[0, 2:3, :] * jnp.dot(z, wout[...], preferred_element_type=F32)


def _lru_scans_ffn(u, conv_w, conv_b, gw, gbh, lam, gg, ctx2d, x2d, mods, wout, ffn_g, ffn_wi, ffn_wd, *,
                   layer, nb, ctx_len, seq, tc):
    n, r = u.shape
    d = x2d.shape[1]
    nc = ctx_len // tc
    nl = seq // tc
    blk8 = tc // SUBLANES

    def cur_blk(b, j, reverse):
        jc = (nc - 1 - j) if reverse else j
        jl = (nl - 1 - (j - nc)) if reverse else (j - nc)
        return jnp.where(j < nc, b * nc + jc, nb * nc + b * nl + jl)

    fwd_map = lambda b, j: (cur_blk(b, j, False), 0)
    bwd_map = lambda b, j: (cur_blk(b, j, True), 0)
    prev_map = lambda b, j: (jnp.maximum(cur_blk(b, j, False) * blk8 - 1, 0), 0)
    next_map = lambda b, j: (jnp.minimum((cur_blk(b, j, False) + 1) * blk8, n // SUBLANES - 1), 0)
    gate_specs = [
        _const_spec((LRU_BLOCKS, LRU_BLOCK, 2 * LRU_BLOCK)),
        _const_spec((2, r)),
        _const_spec((1, r)),
    ]
    chunk = pltpu.VMEM((tc, r), F32)
    carry = pltpu.VMEM((SUBLANES, r), F32)
    hf, xc = pl.pallas_call(
        functools.partial(_scan_fwd_kernel, tc=tc, nc=nc, nl=nl),
        grid=(nb, nc + nl),
        in_specs=[
            pl.BlockSpec((tc, r), fwd_map),
            pl.BlockSpec((SUBLANES, r), prev_map),
            pl.BlockSpec((SUBLANES, r), next_map),
            _const_spec((CONV_W, r)),
            _const_spec((1, r)),
        ] + gate_specs,
        out_specs=[pl.BlockSpec((tc, r), fwd_map), pl.BlockSpec((tc, r), fwd_map)],
        out_shape=[jax.ShapeDtypeStruct((n, r), F32), jax.ShapeDtypeStruct((n, r), F32)],
        scratch_shapes=[pltpu.VMEM((tc + 2 * SUBLANES, r), F32), chunk, chunk, carry],
        compiler_params=_cparams(("arbitrary", "arbitrary")),
        name="lru_scan_fwd",
    )(u, u, u, conv_w, conv_b, gw[0], gbh[0], lam[0])
    per_batch = nc + nl
    nsteps = nb * per_batch
    hid = ffn_wd.shape[1]

    def at(step_map, lag):
        def index_map(s):
            sc = jnp.clip(s - lag, 0, nsteps - 1)
            return step_map(sc // per_batch, sc % per_batch)
        return index_map

    ctx_res_map = lambda b, j: (b * nc + jnp.maximum(nc - 1 - j, 0), 0)
    x_res_map = lambda b, j: (b * nl + nl - 1 - jnp.maximum(j - nc, 0), 0)
    mod_map = lambda b, j: (jnp.where(j < nc, nb, b), 0, 0)
    return pl.pallas_call(
        functools.partial(_scan_bwd_ffn_kernel, tc=tc, nc=nc, per_batch=per_batch, nsteps=nsteps),
        grid=(nsteps + 1,),
        in_specs=[pl.BlockSpec((tc, r), at(bwd_map, 0))] + gate_specs + [
            pl.BlockSpec((tc, r), at(bwd_map, 0)),
            pl.BlockSpec((tc, r), at(bwd_map, 0)),
            pl.BlockSpec((tc, d), at(ctx_res_map, 0)),
            pl.BlockSpec((tc, d), at(x_res_map, 0)),
            pl.BlockSpec((1, 6, d), at(mod_map, 0)),
            _const_spec((r, d)),
            pl.BlockSpec((1, 6, d), at(mod_map, 1)),
            _const_spec((1, d)),
            pl.BlockSpec((1, d, 2 * hid), lambda s: (layer, 0, 0), pipeline_mode=pl.Buffered(1)),
            pl.BlockSpec((1, hid, d), lambda s: (layer, 0, 0), pipeline_mode=pl.Buffered(1)),
        ],
        out_specs=pl.BlockSpec((tc, d), at(bwd_map, 1)),
        out_shape=jax.ShapeDtypeStruct((n, d), F32),
        scratch_shapes=[chunk, chunk, carry, chunk, pltpu.VMEM((tc, d), F32)],
        compiler_params=_cparams(("arbitrary",)),
        name="lru_scan_bwd_ffn",
    )(xc, gw[1], gbh[1], lam[1], hf, gg, ctx2d, x2d, mods, wout, mods, ffn_g, ffn_wi, ffn_wd)


def _ffn_pieces(x, mod_ref, g_ref, wi_ref, wd_ref, *, n_chunks):
    h = _norm_mod(x, g_ref[...], mod_ref[0, 3:4, :], mod_ref[0, 4:5, :]).astype(BF16)
    hid = wd_ref.shape[1]
    n_mxu = hid // MXU_DIM
    edges = [round(c * n_mxu / n_chunks) * MXU_DIM for c in range(n_chunks)] + [hid]
    acc = [jnp.zeros(x.shape, F32)]
    pending = []

    def down():
        act, e0, e1 = pending.pop(0)
        acc[0] = acc[0] + jnp.dot(act, wd_ref[0, e0:e1, :], preferred_element_type=F32)

    def piece(e0, e1):
        def run():
            gate = jnp.dot(h, wi_ref[0, :, e0:e1], preferred_element_type=F32)
            up = jnp.dot(h, wi_ref[0, :, hid + e0:hid + e1], preferred_element_type=F32)
            if len(pending) >= FFN_LAG:
                down()
            pending.append((((gate * jax.nn.sigmoid(gate)) * up).astype(BF16), e0, e1))
        return run

    def result():
        while pending:
            down()
        return x + mod_ref[0, 5:6, :] * acc[0]

    return [piece(e0, e1) for e0, e1 in zip(edges[:-1], edges[1:])], result


def _ffn_kernel(*refs, with_attn, final_norm, n_chunks):
    if with_attn:
        x_ref, o_ref, wo_ref, mod_ref, g_ref, wi_ref, wd_ref, fg_ref, out_ref = refs
    else:
        x_ref, mod_ref, g_ref, wi_ref, wd_ref, fg_ref, out_ref = refs
    x = x_ref[...]
    if with_attn:
        x = x + mod_ref[0, 2:3, :] * jnp.dot(o_ref[...], wo_ref[...], preferred_element_type=F32)
    pieces, result = _ffn_pieces(x, mod_ref, g_ref, wi_ref, wd_ref, n_chunks=n_chunks)
    for run in pieces:
        run()
    x = result()
    if final_norm:
        x = _rmsnorm(x, fg_ref[...])
    out_ref[...] = x


def _ffn(xall, mods, g, wi, wd, fg, *, layer, nb, n_ctx_rows, seq, tm, lat_only, attn=None, final_norm=False):
    n, d = xall.shape
    hid = wd.shape[1]
    nct = n_ctx_rows // tm
    tpb = seq // tm
    off = nct if lat_only else 0
    n_out = n - n_ctx_rows if lat_only else n

    def mod_map(i):
        ii = i + off
        return (jnp.where(ii < nct, nb, (ii - nct) // tpb), 0, 0)

    in_specs = [pl.BlockSpec((tm, d), lambda i: (i + off, 0))]
    args = [xall]
    if attn is not None:
        o, wo = attn
        in_specs += [pl.BlockSpec((tm, o.shape[1]), lambda i: (i, 0)), _const_spec(wo.shape)]
        args += [o, wo]
    in_specs += [
        pl.BlockSpec((1, 6, d), mod_map),
        _const_spec((1, d)),
        pl.BlockSpec((1, d, 2 * hid), lambda i: (layer, 0, 0), pipeline_mode=pl.Buffered(1)),
        pl.BlockSpec((1, hid, d), lambda i: (layer, 0, 0), pipeline_mode=pl.Buffered(1)),
        _const_spec((1, d)),
    ]
    args += [mods, g, wi, wd, fg]
    return pl.pallas_call(
        functools.partial(_ffn_kernel, with_attn=attn is not None, final_norm=final_norm, n_chunks=2),
        grid=(n_out // tm,),
        in_specs=in_specs,
        out_specs=pl.BlockSpec((tm, d), lambda i: (i, 0)),
        out_shape=jax.ShapeDtypeStruct((n_out, d), F32),
        compiler_params=_cparams(("arbitrary",)),
        name="ffn_attn" if attn is not None else "ffn",
    )(*args)


def _mla_proj_kernel(*refs, latent):
    if latent:
        (x_ref, g_ref, mod_ref, win_ref, kvg_ref, wk_ref, wvt_ref, qg_ref, wq_ref, cos_ref, sin_ref,
         k_ref, v_ref, q_ref) = refs
        cosv, sinv = cos_ref[...], sin_ref[...]
    else:
        x_ref, g_ref, mod_ref, win_ref, kvg_ref, wk_ref, wvt_ref, k_ref, v_ref = refs
        cosv, sinv = 1.0, 0.0
    h = _norm_mod(x_ref[...], g_ref[...], mod_ref[0, 0:1, :], mod_ref[0, 1:2, :]).astype(BF16)
    c_all = jnp.dot(h, win_ref[...], preferred_element_type=F32)
    kv_lo = Q_LORA + KV_LORA
    lane = lax.broadcasted_iota(jnp.int32, (1, 128), 1)
    low = lane < QK_ROPE
    kt = c_all[:, kv_lo:kv_lo + 128] * jnp.where(low, cosv, sinv)
    kt = kt + pltpu.roll(kt, QK_ROPE, 1)
    k_rope = (jnp.where(low, kt, 0.0).astype(BF16), jnp.where(low, 0.0, kt).astype(BF16))
    ckv = _rmsnorm(c_all[:, Q_LORA:kv_lo], kvg_ref[...])
    k_nope = jnp.dot(ckv.astype(BF16), wk_ref[...], preferred_element_type=F32)
    v_t = jnp.dot(wvt_ref[...], ckv.T.astype(BF16), preferred_element_type=F32)
    for hd in range(MLA_HEADS):
        sl = slice(hd * 128, (hd + 1) * 128)
        k_ref[0, hd, :, 0:QK_NOPE] = k_nope[:, sl].astype(BF16)
        k_ref[0, hd, :, QK_NOPE:HEAD_PAD] = k_rope[hd % 2]
        v_ref[0, hd] = v_t[sl, :].astype(BF16)
    if latent:
        cq = _rmsnorm(c_all[:, :Q_LORA], qg_ref[...]).astype(BF16)
        q_all = jnp.dot(cq, wq_ref[...], preferred_element_type=F32) * Q_PRESCALE
        hw = MLA_HEADS * QK_NOPE
        hr = MLA_HEADS * QK_ROPE
        for pair in range(MLA_HEADS // 2):
            ps = slice(hw + pair * 128, hw + (pair + 1) * 128)
            ss = slice(hw + hr + pair * 128, hw + hr + (pair + 1) * 128)
            q_rope = (q_all[:, ps] * cosv + q_all[:, ss] * sinv).astype(BF16)
            for hd in (2 * pair, 2 * pair + 1):
                q_ref[0, hd, :, 0:QK_NOPE] = q_all[:, hd * 128:(hd + 1) * 128].astype(BF16)
                q_ref[0, hd, :, QK_NOPE:HEAD_PAD] = q_rope


def _mla_proj(xall, g, mods, win, kvg, wk, wvt, *, nb, ctx_len, seq, tm, latent_args=None):
    n, d = xall.shape
    latent = latent_args is not None
    rows = seq if latent else ctx_len
    off = nb * ctx_len // tm if latent else 0
    tpb = rows // tm
    kv_map = lambda i: (i // tpb, 0, i % tpb, 0)
    vt_map = lambda i: (i // tpb, 0, 0, i % tpb)
    in_specs = [
        pl.BlockSpec((tm, d), lambda i: (i + off, 0)),
        _const_spec((1, d)),
        pl.BlockSpec((1, 6, d), lambda i: ((i // tpb) if latent else nb, 0, 0)),
        _const_spec(win.shape),
        _const_spec((1, KV_LORA)),
        _const_spec(wk.shape),
        _const_spec(wvt.shape),
    ]
    args = [xall, g, mods, win, kvg, wk, wvt]
    out_specs = [pl.BlockSpec((1, MLA_HEADS, tm, HEAD_PAD), kv_map), pl.BlockSpec((1, MLA_HEADS, V_HEAD, tm), vt_map)]
    out_shape = [jax.ShapeDtypeStruct((nb, MLA_HEADS, rows, HEAD_PAD), BF16),
                 jax.ShapeDtypeStruct((nb, MLA_HEADS, V_HEAD, rows), BF16)]
    if latent:
        qg, wq, cos_t, sin_t = latent_args
        tab_map = lambda i: (i % tpb, 0)
        in_specs += [_const_spec((1, Q_LORA)), _const_spec(wq.shape),
                     pl.BlockSpec((tm, 128), tab_map), pl.BlockSpec((tm, 128), tab_map)]
        args += [qg, wq, cos_t, sin_t]
        out_specs.append(pl.BlockSpec((1, MLA_HEADS, tm, HEAD_PAD), kv_map))
        out_shape.append(jax.ShapeDtypeStruct((nb, MLA_HEADS, rows, HEAD_PAD), BF16))
    return pl.pallas_call(
        functools.partial(_mla_proj_kernel, latent=latent),
        grid=(nb * tpb,),
        in_specs=in_specs,
        out_specs=out_specs,
        out_shape=out_shape,
        compiler_params=_cparams(("arbitrary",)),
        name="mla_proj_lat" if latent else "mla_proj_ctx",
    )(*args)


def _attn_kernel(q_ref, k_ref, kc_ref, vt_ref, vtc_ref, o_ref, s0_ref, s1_ref, *, tq, kc):
    n_lat = k_ref.shape[2]
    t_all = n_lat + kc_ref.shape[2]
    nkc = t_all // kc
    nq = q_ref.shape[2] // tq
    s_refs = (s0_ref, s1_ref)
    dims = (((1,), (1,)), ((), ()))

    def group_reduce(x, op):
        return op(x.reshape(x.shape[0] // SUBLANES, SUBLANES, tq), axis=0)

    def scores(qv, s_ref):
        st = lax.dot_general(k_ref[0, 0], qv, dims, preferred_element_type=F32)
        sc = lax.dot_general(kc_ref[0, 0], qv, dims, preferred_element_type=F32)
        s_ref[0:n_lat, :] = st
        s_ref[n_lat:t_all, :] = sc
        return jnp.maximum(group_reduce(st, jnp.max), group_reduce(sc, jnp.max))

    def vt_chunk(c):
        if c * kc < n_lat:
            return vt_ref[0, 0, :, c * kc:(c + 1) * kc]
        return vtc_ref[0, 0, :, c * kc - n_lat:(c + 1) * kc - n_lat]

    def q_tile(i):
        return q_ref[0, 0, pl.ds(pl.multiple_of(i * tq, tq), tq), :]

    def tile_step(i, slot, m8, with_next):
        s_cur, s_nxt = s_refs[slot], s_refs[1 - slot]
        m_next = scores(q_tile(i + 1), s_nxt) if with_next else None
        m = jnp.max(m8, axis=0, keepdims=True)
        l8 = jnp.zeros((SUBLANES, tq), F32)
        acc = jnp.zeros((V_HEAD, tq), F32)
        p_prev = None
        for c in range(nkc):
            if p_prev is not None:
                acc = acc + jnp.dot(vt_chunk(c - 1), p_prev, preferred_element_type=F32)
            p = jnp.exp2(s_cur[c * kc:(c + 1) * kc, :] - m)
            l8 = l8 + group_reduce(p, jnp.sum)
            p_prev = p.astype(BF16)
        acc = acc + jnp.dot(vt_chunk(nkc - 1), p_prev, preferred_element_type=F32)
        l = jnp.sum(l8, axis=0, keepdims=True)
        rows = pl.ds(pl.multiple_of(i * tq, tq), tq)
        o_ref[0, rows, :] = (acc / l).T.astype(o_ref.dtype)
        return m_next

    m8 = scores(q_tile(0), s0_ref)

    def pair(j, m8):
        m8 = tile_step(2 * j, 0, m8, True)
        return tile_step(2 * j + 1, 1, m8, True)

    n_pairs = (nq - 1) // 2
    m8 = lax.fori_loop(0, n_pairs, pair, m8)
    if (nq - 1) % 2:
        m8 = tile_step(nq - 2, 0, m8, True)
    tile_step(nq - 1, (nq - 1) % 2, m8, False)


def _attention(q, k, kc, vt, vtc, *, tq):
    nb, nh, seq, dp = q.shape
    ctx_len = kc.shape[2]
    t_all = seq + ctx_len
    chunk = int(np.gcd(np.gcd(MXU_DIM, seq), ctx_len))
    head = lambda b, h: (b, h, 0, 0)
    return pl.pallas_call(
        functools.partial(_attn_kernel, tq=tq, kc=chunk),
        grid=(nb, nh),
        scratch_shapes=[pltpu.VMEM((t_all, tq), F32), pltpu.VMEM((t_all, tq), F32)],
        in_specs=[
            pl.BlockSpec((1, 1, seq, dp), head),
            pl.BlockSpec((1, 1, seq, dp), head),
            pl.BlockSpec((1, 1, ctx_len, dp), head),
            pl.BlockSpec((1, 1, V_HEAD, seq), head),
            pl.BlockSpec((1, 1, V_HEAD, ctx_len), head),
        ],
        out_specs=pl.BlockSpec((1, seq, V_HEAD), lambda b, h: (b, 0, h)),
        out_shape=jax.ShapeDtypeStruct((nb, seq, nh * V_HEAD), BF16),
        compiler_params=_cparams(("arbitrary", "arbitrary")),
        name="mla_attention",
    )(q, k, kc, vt, vtc)


def _rope_cols(w):
    qd = QK_ROPE // 4
    r0, r1, r2, r3 = (w[..., i * qd:(i + 1) * qd] for i in range(4))
    return jnp.concatenate([r0, r2, r1, r3], axis=-1), jnp.concatenate([-r1, -r3, r0, r2], axis=-1)


def _rope_tables(seq):
    rows = seq // GRID_W
    row_ids = np.repeat(np.arange(rows, dtype=np.float32), GRID_W)
    col_ids = np.tile(np.arange(GRID_W, dtype=np.float32), rows)
    axis_dim = QK_ROPE // 2
    expo = (np.arange(0, axis_dim, 2, dtype=np.float32) / np.float32(axis_dim)).astype(np.float32)
    inv_freq = (np.float32(1.0) / np.power(np.float32(ROPE_THETA), expo)).astype(np.float32)
    ang_r = (row_ids[:, None] * inv_freq).astype(np.float32)
    ang_c = (col_ids[:, None] * inv_freq).astype(np.float32)
    reps = 128 // (QK_ROPE // 2)
    cos_t = np.tile(np.concatenate([np.cos(ang_r), np.cos(ang_c)], axis=1), (1, reps))
    sin_t = np.tile(np.concatenate([np.sin(ang_r), np.sin(ang_c)], axis=1), (1, reps))
    return jnp.asarray(cos_t, F32), jnp.asarray(sin_t, F32)


def kernel(x, c, ctx, c_ctx, ada_w, ada_b, norm_mix_g, norm_ffn_g, ffn_w_in, ffn_w_out, lru_w_in, lru_conv_w,
           lru_conv_b, lru_gate_w, lru_gate_b, lru_lambda, lru_w_out, mla_w_in, mla_q_norm_g, mla_kv_norm_g,
           mla_w_uq, mla_w_ukv, mla_w_o, final_norm_g):
    nb, seq, d = x.shape
    ctx_len = ctx.shape[1]
    depth = ada_w.shape[0]
    assert depth == 2 and d == D_MODEL and nb + 1 <= MOD_ROWS
    n_ctx_rows = nb * ctx_len
    tile = lambda cap: int(np.gcd(np.gcd(cap, n_ctx_rows), seq))

    ctx2d = ctx.reshape(n_ctx_rows, d)
    x2d = x.reshape(nb * seq, d)
    ffn_wi = ffn_w_in.astype(BF16)
    ffn_wd = ffn_w_out.astype(BF16)

    cv = jnp.concatenate([c, c_ctx[None, :], jnp.zeros((MOD_ROWS - nb - 1, d), F32)], axis=0)
    mods = _ada_mods(cv, ada_w, ada_b).reshape(depth, MOD_ROWS, 6, d)

    tm = tile(1024)
    gg, u = _lru_in(ctx2d, x2d, norm_mix_g[0][None], mods[0], lru_w_in[0].astype(BF16), nb=nb, seq=seq, tm=tm)
    tc = int(np.gcd(np.gcd(256, ctx_len), seq))
    gw = lru_gate_w[0]
    gw = (0.5 * jnp.concatenate([gw[:, 0], gw[:, 1]], axis=-1)).astype(BF16)
    x2 = _lru_scans_ffn(u, lru_conv_w[0], lru_conv_b[0][None], gw, 0.5 * lru_gate_b[0], lru_lambda[0][:, None, :],
                        gg, ctx2d, x2d, mods[0], lru_w_out[0].astype(BF16), norm_ffn_g[0][None], ffn_wi, ffn_wd,
                        layer=0, nb=nb, ctx_len=ctx_len, seq=seq, tc=tc)
    tf = tile(512)

    w_in = mla_w_in[0]
    win_p = jnp.concatenate((w_in[:, :Q_LORA + KV_LORA],) + _rope_cols(w_in[:, Q_LORA + KV_LORA:]),
                            axis=1).astype(BF16)
    wq3 = mla_w_uq[0].reshape(Q_LORA, MLA_HEADS, QK_NOPE + QK_ROPE)
    wq_all = jnp.concatenate([part.reshape(Q_LORA, -1)
                              for part in (wq3[:, :, :QK_NOPE],) + _rope_cols(wq3[:, :, QK_NOPE:])],
                             axis=1).astype(BF16)
    cos_t, sin_t = _rope_tables(seq)
    wkv3 = mla_w_ukv[0].reshape(KV_LORA, MLA_HEADS, QK_NOPE + V_HEAD)
    wk = wkv3[:, :, :QK_NOPE].reshape(KV_LORA, MLA_HEADS * QK_NOPE).astype(BF16)
    wvt = wkv3[:, :, QK_NOPE:].reshape(KV_LORA, MLA_HEADS * V_HEAD).T.astype(BF16)
    proj_args = (x2, norm_mix_g[1][None], mods[1], win_p, mla_kv_norm_g[0][None], wk, wvt)
    kc, vtc = _mla_proj(*proj_args, nb=nb, ctx_len=ctx_len, seq=seq, tm=int(np.gcd(256, ctx_len)))
    k, vt, q = _mla_proj(*proj_args, nb=nb, ctx_len=ctx_len, seq=seq, tm=tile(512),
                         latent_args=(mla_q_norm_g[0][None], wq_all, cos_t, sin_t))
    o = _attention(q, k, kc, vt, vtc, tq=int(np.gcd(512, seq)))
    out = _ffn(x2, mods[1], norm_ffn_g[1][None], ffn_wi, ffn_wd, final_norm_g[None], layer=1,
               nb=nb, n_ctx_rows=n_ctx_rows, seq=seq, tm=tf, lat_only=True,
               attn=(o.reshape(nb * seq, MLA_HEADS * V_HEAD), mla_w_o[0].astype(BF16)), final_norm=True)
    return out.reshape(nb, seq, d)
```

```python
import functools

import numpy as np
import jax
import jax.numpy as jnp
from jax import lax
from jax.experimental import pallas as pl
from jax.experimental.pallas import tpu as pltpu

F32 = jnp.float32
BF16 = jnp.bfloat16

D_MODEL = 1024
GRID_W = 64
NORM_EPS = 1e-6
LRU_WIDTH = D_MODEL
LRU_BLOCK = 256
LRU_BLOCKS = LRU_WIDTH // LRU_BLOCK
CONV_W = 4
RG_C = 8.0
MLA_HEADS = 8
Q_LORA = 384
KV_LORA = 256
QK_NOPE = 128
QK_ROPE = 64
V_HEAD = 128
SM_SCALE = (QK_NOPE + QK_ROPE) ** -0.5
Q_PRESCALE = SM_SCALE * float(np.log2(np.e))
ROPE_THETA = 10000.0

MOD_ROWS = 8
SUBLANES = 8
MXU_DIM = 256
HEAD_PAD = 256
VMEM_LIMIT_MB = 56
FFN_LAG = 2
TINY = 1e-30


def _cparams(sem, vmem_mb=VMEM_LIMIT_MB):
    return pltpu.CompilerParams(dimension_semantics=sem, vmem_limit_bytes=vmem_mb << 20)


def _const_spec(shape):
    nd = len(shape)
    return pl.BlockSpec(shape, lambda *_: (0,) * nd)


def _norm_mod(x, g, shift, scale):
    ms = jnp.mean(x * x, axis=-1, keepdims=True)
    y = (x * lax.rsqrt(ms + NORM_EPS)) * g
    return y * (1.0 + scale) + shift


def _rmsnorm(x, g):
    ms = jnp.mean(x * x, axis=-1, keepdims=True)
    return (x * lax.rsqrt(ms + NORM_EPS)) * g


def _ada_kernel(cv_ref, w_ref, b_ref, o_ref):
    cv = cv_ref[...]
    s = cv * jax.nn.sigmoid(cv)
    o_ref[0] = jnp.dot(s.astype(BF16), w_ref[0].astype(BF16), preferred_element_type=F32) + b_ref[0]


def _ada_mods(cv, ada_w, ada_b):
    depth, d, n = ada_w.shape
    tn = 1536
    return pl.pallas_call(
        _ada_kernel,
        grid=(depth, n // tn),
        in_specs=[
            pl.BlockSpec((MOD_ROWS, d), lambda l, j: (0, 0)),
            pl.BlockSpec((1, d, tn), lambda l, j: (l, 0, j)),
            pl.BlockSpec((1, 1, tn), lambda l, j: (l, 0, j)),
        ],
        out_specs=pl.BlockSpec((1, MOD_ROWS, tn), lambda l, j: (l, 0, j)),
        out_shape=jax.ShapeDtypeStruct((depth, MOD_ROWS, n), F32),
        compiler_params=_cparams(("arbitrary", "arbitrary")),
        name="ada_mods",
    )(cv, ada_w, ada_b.reshape(depth, 1, n))


def _lru_in_kernel(ctx_ref, x_ref, g_ref, mod_ref, w_ref, gg_ref, u_ref, *, nct):
    xin = jnp.where(pl.program_id(0) < nct, ctx_ref[...], x_ref[...])
    h = _norm_mod(xin, g_ref[...], mod_ref[0, 0:1, :], mod_ref[0, 1:2, :])
    y = jnp.dot(h.astype(BF16), w_ref[...], preferred_element_type=F32)
    r = LRU_WIDTH
    gg_ref[...] = jax.nn.gelu(y[:, :r])
    u_ref[...] = y[:, r:]


def _lru_in(ctx2d, x2d, g, mods, w, *, nb, seq, tm):
    d = x2d.shape[1]
    n = ctx2d.shape[0] + x2d.shape[0]
    r = LRU_WIDTH
    nct = ctx2d.shape[0] // tm
    tpb = seq // tm

    def mod_map(i):
        return (jnp.where(i < nct, nb, (i - nct) // tpb), 0, 0)

    return pl.pallas_call(
        functools.partial(_lru_in_kernel, nct=nct),
        grid=(n // tm,),
        in_specs=[
            pl.BlockSpec((tm, d), lambda i: (jnp.minimum(i, nct - 1), 0)),
            pl.BlockSpec((tm, d), lambda i: (jnp.maximum(i - nct, 0), 0)),
            _const_spec((1, d)),
            pl.BlockSpec((1, 6, d), mod_map),
            _const_spec((d, 2 * r)),
        ],
        out_specs=[pl.BlockSpec((tm, r), lambda i: (i, 0)), pl.BlockSpec((tm, r), lambda i: (i, 0))],
        out_shape=[jax.ShapeDtypeStruct((n, r), F32), jax.ShapeDtypeStruct((n, r), F32)],
        compiler_params=_cparams(("arbitrary",)),
        name="lru_in",
    )(ctx2d, x2d, g, mods, w)


def _lru_coeffs(xc, gw, gbh, lam, a_s, b_s, between=None):
    xcb = xc.astype(BF16)
    neg = -lam[...]
    sp = jnp.maximum(neg, 0.0) + jnp.log1p(jnp.exp(-jnp.abs(neg)))
    c2 = (-0.5 * RG_C) * sp
    for n in range(LRU_BLOCKS):
        sl = slice(n * LRU_BLOCK, (n + 1) * LRU_BLOCK)
        pre = jnp.dot(xcb[:, sl], gw[n], preferred_element_type=F32)
        t_r = jnp.tanh(pre[:, :LRU_BLOCK] + gbh[0:1, sl])
        t_i = jnp.tanh(pre[:, LRU_BLOCK:] + gbh[1:2, sl])
        a = jnp.exp(c2[:, sl] * t_r + c2[:, sl])
        gap = 1.0 - a * a
        mult = gap * lax.rsqrt(jnp.maximum(gap, TINY))
        a_s[:, sl] = a
        b_s[:, sl] = (mult * (0.5 * t_i + 0.5)) * xc[:, sl]
        if between is not None:
            between()


def _lru_chunk_scan(a_s, b_s, h_s, hc, *, reverse, tc, between=None):
    r = LRU_WIDTH
    row = lax.broadcasted_iota(jnp.int32, (SUBLANES, r), 0)
    ngroups = tc // SUBLANES

    def group(gi, hprev):
        g0 = (ngroups - 1 - gi) if reverse else gi
        start = g0 * SUBLANES
        rows = pl.ds(start if isinstance(start, int) else pl.multiple_of(start, SUBLANES), SUBLANES)
        av = a_s[rows, :]
        bv = b_s[rows, :]
        for s in (1, 2, 4):
            if reverse:
                keep = row < (SUBLANES - s)
                shift = SUBLANES - s
            else:
                keep = row >= s
                shift = s
            a_sh = jnp.where(keep, pltpu.roll(av, shift, 0), 1.0)
            b_sh = jnp.where(keep, pltpu.roll(bv, shift, 0), 0.0)
            bv = av * b_sh + bv
            av = av * a_sh
        hrows = av * hprev + bv
        h_s[rows, :] = hrows
        edge = hrows[0:1, :] if reverse else hrows[SUBLANES - 1:SUBLANES, :]
        return jnp.broadcast_to(edge, (SUBLANES, r))

    if between is None:
        hc[...] = lax.fori_loop(0, ngroups, group, hc[...], unroll=2)
    else:
        h = hc[...]
        for gi in range(ngroups):
            h = group(gi, h)
            between(gi)
        hc[...] = h


def _scan_fwd_kernel(ucur, uprev, unext, cw, cb, gw, gbh, lam, hf, xc_out, ext, a_s, b_s, hc, *, tc, nc, nl):
    j = pl.program_id(1)
    in_ctx = j < nc
    pos = jnp.where(in_ctx, j, j - nc)
    first = pos == 0
    last = pos == jnp.where(in_ctx, nc, nl) - 1

    @pl.when(j == 0)
    def _():
        hc[...] = jnp.zeros_like(hc)

    n_ext = tc + 2 * SUBLANES
    ext[0:SUBLANES, :] = jnp.where(first, 0.0, uprev[...])
    ext[SUBLANES:SUBLANES + tc, :] = ucur[...]
    ext[SUBLANES + tc:n_ext, :] = jnp.where(last, 0.0, unext[...])
    e = ext[...]
    mid = slice(SUBLANES, SUBLANES + tc)
    xc = cb[...] + cw[0:1, :] * pltpu.roll(e, 1, 0)[mid] + cw[1:2, :] * e[mid] \
        + cw[2:3, :] * pltpu.roll(e, n_ext - 1, 0)[mid] + cw[3:4, :] * pltpu.roll(e, n_ext - 2, 0)[mid]
    xc_out[...] = xc
    _lru_coeffs(xc, gw, gbh, lam, a_s, b_s)
    _lru_chunk_scan(a_s, b_s, hf, hc, reverse=False, tc=tc)


def _scan_bwd_ffn_kernel(xc, gw, gbh, lam, hf, gg, ctx_res, x_res, mod, wout, mod_prev, fg, wi, wd,
                         out, a_s, b_s, hc, hb_s, x1_prev, *, tc, nc, per_batch, nsteps):
    s = pl.program_id(0)
    j = jnp.minimum(s, nsteps - 1) % per_batch

    @pl.when(s == 0)
    def _():
        x1_prev[...] = jnp.zeros_like(x1_prev)

    @pl.when(j == 0)
    def _():
        hc[...] = jnp.zeros_like(hc)

    pieces, ffn_result = _ffn_pieces(x1_prev[...], mod_prev, fg, wi, wd, n_chunks=wd.shape[1] // MXU_DIM)
    todo = list(pieces)
    run_next = lambda: todo.pop(0)() if todo else None
    run_two = lambda: (run_next(), run_next())
    run_two()
    _lru_coeffs(xc[...], gw, gbh, lam, a_s, b_s, between=run_two)
    ngroups = tc // SUBLANES
    n_scan = len(todo)

    def between(gi):
        while todo and (n_scan - len(todo)) * ngroups < (gi + 1) * n_scan:
            run_next()

    _lru_chunk_scan(a_s, b_s, hb_s, hc, reverse=True, tc=tc, between=between)
    out[...] = ffn_result()
    z = ((hf[...] + hb_s[...]) * gg[...]).astype(BF16)
    res = jnp.where(j < nc, ctx_res[...], x_res[...])
    x1_prev[...] = res + mod[0, 2:3, :] * jnp.dot(z, wout[...], preferred_element_type=F32)


def _lru_scans_ffn(u, conv_w, conv_b, gw, gbh, lam, gg, ctx2d, x2d, mods, wout, ffn_g, ffn_wi, ffn_wd, *,
                   layer, nb, ctx_len, seq, tc):
    n, r = u.shape
    d = x2d.shape[1]
    nc = ctx_len // tc
    nl = seq // tc
    blk8 = tc // SUBLANES

    def cur_blk(b, j, reverse):
        jc = (nc - 1 - j) if reverse else j
        jl = (nl - 1 - (j - nc)) if reverse else (j - nc)
        return jnp.where(j < nc, b * nc + jc, nb * nc + b * nl + jl)

    fwd_map = lambda b, j: (cur_blk(b, j, False), 0)
    bwd_map = lambda b, j: (cur_blk(b, j, True), 0)
    prev_map = lambda b, j: (jnp.maximum(cur_blk(b, j, False) * blk8 - 1, 0), 0)
    next_map = lambda b, j: (jnp.minimum((cur_blk(b, j, False) + 1) * blk8, n // SUBLANES - 1), 0)
    gate_specs = [
        _const_spec((LRU_BLOCKS, LRU_BLOCK, 2 * LRU_BLOCK)),
        _const_spec((2, r)),
        _const_spec((1, r)),
    ]
    chunk = pltpu.VMEM((tc, r), F32)
    carry = pltpu.VMEM((SUBLANES, r), F32)
    hf, xc = pl.pallas_call(
        functools.partial(_scan_fwd_kernel, tc=tc, nc=nc, nl=nl),
        grid=(nb, nc + nl),
        in_specs=[
            pl.BlockSpec((tc, r), fwd_map),
            pl.BlockSpec((SUBLANES, r), prev_map),
            pl.BlockSpec((SUBLANES, r), next_map),
            _const_spec((CONV_W, r)),
            _const_spec((1, r)),
        ] + gate_specs,
        out_specs=[pl.BlockSpec((tc, r), fwd_map), pl.BlockSpec((tc, r), fwd_map)],
        out_shape=[jax.ShapeDtypeStruct((n, r), F32), jax.ShapeDtypeStruct((n, r), F32)],
        scratch_shapes=[pltpu.VMEM((tc + 2 * SUBLANES, r), F32), chunk, chunk, carry],
        compiler_params=_cparams(("arbitrary", "arbitrary")),
        name="lru_scan_fwd",
    )(u, u, u, conv_w, conv_b, gw[0], gbh[0], lam[0])
    per_batch = nc + nl
    nsteps = nb * per_batch
    hid = ffn_wd.shape[1]

    def at(step_map, lag):
        def index_map(s):
            sc = jnp.clip(s - lag, 0, nsteps - 1)
            return step_map(sc // per_batch, sc % per_batch)
        return index_map

    ctx_res_map = lambda b, j: (b * nc + jnp.maximum(nc - 1 - j, 0), 0)
    x_res_map = lambda b, j: (b * nl + nl - 1 - jnp.maximum(j - nc, 0), 0)
    mod_map = lambda b, j: (jnp.where(j < nc, nb, b), 0, 0)
    return pl.pallas_call(
        functools.partial(_scan_bwd_ffn_kernel, tc=tc, nc=nc, per_batch=per_batch, nsteps=nsteps),
        grid=(nsteps + 1,),
        in_specs=[pl.BlockSpec((tc, r), at(bwd_map, 0))] + gate_specs + [
            pl.BlockSpec((tc, r), at(bwd_map, 0)),
            pl.BlockSpec((tc, r), at(bwd_map, 0)),
            pl.BlockSpec((tc, d), at(ctx_res_map, 0)),
            pl.BlockSpec((tc, d), at(x_res_map, 0)),
            pl.BlockSpec((1, 6, d), at(mod_map, 0)),
            _const_spec((r, d)),
            pl.BlockSpec((1, 6, d), at(mod_map, 1)),
            _const_spec((1, d)),
            pl.BlockSpec((1, d, 2 * hid), lambda s: (layer, 0, 0), pipeline_mode=pl.Buffered(1)),
            pl.BlockSpec((1, hid, d), lambda s: (layer, 0, 0), pipeline_mode=pl.Buffered(1)),
        ],
        out_specs=pl.BlockSpec((tc, d), at(bwd_map, 1)),
        out_shape=jax.ShapeDtypeStruct((n, d), F32),
        scratch_shapes=[chunk, chunk, carry, chunk, pltpu.VMEM((tc, d), F32)],
        compiler_params=_cparams(("arbitrary",)),
        name="lru_scan_bwd_ffn",
    )(xc, gw[1], gbh[1], lam[1], hf, gg, ctx2d, x2d, mods, wout, mods, ffn_g, ffn_wi, ffn_wd)


def _ffn_pieces(x, mod_ref, g_ref, wi_ref, wd_ref, *, n_chunks):
    h = _norm_mod(x, g_ref[...], mod_ref[0, 3:4, :], mod_ref[0, 4:5, :]).astype(BF16)
    hid = wd_ref.shape[1]
    n_mxu = hid // MXU_DIM
    edges = [round(c * n_mxu / n_chunks) * MXU_DIM for c in range(n_chunks)] + [hid]
    acc = [jnp.zeros(x.shape, F32)]
    pending = []

    def down():
        act, e0, e1 = pending.pop(0)
        acc[0] = acc[0] + jnp.dot(act, wd_ref[0, e0:e1, :], preferred_element_type=F32)

    def piece(e0, e1):
        def run():
            gate = jnp.dot(h, wi_ref[0, :, e0:e1], preferred_element_type=F32)
            up = jnp.dot(h, wi_ref[0, :, hid + e0:hid + e1], preferred_element_type=F32)
            if len(pending) >= FFN_LAG:
                down()
            pending.append((((gate * jax.nn.sigmoid(gate)) * up).astype(BF16), e0, e1))
        return run

    def result():
        while pending:
            down()
        return x + mod_ref[0, 5:6, :] * acc[0]

    return [piece(e0, e1) for e0, e1 in zip(edges[:-1], edges[1:])], result


def _ffn_kernel(*refs, with_attn, final_norm, n_chunks):
    if with_attn:
        x_ref, o_ref, wo_ref, mod_ref, g_ref, wi_ref, wd_ref, fg_ref, out_ref = refs
    else:
        x_ref, mod_ref, g_ref, wi_ref, wd_ref, fg_ref, out_ref = refs
    x = x_ref[...]
    if with_attn:
        x = x + mod_ref[0, 2:3, :] * jnp.dot(o_ref[...], wo_ref[...], preferred_element_type=F32)
    pieces, result = _ffn_pieces(x, mod_ref, g_ref, wi_ref, wd_ref, n_chunks=n_chunks)
    for run in pieces:
        run()
    x = result()
    if final_norm:
        x = _rmsnorm(x, fg_ref[...])
    out_ref[...] = x


def _ffn(xall, mods, g, wi, wd, fg, *, layer, nb, n_ctx_rows, seq, tm, lat_only, attn=None, final_norm=False):
    n, d = xall.shape
    hid = wd.shape[1]
    nct = n_ctx_rows // tm
    tpb = seq // tm
    off = nct if lat_only else 0
    n_out = n - n_ctx_rows if lat_only else n

    def mod_map(i):
        ii = i + off
        return (jnp.where(ii < nct, nb, (ii - nct) // tpb), 0, 0)

    in_specs = [pl.BlockSpec((tm, d), lambda i: (i + off, 0))]
    args = [xall]
    if attn is not None:
        o, wo = attn
        in_specs += [pl.BlockSpec((tm, o.shape[1]), lambda i: (i, 0)), _const_spec(wo.shape)]
        args += [o, wo]
    in_specs += [
        pl.BlockSpec((1, 6, d), mod_map),
        _const_spec((1, d)),
        pl.BlockSpec((1, d, 2 * hid), lambda i: (layer, 0, 0), pipeline_mode=pl.Buffered(1)),
        pl.BlockSpec((1, hid, d), lambda i: (layer, 0, 0), pipeline_mode=pl.Buffered(1)),
        _const_spec((1, d)),
    ]
    args += [mods, g, wi, wd, fg]
    return pl.pallas_call(
        functools.partial(_ffn_kernel, with_attn=attn is not None, final_norm=final_norm, n_chunks=2),
        grid=(n_out // tm,),
        in_specs=in_specs,
        out_specs=pl.BlockSpec((tm, d), lambda i: (i, 0)),
        out_shape=jax.ShapeDtypeStruct((n_out, d), F32),
        compiler_params=_cparams(("arbitrary",)),
        name="ffn_attn" if attn is not None else "ffn",
    )(*args)


def _mla_proj_kernel(*refs, latent):
    if latent:
        (x_ref, g_ref, mod_ref, win_ref, kvg_ref, wk_ref, wvt_ref, qg_ref, wq_ref, cos_ref, sin_ref,
         k_ref, v_ref, q_ref) = refs
        cosv, sinv = cos_ref[...], sin_ref[...]
    else:
        x_ref, g_ref, mod_ref, win_ref, kvg_ref, wk_ref, wvt_ref, k_ref, v_ref = refs
        cosv, sinv = 1.0, 0.0
    h = _norm_mod(x_ref[...], g_ref[...], mod_ref[0, 0:1, :], mod_ref[0, 1:2, :]).astype(BF16)
    c_all = jnp.dot(h, win_ref[...], preferred_element_type=F32)
    kv_lo = Q_LORA + KV_LORA
    lane = lax.broadcasted_iota(jnp.int32, (1, 128), 1)
    low = lane < QK_ROPE
    kt = c_all[:, kv_lo:kv_lo + 128] * jnp.where(low, cosv, sinv)
    kt = kt + pltpu.roll(kt, QK_ROPE, 1)
    k_rope = (jnp.where(low, kt, 0.0).astype(BF16), jnp.where(low, 0.0, kt).astype(BF16))
    ckv = _rmsnorm(c_all[:, Q_LORA:kv_lo], kvg_ref[...])
    k_nope = jnp.dot(ckv.astype(BF16), wk_ref[...], preferred_element_type=F32)
    v_t = jnp.dot(wvt_ref[...], ckv.T.astype(BF16), preferred_element_type=F32)
    for hd in range(MLA_HEADS):
        sl = slice(hd * 128, (hd + 1) * 128)
        k_ref[0, hd, :, 0:QK_NOPE] = k_nope[:, sl].astype(BF16)
        k_ref[0, hd, :, QK_NOPE:HEAD_PAD] = k_rope[hd % 2]
        v_ref[0, hd] = v_t[sl, :].astype(BF16)
    if latent:
        cq = _rmsnorm(c_all[:, :Q_LORA], qg_ref[...]).astype(BF16)
        q_all = jnp.dot(cq, wq_ref[...], preferred_element_type=F32) * Q_PRESCALE
        hw = MLA_HEADS * QK_NOPE
        hr = MLA_HEADS * QK_ROPE
        for pair in range(MLA_HEADS // 2):
            ps = slice(hw + pair * 128, hw + (pair + 1) * 128)
            ss = slice(hw + hr + pair * 128, hw + hr + (pair + 1) * 128)
            q_rope = (q_all[:, ps] * cosv + q_all[:, ss] * sinv).astype(BF16)
            for hd in (2 * pair, 2 * pair + 1):
                q_ref[0, hd, :, 0:QK_NOPE] = q_all[:, hd * 128:(hd + 1) * 128].astype(BF16)
                q_ref[0, hd, :, QK_NOPE:HEAD_PAD] = q_rope


def _mla_proj(xall, g, mods, win, kvg, wk, wvt, *, nb, ctx_len, seq, tm, latent_args=None):
    n, d = xall.shape
    latent = latent_args is not None
    rows = seq if latent else ctx_len
    off = nb * ctx_len // tm if latent else 0
    tpb = rows // tm
    kv_map = lambda i: (i // tpb, 0, i % tpb, 0)
    vt_map = lambda i: (i // tpb, 0, 0, i % tpb)
    in_specs = [
        pl.BlockSpec((tm, d), lambda i: (i + off, 0)),
        _const_spec((1, d)),
        pl.BlockSpec((1, 6, d), lambda i: ((i // tpb) if latent else nb, 0, 0)),
        _const_spec(win.shape),
        _const_spec((1, KV_LORA)),
        _const_spec(wk.shape),
        _const_spec(wvt.shape),
    ]
    args = [xall, g, mods, win, kvg, wk, wvt]
    out_specs = [pl.BlockSpec((1, MLA_HEADS, tm, HEAD_PAD), kv_map), pl.BlockSpec((1, MLA_HEADS, V_HEAD, tm), vt_map)]
    out_shape = [jax.ShapeDtypeStruct((nb, MLA_HEADS, rows, HEAD_PAD), BF16),
                 jax.ShapeDtypeStruct((nb, MLA_HEADS, V_HEAD, rows), BF16)]
    if latent:
        qg, wq, cos_t, sin_t = latent_args
        tab_map = lambda i: (i % tpb, 0)
        in_specs += [_const_spec((1, Q_LORA)), _const_spec(wq.shape),
                     pl.BlockSpec((tm, 128), tab_map), pl.BlockSpec((tm, 128), tab_map)]
        args += [qg, wq, cos_t, sin_t]
        out_specs.append(pl.BlockSpec((1, MLA_HEADS, tm, HEAD_PAD), kv_map))
        out_shape.append(jax.ShapeDtypeStruct((nb, MLA_HEADS, rows, HEAD_PAD), BF16))
    return pl.pallas_call(
        functools.partial(_mla_proj_kernel, latent=latent),
        grid=(nb * tpb,),
        in_specs=in_specs,
        out_specs=out_specs,
        out_shape=out_shape,
        compiler_params=_cparams(("arbitrary",)),
        name="mla_proj_lat" if latent else "mla_proj_ctx",
    )(*args)


def _attn_kernel(q_ref, k_ref, kc_ref, vt_ref, vtc_ref, o_ref, s0_ref, s1_ref, qt_ref, acc_ref, l_ref, *, tq, kc):
    n_lat = k_ref.shape[2]
    t_all = n_lat + kc_ref.shape[2]
    nkc = t_all // kc
    nq = q_ref.shape[2] // tq
    s_refs = (s0_ref, s1_ref)
    dims = (((1,), (1,)), ((), ()))

    def group_reduce(x, op):
        return op(x.reshape(x.shape[0] // SUBLANES, SUBLANES, tq), axis=0)

    def k_chunk(c):
        if c * kc < n_lat:
            return k_ref[0, 0, c * kc:(c + 1) * kc, :]
        return kc_ref[0, 0, c * kc - n_lat:(c + 1) * kc - n_lat, :]

    def vt_chunk(c):
        if c * kc < n_lat:
            return vt_ref[0, 0, :, c * kc:(c + 1) * kc]
        return vtc_ref[0, 0, :, c * kc - n_lat:(c + 1) * kc - n_lat]

    def scores(qt, c, s_ref):
        st = jnp.dot(k_chunk(c), qt, preferred_element_type=F32)
        s_ref[c * kc:(c + 1) * kc, :] = st
        return group_reduce(st, jnp.max)

    def q_tile_t(i):
        q = q_ref[0, 0, pl.ds(pl.multiple_of(i * tq, tq), tq), :]
        return q.astype(F32).T.astype(BF16)

    def finalize(i):
        l = jnp.sum(l_ref[...], axis=0, keepdims=True)
        rows = pl.ds(pl.multiple_of(i * tq, tq), tq)
        o_ref[0, rows, :] = (acc_ref[...] / l).T.astype(o_ref.dtype)

    def tile_step(i, slot, m8, with_next):
        s_cur, s_nxt = s_refs[slot], s_refs[1 - slot]
        m_next = jnp.full((SUBLANES, tq), -jnp.inf, F32)
        if with_next:
            qt_next = qt_ref[1 - slot]
        m = jnp.max(m8, axis=0, keepdims=True)
        l8 = jnp.zeros((SUBLANES, tq), F32)
        acc = jnp.zeros((V_HEAD, tq), F32)
        p_prev = None
        for c in range(nkc):
            if with_next:
                m_next = jnp.maximum(m_next, scores(qt_next, c, s_nxt))
            if p_prev is not None:
                acc = acc + jnp.dot(vt_chunk(c - 1), p_prev, preferred_element_type=F32)
            p = jnp.exp2(s_cur[c * kc:(c + 1) * kc, :] - m)
            l8 = l8 + group_reduce(p, jnp.sum)
            p_prev = p.astype(BF16)
            if c == 1:
                finalize(jnp.maximum(i - 1, 0))
            if c == nkc // 2 and with_next:
                qt_ref[slot] = q_tile_t(jnp.minimum(i + 2, nq - 1))
        acc_ref[...] = acc + jnp.dot(vt_chunk(nkc - 1), p_prev, preferred_element_type=F32)
        l_ref[...] = l8
        return m_next

    acc_ref[...] = jnp.zeros_like(acc_ref)
    l_ref[...] = jnp.ones_like(l_ref)
    qt0 = q_tile_t(0)
    qt_ref[1] = q_tile_t(min(1, nq - 1))
    m8 = scores(qt0, 0, s0_ref)
    for c in range(1, nkc):
        m8 = jnp.maximum(m8, scores(qt0, c, s0_ref))

    def step(i, m8):
        return lax.cond(i % 2 == 0,
                        lambda m: tile_step(i, 0, m, True),
                        lambda m: tile_step(i, 1, m, True), m8)

    m8 = lax.fori_loop(0, nq - 1, step, m8)
    tile_step(nq - 1, (nq - 1) % 2, m8, False)
    finalize(nq - 1)


def _attention(q, k, kc, vt, vtc, *, tq):
    nb, nh, seq, dp = q.shape
    ctx_len = kc.shape[2]
    t_all = seq + ctx_len
    chunk = int(np.gcd(np.gcd(MXU_DIM, seq), ctx_len))
    head = lambda b, h: (b, h, 0, 0)
    return pl.pallas_call(
        functools.partial(_attn_kernel, tq=tq, kc=chunk),
        grid=(nb, nh),
        scratch_shapes=[pltpu.VMEM((t_all, tq), F32), pltpu.VMEM((t_all, tq), F32),
                        pltpu.VMEM((2, dp, tq), BF16), pltpu.VMEM((V_HEAD, tq), F32), pltpu.VMEM((SUBLANES, tq), F32)],
        in_specs=[
            pl.BlockSpec((1, 1, seq, dp), head),
            pl.BlockSpec((1, 1, seq, dp), head),
            pl.BlockSpec((1, 1, ctx_len, dp), head),
            pl.BlockSpec((1, 1, V_HEAD, seq), head),
            pl.BlockSpec((1, 1, V_HEAD, ctx_len), head),
        ],
        out_specs=pl.BlockSpec((1, seq, V_HEAD), lambda b, h: (b, 0, h)),
        out_shape=jax.ShapeDtypeStruct((nb, seq, nh * V_HEAD), BF16),
        compiler_params=_cparams(("arbitrary", "arbitrary")),
        name="mla_attention",
    )(q, k, kc, vt, vtc)


def _rope_cols(w):
    qd = QK_ROPE // 4
    r0, r1, r2, r3 = (w[..., i * qd:(i + 1) * qd] for i in range(4))
    return jnp.concatenate([r0, r2, r1, r3], axis=-1), jnp.concatenate([-r1, -r3, r0, r2], axis=-1)


def _rope_tables(seq):
    rows = seq // GRID_W
    row_ids = np.repeat(np.arange(rows, dtype=np.float32), GRID_W)
    col_ids = np.tile(np.arange(GRID_W, dtype=np.float32), rows)
    axis_dim = QK_ROPE // 2
    expo = (np.arange(0, axis_dim, 2, dtype=np.float32) / np.float32(axis_dim)).astype(np.float32)
    inv_freq = (np.float32(1.0) / np.power(np.float32(ROPE_THETA), expo)).astype(np.float32)
    ang_r = (row_ids[:, None] * inv_freq).astype(np.float32)
    ang_c = (col_ids[:, None] * inv_freq).astype(np.float32)
    reps = 128 // (QK_ROPE // 2)
    cos_t = np.tile(np.concatenate([np.cos(ang_r), np.cos(ang_c)], axis=1), (1, reps))
    sin_t = np.tile(np.concatenate([np.sin(ang_r), np.sin(ang_c)], axis=1), (1, reps))
    return jnp.asarray(cos_t, F32), jnp.asarray(sin_t, F32)


def kernel(x, c, ctx, c_ctx, ada_w, ada_b, norm_mix_g, norm_ffn_g, ffn_w_in, ffn_w_out, lru_w_in, lru_conv_w,
           lru_conv_b, lru_gate_w, lru_gate_b, lru_lambda, lru_w_out, mla_w_in, mla_q_norm_g, mla_kv_norm_g,
           mla_w_uq, mla_w_ukv, mla_w_o, final_norm_g):
    nb, seq, d = x.shape
    ctx_len = ctx.shape[1]
    depth = ada_w.shape[0]
    assert depth == 2 and d == D_MODEL and nb + 1 <= MOD_ROWS
    n_ctx_rows = nb * ctx_len
    tile = lambda cap: int(np.gcd(np.gcd(cap, n_ctx_rows), seq))

    ctx2d = ctx.reshape(n_ctx_rows, d)
    x2d = x.reshape(nb * seq, d)
    ffn_wi = ffn_w_in.astype(BF16)
    ffn_wd = ffn_w_out.astype(BF16)

    cv = jnp.concatenate([c, c_ctx[None, :], jnp.zeros((MOD_ROWS - nb - 1, d), F32)], axis=0)
    mods = _ada_mods(cv, ada_w, ada_b).reshape(depth, MOD_ROWS, 6, d)

    tm = tile(1024)
    gg, u = _lru_in(ctx2d, x2d, norm_mix_g[0][None], mods[0], lru_w_in[0].astype(BF16), nb=nb, seq=seq, tm=tm)
    tc = int(np.gcd(np.gcd(256, ctx_len), seq))
    gw = lru_gate_w[0]
    gw = (0.5 * jnp.concatenate([gw[:, 0], gw[:, 1]], axis=-1)).astype(BF16)
    x2 = _lru_scans_ffn(u, lru_conv_w[0], lru_conv_b[0][None], gw, 0.5 * lru_gate_b[0], lru_lambda[0][:, None, :],
                        gg, ctx2d, x2d, mods[0], lru_w_out[0].astype(BF16), norm_ffn_g[0][None], ffn_wi, ffn_wd,
                        layer=0, nb=nb, ctx_len=ctx_len, seq=seq, tc=tc)
    tf = tile(512)

    w_in = mla_w_in[0]
    win_p = jnp.concatenate((w_in[:, :Q_LORA + KV_LORA],) + _rope_cols(w_in[:, Q_LORA + KV_LORA:]),
                            axis=1).astype(BF16)
    wq3 = mla_w_uq[0].reshape(Q_LORA, MLA_HEADS, QK_NOPE + QK_ROPE)
    wq_all = jnp.concatenate([part.reshape(Q_LORA, -1)
                              for part in (wq3[:, :, :QK_NOPE],) + _rope_cols(wq3[:, :, QK_NOPE:])],
                             axis=1).astype(BF16)
    cos_t, sin_t = _rope_tables(seq)
    wkv3 = mla_w_ukv[0].reshape(KV_LORA, MLA_HEADS, QK_NOPE + V_HEAD)
    wk = wkv3[:, :, :QK_NOPE].reshape(KV_LORA, MLA_HEADS * QK_NOPE).astype(BF16)
    wvt = wkv3[:, :, QK_NOPE:].reshape(KV_LORA, MLA_HEADS * V_HEAD).T.astype(BF16)
    proj_args = (x2, norm_mix_g[1][None], mods[1], win_p, mla_kv_norm_g[0][None], wk, wvt)
    kc, vtc = _mla_proj(*proj_args, nb=nb, ctx_len=ctx_len, seq=seq, tm=int(np.gcd(256, ctx_len)))
    k, vt, q = _mla_proj(*proj_args, nb=nb, ctx_len=ctx_len, seq=seq, tm=tile(512),
                         latent_args=(mla_q_norm_g[0][None], wq_all, cos_t, sin_t))
    o = _attention(q, k, kc, vt, vtc, tq=int(np.gcd(512, seq)))
    out = _ffn(x2, mods[1], norm_ffn_g[1][None], ffn_wi, ffn_wd, final_norm_g[None], layer=1,
               nb=nb, n_ctx_rows=n_ctx_rows, seq=seq, tm=tf, lat_only=True,
               attn=(o.reshape(nb * seq, MLA_HEADS * V_HEAD), mla_w_o[0].astype(BF16)), final_norm=True)
    return out.reshape(nb, seq, d)
```

```python
import functools

import numpy as np
import jax
import jax.numpy as jnp
from jax import lax
from jax.experimental import pallas as pl
from jax.experimental.pallas import tpu as pltpu

F32 = jnp.float32
BF16 = jnp.bfloat16

D_MODEL = 1024
GRID_W = 64
NORM_EPS = 1e-6
LRU_WIDTH = D_MODEL
LRU_BLOCK = 256
LRU_BLOCKS = LRU_WIDTH // LRU_BLOCK
CONV_W = 4
RG_C = 8.0
MLA_HEADS = 8
Q_LORA = 384
KV_LORA = 256
QK_NOPE = 128
QK_ROPE = 64
V_HEAD = 128
SM_SCALE = (QK_NOPE + QK_ROPE) ** -0.5
LOG2_E = float(np.log2(np.e))
Q_PRESCALE = SM_SCALE * LOG2_E
ROPE_THETA = 10000.0

MOD_ROWS = 8
SUBLANES = 8
BF16_ROWS = 16
MXU_DIM = 256
HEAD_PAD = 256
VMEM_LIMIT_MB = 56
FFN_LAG = 2
TINY = 1e-30


def _cparams(sem, vmem_mb=VMEM_LIMIT_MB):
    return pltpu.CompilerParams(dimension_semantics=sem, vmem_limit_bytes=vmem_mb << 20)


def _const_spec(shape):
    nd = len(shape)
    return pl.BlockSpec(shape, lambda *_: (0,) * nd)


def _norm_mod(x, g, shift, scale):
    ms = jnp.mean(x * x, axis=-1, keepdims=True)
    y = (x * lax.rsqrt(ms + NORM_EPS)) * g
    return y * (1.0 + scale) + shift


def _rmsnorm(x, g):
    ms = jnp.mean(x * x, axis=-1, keepdims=True)
    return (x * lax.rsqrt(ms + NORM_EPS)) * g


def _ada_kernel(cv_ref, w_ref, b_ref, o_ref):
    cv = cv_ref[...]
    s = cv * jax.nn.sigmoid(cv)
    o_ref[0] = jnp.dot(s.astype(BF16), w_ref[0].astype(BF16), preferred_element_type=F32) + b_ref[0]


def _ada_mods(cv, ada_w, ada_b):
    depth, d, n = ada_w.shape
    tn = 1536
    return pl.pallas_call(
        _ada_kernel,
        grid=(depth, n // tn),
        in_specs=[
            pl.BlockSpec((MOD_ROWS, d), lambda l, j: (0, 0)),
            pl.BlockSpec((1, d, tn), lambda l, j: (l, 0, j)),
            pl.BlockSpec((1, 1, tn), lambda l, j: (l, 0, j)),
        ],
        out_specs=pl.BlockSpec((1, MOD_ROWS, tn), lambda l, j: (l, 0, j)),
        out_shape=jax.ShapeDtypeStruct((depth, MOD_ROWS, n), F32),
        compiler_params=_cparams(("arbitrary", "arbitrary")),
        name="ada_mods",
    )(cv, ada_w, ada_b.reshape(depth, 1, n))


def _lru_in_kernel(ctx_ref, x_ref, g_ref, mod_ref, w_ref, gg_ref, u_ref, *, nct):
    xin = jnp.where(pl.program_id(0) < nct, ctx_ref[...], x_ref[...])
    h = _norm_mod(xin, g_ref[...], mod_ref[0, 0:1, :], mod_ref[0, 1:2, :])
    y = jnp.dot(h.astype(BF16), w_ref[...], preferred_element_type=F32)
    r = LRU_WIDTH
    gg_ref[...] = jax.nn.gelu(y[:, :r])
    u_ref[...] = y[:, r:]


def _lru_in(ctx2d, x2d, g, mods, w, *, nb, seq, tm):
    d = x2d.shape[1]
    n = ctx2d.shape[0] + x2d.shape[0]
    r = LRU_WIDTH
    nct = ctx2d.shape[0] // tm
    tpb = seq // tm

    def mod_map(i):
        return (jnp.where(i < nct, nb, (i - nct) // tpb), 0, 0)

    return pl.pallas_call(
        functools.partial(_lru_in_kernel, nct=nct),
        grid=(n // tm,),
        in_specs=[
            pl.BlockSpec((tm, d), lambda i: (jnp.minimum(i, nct - 1), 0)),
            pl.BlockSpec((tm, d), lambda i: (jnp.maximum(i - nct, 0), 0)),
            _const_spec((1, d)),
            pl.BlockSpec((1, 6, d), mod_map),
            _const_spec((d, 2 * r)),
        ],
        out_specs=[pl.BlockSpec((tm, r), lambda i: (i, 0)), pl.BlockSpec((tm, r), lambda i: (i, 0))],
        out_shape=[jax.ShapeDtypeStruct((n, r), F32), jax.ShapeDtypeStruct((n, r), F32)],
        compiler_params=_cparams(("arbitrary",)),
        name="lru_in",
    )(ctx2d, x2d, g, mods, w)


def _lru_coeffs(xc, gw, gbh, lam, a_s, b_s, between=None):
    xcb = xc.astype(BF16)
    neg = -lam[...]
    sp = jnp.maximum(neg, 0.0) + jnp.log1p(jnp.exp(-jnp.abs(neg)))
    c2 = (-0.5 * RG_C * LOG2_E) * sp
    for n in range(LRU_BLOCKS):
        sl = slice(n * LRU_BLOCK, (n + 1) * LRU_BLOCK)
        pre = jnp.dot(xcb[:, sl], gw[n], preferred_element_type=F32)
        t_r = jnp.tanh(pre[:, :LRU_BLOCK] + gbh[0:1, sl])
        t_i = jnp.tanh(pre[:, LRU_BLOCK:] + gbh[1:2, sl])
        a = jnp.exp2(c2[:, sl] * t_r + c2[:, sl])
        gap = 1.0 - a * a
        mult = gap * lax.rsqrt(jnp.maximum(gap, TINY))
        a_s[:, sl] = a
        b_s[:, sl] = (mult * (0.5 * t_i + 0.5)) * xc[:, sl]
        if between is not None:
            between()


def _lru_chunk_scan(a_s, b_s, h_s, hc, *, reverse, tc, between=None):
    r = LRU_WIDTH
    row = lax.broadcasted_iota(jnp.int32, (SUBLANES, r), 0)
    ngroups = tc // SUBLANES

    def group(gi, hprev):
        g0 = (ngroups - 1 - gi) if reverse else gi
        start = g0 * SUBLANES
        rows = pl.ds(start if isinstance(start, int) else pl.multiple_of(start, SUBLANES), SUBLANES)
        av = a_s[rows, :]
        bv = b_s[rows, :]
        for s in (1, 2, 4):
            if reverse:
                keep = row < (SUBLANES - s)
                shift = SUBLANES - s
            else:
                keep = row >= s
                shift = s
            a_sh = jnp.where(keep, pltpu.roll(av, shift, 0), 1.0)
            b_sh = jnp.where(keep, pltpu.roll(bv, shift, 0), 0.0)
            bv = av * b_sh + bv
            av = av * a_sh
        hrows = av * hprev + bv
        h_s[rows, :] = hrows
        edge = hrows[0:1, :] if reverse else hrows[SUBLANES - 1:SUBLANES, :]
        return jnp.broadcast_to(edge, (SUBLANES, r))

    if between is None:
        hc[...] = lax.fori_loop(0, ngroups, group, hc[...], unroll=2)
    else:
        h = hc[...]
        for gi in range(ngroups):
            h = group(gi, h)
            between(gi)
        hc[...] = h


def _scan_fwd_kernel(ucur, uprev, unext, cw, cb, gw, gbh, lam, wi32, wd32, hf, xc_out, wi16, wd16,
                     ext, a_s, b_s, hc, *, tc, nc, nl):
    wi16[...] = wi32[...].astype(BF16)
    wd16[...] = wd32[...].astype(BF16)
    j = pl.program_id(1)
    in_ctx = j < nc
    pos = jnp.where(in_ctx, j, j - nc)
    first = pos == 0
    last = pos == jnp.where(in_ctx, nc, nl) - 1

    @pl.when(j == 0)
    def _():
        hc[...] = jnp.zeros_like(hc)

    n_ext = tc + 2 * SUBLANES
    ext[0:SUBLANES, :] = jnp.where(first, 0.0, uprev[...])
    ext[SUBLANES:SUBLANES + tc, :] = ucur[...]
    ext[SUBLANES + tc:n_ext, :] = jnp.where(last, 0.0, unext[...])
    e = ext[...]
    mid = slice(SUBLANES, SUBLANES + tc)
    xc = cb[...] + cw[0:1, :] * pltpu.roll(e, 1, 0)[mid] + cw[1:2, :] * e[mid] \
        + cw[2:3, :] * pltpu.roll(e, n_ext - 1, 0)[mid] + cw[3:4, :] * pltpu.roll(e, n_ext - 2, 0)[mid]
    xc_out[...] = xc
    _lru_coeffs(xc, gw, gbh, lam, a_s, b_s)
    _lru_chunk_scan(a_s, b_s, hf, hc, reverse=False, tc=tc)


def _scan_bwd_ffn_kernel(xc, gw, gbh, lam, hf, gg, ctx_res, x_res, mod, wout, mod_prev, fg, wi, wd,
                         out, a_s, b_s, hc, hb_s, x1_prev, *, tc, nc, per_batch, nsteps):
    s = pl.program_id(0)
    j = jnp.minimum(s, nsteps - 1) % per_batch

    @pl.when(s == 0)
    def _():
        x1_prev[...] = jnp.zeros_like(x1_prev)

    @pl.when(j == 0)
    def _():
        hc[...] = jnp.zeros_like(hc)

    pieces, ffn_result = _ffn_pieces(x1_prev[...], mod_prev, fg, wi, wd, n_chunks=wd.shape[1] // MXU_DIM)
    todo = list(pieces)
    run_next = lambda: todo.pop(0)() if todo else None
    run_two = lambda: (run_next(), run_next())
    run_two()
    _lru_coeffs(xc[...], gw, gbh, lam, a_s, b_s, between=run_two)
    ngroups = tc // SUBLANES
    n_scan = len(todo)

    def between(gi):
        while todo and (n_scan - len(todo)) * ngroups < (gi + 1) * n_scan:
            run_next()

    _lru_chunk_scan(a_s, b_s, hb_s, hc, reverse=True, tc=tc, between=between)
    out[...] = ffn_result()
    z = ((hf[...] + hb_s[...]) * gg[...]).astype(BF16)
    res = jnp.where(j < nc, ctx_res[...], x_res[...])
    x1_prev[...] = res + mod[0, 2:3, :] * jnp.dot(z, wout[...], preferred_element_type=F32)


def _lru_scans_ffn(u, conv_w, conv_b, gw, gbh, lam, gg, ctx2d, x2d, mods, wout, ffn_g, ffn_wi32, ffn_wd32, *,
                   layer, nb, ctx_len, seq, tc):
    n, r = u.shape
    d = x2d.shape[1]
    nc = ctx_len // tc
    nl = seq // tc
    blk8 = tc // SUBLANES

    def cur_blk(b, j, reverse):
        jc = (nc - 1 - j) if reverse else j
        jl = (nl - 1 - (j - nc)) if reverse else (j - nc)
        return jnp.where(j < nc, b * nc + jc, nb * nc + b * nl + jl)

    fwd_map = lambda b, j: (cur_blk(b, j, False), 0)
    bwd_map = lambda b, j: (cur_blk(b, j, True), 0)
    prev_map = lambda b, j: (jnp.maximum(cur_blk(b, j, False) * blk8 - 1, 0), 0)
    next_map = lambda b, j: (jnp.minimum((cur_blk(b, j, False) + 1) * blk8, n // SUBLANES - 1), 0)
    gate_specs = [
        _const_spec((LRU_BLOCKS, LRU_BLOCK, 2 * LRU_BLOCK)),
        _const_spec((2, r)),
        _const_spec((1, r)),
    ]
    chunk = pltpu.VMEM((tc, r), F32)
    carry = pltpu.VMEM((SUBLANES, r), F32)
    per_batch = nc + nl
    nsteps = nb * per_batch
    depth, hid = ffn_wd32.shape[:2]
    wi32 = ffn_wi32.reshape(depth * d, 2 * hid)
    wd32 = ffn_wd32.reshape(depth * hid, d)

    def slab_spec(w):
        rows = w.shape[0]
        blk = next(b for b in range(BF16_ROWS, rows + 1, BF16_ROWS) if rows % b == 0 and rows // b <= nsteps)
        return pl.BlockSpec((blk, w.shape[1]), lambda b, j: (jnp.minimum(b * per_batch + j, rows // blk - 1), 0))

    hf, xc, ffn_wi, ffn_wd = pl.pallas_call(
        functools.partial(_scan_fwd_kernel, tc=tc, nc=nc, nl=nl),
        grid=(nb, nc + nl),
        in_specs=[
            pl.BlockSpec((tc, r), fwd_map),
            pl.BlockSpec((SUBLANES, r), prev_map),
            pl.BlockSpec((SUBLANES, r), next_map),
            _const_spec((CONV_W, r)),
            _const_spec((1, r)),
        ] + gate_specs + [slab_spec(wi32), slab_spec(wd32)],
        out_specs=[pl.BlockSpec((tc, r), fwd_map), pl.BlockSpec((tc, r), fwd_map), slab_spec(wi32), slab_spec(wd32)],
        out_shape=[jax.ShapeDtypeStruct((n, r), F32), jax.ShapeDtypeStruct((n, r), F32),
                   jax.ShapeDtypeStruct(wi32.shape, BF16), jax.ShapeDtypeStruct(wd32.shape, BF16)],
        scratch_shapes=[pltpu.VMEM((tc + 2 * SUBLANES, r), F32), chunk, chunk, carry],
        compiler_params=_cparams(("arbitrary", "arbitrary")),
        name="lru_scan_fwd",
    )(u, u, u, conv_w, conv_b, gw[0], gbh[0], lam[0], wi32, wd32)
    ffn_wi = ffn_wi.reshape(depth, d, 2 * hid)
    ffn_wd = ffn_wd.reshape(depth, hid, d)

    def at(step_map, lag):
        def index_map(s):
            sc = jnp.clip(s - lag, 0, nsteps - 1)
            return step_map(sc // per_batch, sc % per_batch)
        return index_map

    ctx_res_map = lambda b, j: (b * nc + jnp.maximum(nc - 1 - j, 0), 0)
    x_res_map = lambda b, j: (b * nl + nl - 1 - jnp.maximum(j - nc, 0), 0)
    mod_map = lambda b, j: (jnp.where(j < nc, nb, b), 0, 0)
    x2 = pl.pallas_call(
        functools.partial(_scan_bwd_ffn_kernel, tc=tc, nc=nc, per_batch=per_batch, nsteps=nsteps),
        grid=(nsteps + 1,),
        in_specs=[pl.BlockSpec((tc, r), at(bwd_map, 0))] + gate_specs + [
            pl.BlockSpec((tc, r), at(bwd_map, 0)),
            pl.BlockSpec((tc, r), at(bwd_map, 0)),
            pl.BlockSpec((tc, d), at(ctx_res_map, 0)),
            pl.BlockSpec((tc, d), at(x_res_map, 0)),
            pl.BlockSpec((1, 6, d), at(mod_map, 0)),
            _const_spec((r, d)),
            pl.BlockSpec((1, 6, d), at(mod_map, 1)),
            _const_spec((1, d)),
            pl.BlockSpec((1, d, 2 * hid), lambda s: (layer, 0, 0), pipeline_mode=pl.Buffered(1)),
            pl.BlockSpec((1, hid, d), lambda s: (layer, 0, 0), pipeline_mode=pl.Buffered(1)),
        ],
        out_specs=pl.BlockSpec((tc, d), at(bwd_map, 1)),
        out_shape=jax.ShapeDtypeStruct((n, d), F32),
        scratch_shapes=[chunk, chunk, carry, chunk, pltpu.VMEM((tc, d), F32)],
        compiler_params=_cparams(("arbitrary",)),
        name="lru_scan_bwd_ffn",
    )(xc, gw[1], gbh[1], lam[1], hf, gg, ctx2d, x2d, mods, wout, mods, ffn_g, ffn_wi, ffn_wd)
    return x2, ffn_wi, ffn_wd


def _ffn_pieces(x, mod_ref, g_ref, wi_ref, wd_ref, *, n_chunks):
    h = _norm_mod(x, g_ref[...], mod_ref[0, 3:4, :], mod_ref[0, 4:5, :]).astype(BF16)
    hid = wd_ref.shape[1]
    n_mxu = hid // MXU_DIM
    edges = [round(c * n_mxu / n_chunks) * MXU_DIM for c in range(n_chunks)] + [hid]
    acc = [jnp.zeros(x.shape, F32)]
    pending = []

    def down():
        act, e0, e1 = pending.pop(0)
        acc[0] = acc[0] + jnp.dot(act, wd_ref[0, e0:e1, :], preferred_element_type=F32)

    def piece(e0, e1):
        def run():
            gate = jnp.dot(h, wi_ref[0, :, e0:e1], preferred_element_type=F32)
            up = jnp.dot(h, wi_ref[0, :, hid + e0:hid + e1], preferred_element_type=F32)
            if len(pending) >= FFN_LAG:
                down()
            pending.append((((gate * jax.nn.sigmoid(gate)) * up).astype(BF16), e0, e1))
        return run

    def result():
        while pending:
            down()
        return x + mod_ref[0, 5:6, :] * acc[0]

    return [piece(e0, e1) for e0, e1 in zip(edges[:-1], edges[1:])], result


def _ffn_kernel(*refs, with_attn, final_norm, n_chunks):
    if with_attn:
        x_ref, o_ref, wo_ref, mod_ref, g_ref, wi_ref, wd_ref, fg_ref, out_ref = refs
    else:
        x_ref, mod_ref, g_ref, wi_ref, wd_ref, fg_ref, out_ref = refs
    x = x_ref[...]
    if with_attn:
        x = x + mod_ref[0, 2:3, :] * jnp.dot(o_ref[...], wo_ref[...], preferred_element_type=F32)
    pieces, result = _ffn_pieces(x, mod_ref, g_ref, wi_ref, wd_ref, n_chunks=n_chunks)
    for run in pieces:
        run()
    x = result()
    if final_norm:
        x = _rmsnorm(x, fg_ref[...])
    out_ref[...] = x


def _ffn(xall, mods, g, wi, wd, fg, *, layer, nb, n_ctx_rows, seq, tm, lat_only, attn=None, final_norm=False):
    n, d = xall.shape
    hid = wd.shape[1]
    nct = n_ctx_rows // tm
    tpb = seq // tm
    off = nct if lat_only else 0
    n_out = n - n_ctx_rows if lat_only else n

    def mod_map(i):
        ii = i + off
        return (jnp.where(ii < nct, nb, (ii - nct) // tpb), 0, 0)

    in_specs = [pl.BlockSpec((tm, d), lambda i: (i + off, 0))]
    args = [xall]
    if attn is not None:
        o, wo = attn
        in_specs += [pl.BlockSpec((tm, o.shape[1]), lambda i: (i, 0)), _const_spec(wo.shape)]
        args += [o, wo]
    in_specs += [
        pl.BlockSpec((1, 6, d), mod_map),
        _const_spec((1, d)),
        pl.BlockSpec((1, d, 2 * hid), lambda i: (layer, 0, 0), pipeline_mode=pl.Buffered(1)),
        pl.BlockSpec((1, hid, d), lambda i: (layer, 0, 0), pipeline_mode=pl.Buffered(1)),
        _const_spec((1, d)),
    ]
    args += [mods, g, wi, wd, fg]
    return pl.pallas_call(
        functools.partial(_ffn_kernel, with_attn=attn is not None, final_norm=final_norm, n_chunks=2),
        grid=(n_out // tm,),
        in_specs=in_specs,
        out_specs=pl.BlockSpec((tm, d), lambda i: (i, 0)),
        out_shape=jax.ShapeDtypeStruct((n_out, d), F32),
        compiler_params=_cparams(("arbitrary",)),
        name="ffn_attn" if attn is not None else "ffn",
    )(*args)


def _mla_proj_kernel(*refs, latent):
    if latent:
        (x_ref, g_ref, mod_ref, win_ref, kvg_ref, wk_ref, wvt_ref, qg_ref, wq_ref, cos_ref, sin_ref,
         k_ref, v_ref, q_ref) = refs
        cosv, sinv = cos_ref[...], sin_ref[...]
    else:
        x_ref, g_ref, mod_ref, win_ref, kvg_ref, wk_ref, wvt_ref, k_ref, v_ref = refs
        cosv, sinv = 1.0, 0.0
    h = _norm_mod(x_ref[...], g_ref[...], mod_ref[0, 0:1, :], mod_ref[0, 1:2, :]).astype(BF16)
    c_all = jnp.dot(h, win_ref[...], preferred_element_type=F32)
    kv_lo = Q_LORA + KV_LORA
    lane = lax.broadcasted_iota(jnp.int32, (1, 128), 1)
    low = lane < QK_ROPE
    kt = c_all[:, kv_lo:kv_lo + 128] * jnp.where(low, cosv, sinv)
    kt = kt + pltpu.roll(kt, QK_ROPE, 1)
    k_rope = (jnp.where(low, kt, 0.0).astype(BF16), jnp.where(low, 0.0, kt).astype(BF16))
    ckv = _rmsnorm(c_all[:, Q_LORA:kv_lo], kvg_ref[...])
    k_nope = jnp.dot(ckv.astype(BF16), wk_ref[...], preferred_element_type=F32)
    v_t = jnp.dot(wvt_ref[...], ckv.T.astype(BF16), preferred_element_type=F32)
    for hd in range(MLA_HEADS):
        sl = slice(hd * 128, (hd + 1) * 128)
        k_ref[0, hd, :, 0:QK_NOPE] = k_nope[:, sl].astype(BF16)
        k_ref[0, hd, :, QK_NOPE:HEAD_PAD] = k_rope[hd % 2]
        v_ref[0, hd] = v_t[sl, :].astype(BF16)
    if latent:
        cq = _rmsnorm(c_all[:, :Q_LORA], qg_ref[...]).astype(BF16)
        q_all = jnp.dot(cq, wq_ref[...], preferred_element_type=F32) * Q_PRESCALE
        hw = MLA_HEADS * QK_NOPE
        hr = MLA_HEADS * QK_ROPE
        for pair in range(MLA_HEADS // 2):
            ps = slice(hw + pair * 128, hw + (pair + 1) * 128)
            ss = slice(hw + hr + pair * 128, hw + hr + (pair + 1) * 128)
            q_rope = (q_all[:, ps] * cosv + q_all[:, ss] * sinv).astype(BF16)
            for hd in (2 * pair, 2 * pair + 1):
                q_ref[0, hd, :, 0:QK_NOPE] = q_all[:, hd * 128:(hd + 1) * 128].astype(BF16)
                q_ref[0, hd, :, QK_NOPE:HEAD_PAD] = q_rope


def _mla_proj(xall, g, mods, win, kvg, wk, wvt, *, nb, ctx_len, seq, tm, latent_args=None):
    n, d = xall.shape
    latent = latent_args is not None
    rows = seq if latent else ctx_len
    off = nb * ctx_len // tm if latent else 0
    tpb = rows // tm
    kv_map = lambda i: (i // tpb, 0, i % tpb, 0)
    vt_map = lambda i: (i // tpb, 0, 0, i % tpb)
    in_specs = [
        pl.BlockSpec((tm, d), lambda i: (i + off, 0)),
        _const_spec((1, d)),
        pl.BlockSpec((1, 6, d), lambda i: ((i // tpb) if latent else nb, 0, 0)),
        _const_spec(win.shape),
        _const_spec((1, KV_LORA)),
        _const_spec(wk.shape),
        _const_spec(wvt.shape),
    ]
    args = [xall, g, mods, win, kvg, wk, wvt]
    out_specs = [pl.BlockSpec((1, MLA_HEADS, tm, HEAD_PAD), kv_map), pl.BlockSpec((1, MLA_HEADS, V_HEAD, tm), vt_map)]
    out_shape = [jax.ShapeDtypeStruct((nb, MLA_HEADS, rows, HEAD_PAD), BF16),
                 jax.ShapeDtypeStruct((nb, MLA_HEADS, V_HEAD, rows), BF16)]
    if latent:
        qg, wq, cos_t, sin_t = latent_args
        tab_map = lambda i: (i % tpb, 0)
        in_specs += [_const_spec((1, Q_LORA)), _const_spec(wq.shape),
                     pl.BlockSpec((tm, 128), tab_map), pl.BlockSpec((tm, 128), tab_map)]
        args += [qg, wq, cos_t, sin_t]
        out_specs.append(pl.BlockSpec((1, MLA_HEADS, tm, HEAD_PAD), kv_map))
        out_shape.append(jax.ShapeDtypeStruct((nb, MLA_HEADS, rows, HEAD_PAD), BF16))
    return pl.pallas_call(
        functools.partial(_mla_proj_kernel, latent=latent),
        grid=(nb * tpb,),
        in_specs=in_specs,
        out_specs=out_specs,
        out_shape=out_shape,
        compiler_params=_cparams(("arbitrary",)),
        name="mla_proj_lat" if latent else "mla_proj_ctx",
    )(*args)


def _attn_kernel(q_ref, k_ref, kc_ref, vt_ref, vtc_ref, o_ref, s0_ref, s1_ref, qt_ref, acc_ref, l_ref, *, tq, kc):
    n_lat = k_ref.shape[2]
    t_all = n_lat + kc_ref.shape[2]
    nkc = t_all // kc
    nq = q_ref.shape[2] // tq
    s_refs = (s0_ref, s1_ref)
    dims = (((1,), (1,)), ((), ()))

    def group_reduce(x, op):
        return op(x.reshape(x.shape[0] // SUBLANES, SUBLANES, tq), axis=0)

    def k_chunk(c):
        if c * kc < n_lat:
            return k_ref[0, 0, c * kc:(c + 1) * kc, :]
        return kc_ref[0, 0, c * kc - n_lat:(c + 1) * kc - n_lat, :]

    def vt_chunk(c):
        if c * kc < n_lat:
            return vt_ref[0, 0, :, c * kc:(c + 1) * kc]
        return vtc_ref[0, 0, :, c * kc - n_lat:(c + 1) * kc - n_lat]

    def scores(qt, c, s_ref):
        st = jnp.dot(k_chunk(c), qt, preferred_element_type=F32)
        s_ref[c * kc:(c + 1) * kc, :] = st
        return group_reduce(st, jnp.max)

    def q_tile_t(i):
        q = q_ref[0, 0, pl.ds(pl.multiple_of(i * tq, tq), tq), :]
        return q.astype(F32).T.astype(BF16)

    def finalize(i):
        l = jnp.sum(l_ref[...], axis=0, keepdims=True)
        rows = pl.ds(pl.multiple_of(i * tq, tq), tq)
        o_ref[0, rows, :] = (acc_ref[...] / l).T.astype(o_ref.dtype)

    def tile_step(i, slot, m8, with_next):
        s_cur, s_nxt = s_refs[slot], s_refs[1 - slot]
        m_next = jnp.full((SUBLANES, tq), -jnp.inf, F32)
        if with_next:
            qt_next = qt_ref[1 - slot]
        m = jnp.max(m8, axis=0, keepdims=True)
        l8 = jnp.zeros((SUBLANES, tq), F32)
        acc = jnp.zeros((V_HEAD, tq), F32)
        p_prev = None
        for c in range(nkc):
            if with_next:
                m_next = jnp.maximum(m_next, scores(qt_next, c, s_nxt))
            if p_prev is not None:
                acc = acc + jnp.dot(vt_chunk(c - 1), p_prev, preferred_element_type=F32)
            p = jnp.exp2(s_cur[c * kc:(c + 1) * kc, :] - m)
            l8 = l8 + group_reduce(p, jnp.sum)
            p_prev = p.astype(BF16)
            if c == 1:
                finalize(jnp.maximum(i - 1, 0))
            if c == nkc // 2 and with_next:
                qt_ref[slot] = q_tile_t(jnp.minimum(i + 2, nq - 1))
        acc_ref[...] = acc + jnp.dot(vt_chunk(nkc - 1), p_prev, preferred_element_type=F32)
        l_ref[...] = l8
        return m_next

    acc_ref[...] = jnp.zeros_like(acc_ref)
    l_ref[...] = jnp.ones_like(l_ref)
    qt0 = q_tile_t(0)
    qt_ref[1] = q_tile_t(min(1, nq - 1))
    m8 = scores(qt0, 0, s0_ref)
    for c in range(1, nkc):
        m8 = jnp.maximum(m8, scores(qt0, c, s0_ref))

    def step(i, m8):
        return lax.cond(i % 2 == 0,
                        lambda m: tile_step(i, 0, m, True),
                        lambda m: tile_step(i, 1, m, True), m8)

    m8 = lax.fori_loop(0, nq - 1, step, m8)
    tile_step(nq - 1, (nq - 1) % 2, m8, False)
    finalize(nq - 1)


def _attention(q, k, kc, vt, vtc, *, tq):
    nb, nh, seq, dp = q.shape
    ctx_len = kc.shape[2]
    t_all = seq + ctx_len
    chunk = int(np.gcd(np.gcd(MXU_DIM, seq), ctx_len))
    head = lambda b, h: (b, h, 0, 0)
    return pl.pallas_call(
        functools.partial(_attn_kernel, tq=tq, kc=chunk),
        grid=(nb, nh),
        scratch_shapes=[pltpu.VMEM((t_all, tq), F32), pltpu.VMEM((t_all, tq), F32),
                        pltpu.VMEM((2, dp, tq), BF16), pltpu.VMEM((V_HEAD, tq), F32), pltpu.VMEM((SUBLANES, tq), F32)],
        in_specs=[
            pl.BlockSpec((1, 1, seq, dp), head),
            pl.BlockSpec((1, 1, seq, dp), head),
            pl.BlockSpec((1, 1, ctx_len, dp), head),
            pl.BlockSpec((1, 1, V_HEAD, seq), head),
            pl.BlockSpec((1, 1, V_HEAD, ctx_len), head),
        ],
        out_specs=pl.BlockSpec((1, seq, V_HEAD), lambda b, h: (b, 0, h)),
        out_shape=jax.ShapeDtypeStruct((nb, seq, nh * V_HEAD), BF16),
        compiler_params=_cparams(("arbitrary", "arbitrary")),
        name="mla_attention",
    )(q, k, kc, vt, vtc)


def _rope_cols(w):
    qd = QK_ROPE // 4
    r0, r1, r2, r3 = (w[..., i * qd:(i + 1) * qd] for i in range(4))
    return jnp.concatenate([r0, r2, r1, r3], axis=-1), jnp.concatenate([-r1, -r3, r0, r2], axis=-1)


def _rope_tables(seq):
    rows = seq // GRID_W
    row_ids = np.repeat(np.arange(rows, dtype=np.float32), GRID_W)
    col_ids = np.tile(np.arange(GRID_W, dtype=np.float32), rows)
    axis_dim = QK_ROPE // 2
    expo = (np.arange(0, axis_dim, 2, dtype=np.float32) / np.float32(axis_dim)).astype(np.float32)
    inv_freq = (np.float32(1.0) / np.power(np.float32(ROPE_THETA), expo)).astype(np.float32)
    ang_r = (row_ids[:, None] * inv_freq).astype(np.float32)
    ang_c = (col_ids[:, None] * inv_freq).astype(np.float32)
    reps = 128 // (QK_ROPE // 2)
    cos_t = np.tile(np.concatenate([np.cos(ang_r), np.cos(ang_c)], axis=1), (1, reps))
    sin_t = np.tile(np.concatenate([np.sin(ang_r), np.sin(ang_c)], axis=1), (1, reps))
    return jnp.asarray(cos_t, F32), jnp.asarray(sin_t, F32)


def kernel(x, c, ctx, c_ctx, ada_w, ada_b, norm_mix_g, norm_ffn_g, ffn_w_in, ffn_w_out, lru_w_in, lru_conv_w,
           lru_conv_b, lru_gate_w, lru_gate_b, lru_lambda, lru_w_out, mla_w_in, mla_q_norm_g, mla_kv_norm_g,
           mla_w_uq, mla_w_ukv, mla_w_o, final_norm_g):
    nb, seq, d = x.shape
    ctx_len = ctx.shape[1]
    depth = ada_w.shape[0]
    assert depth == 2 and d == D_MODEL and nb + 1 <= MOD_ROWS
    n_ctx_rows = nb * ctx_len
    tile = lambda cap: int(np.gcd(np.gcd(cap, n_ctx_rows), seq))

    ctx2d = ctx.reshape(n_ctx_rows, d)
    x2d = x.reshape(nb * seq, d)

    cv = jnp.concatenate([c, c_ctx[None, :], jnp.zeros((MOD_ROWS - nb - 1, d), F32)], axis=0)
    mods = _ada_mods(cv, ada_w, ada_b).reshape(depth, MOD_ROWS, 6, d)

    tm = tile(1024)
    gg, u = _lru_in(ctx2d, x2d, norm_mix_g[0][None], mods[0], lru_w_in[0].astype(BF16), nb=nb, seq=seq, tm=tm)
    tc = int(np.gcd(np.gcd(256, ctx_len), seq))
    gw = lru_gate_w[0]
    gw = (0.5 * jnp.concatenate([gw[:, 0], gw[:, 1]], axis=-1)).astype(BF16)
    x2, ffn_wi, ffn_wd = _lru_scans_ffn(
        u, lru_conv_w[0], lru_conv_b[0][None], gw, 0.5 * lru_gate_b[0], lru_lambda[0][:, None, :], gg, ctx2d, x2d,
        mods[0], lru_w_out[0].astype(BF16), norm_ffn_g[0][None], ffn_w_in, ffn_w_out,
        layer=0, nb=nb, ctx_len=ctx_len, seq=seq, tc=tc)
    tf = tile(512)

    w_in = mla_w_in[0]
    win_p = jnp.concatenate((w_in[:, :Q_LORA + KV_LORA],) + _rope_cols(w_in[:, Q_LORA + KV_LORA:]),
                            axis=1).astype(BF16)
    wq3 = mla_w_uq[0].reshape(Q_LORA, MLA_HEADS, QK_NOPE + QK_ROPE)
    wq_all = jnp.concatenate([part.reshape(Q_LORA, -1)
                              for part in (wq3[:, :, :QK_NOPE],) + _rope_cols(wq3[:, :, QK_NOPE:])],
                             axis=1).astype(BF16)
    cos_t, sin_t = _rope_tables(seq)
    wkv3 = mla_w_ukv[0].reshape(KV_LORA, MLA_HEADS, QK_NOPE + V_HEAD)
    wk = wkv3[:, :, :QK_NOPE].reshape(KV_LORA, MLA_HEADS * QK_NOPE).astype(BF16)
    wvt = wkv3[:, :, QK_NOPE:].reshape(KV_LORA, MLA_HEADS * V_HEAD).T.astype(BF16)
    proj_args = (x2, norm_mix_g[1][None], mods[1], win_p, mla_kv_norm_g[0][None], wk, wvt)
    kc, vtc = _mla_proj(*proj_args, nb=nb, ctx_len=ctx_len, seq=seq, tm=int(np.gcd(256, ctx_len)))
    k, vt, q = _mla_proj(*proj_args, nb=nb, ctx_len=ctx_len, seq=seq, tm=tile(512),
                         latent_args=(mla_q_norm_g[0][None], wq_all, cos_t, sin_t))
    o = _attention(q, k, kc, vt, vtc, tq=int(np.gcd(512, seq)))
    out = _ffn(x2, mods[1], norm_ffn_g[1][None], ffn_wi, ffn_wd, final_norm_g[None], layer=1,
               nb=nb, n_ctx_rows=n_ctx_rows, seq=seq, tm=tf, lat_only=True,
               attn=(o.reshape(nb * seq, MLA_HEADS * V_HEAD), mla_w_o[0].astype(BF16)), final_norm=True)
    return out.reshape(nb, seq, d)
```

```python
import functools

import numpy as np
import jax
import jax.numpy as jnp
from jax import lax
from jax.experimental import pallas as pl
from jax.experimental.pallas import tpu as pltpu

F32 = jnp.float32
BF16 = jnp.bfloat16

D_MODEL = 1024
GRID_W = 64
NORM_EPS = 1e-6
LRU_WIDTH = D_MODEL
LRU_BLOCK = 256
LRU_BLOCKS = LRU_WIDTH // LRU_BLOCK
CONV_W = 4
RG_C = 8.0
MLA_HEADS = 8
Q_LORA = 384
KV_LORA = 256
QK_NOPE = 128
QK_ROPE = 64
V_HEAD = 128
SM_SCALE = (QK_NOPE + QK_ROPE) ** -0.5
LOG2_E = float(np.log2(np.e))
Q_PRESCALE = SM_SCALE * LOG2_E
ROPE_THETA = 10000.0

MOD_ROWS = 8
SUBLANES = 8
BF16_ROWS = 16
MXU_DIM = 256
HEAD_PAD = 256
VMEM_LIMIT_MB = 56
FFN_LAG = 2
TINY = 1e-30


def _cparams(sem, vmem_mb=VMEM_LIMIT_MB):
    return pltpu.CompilerParams(dimension_semantics=sem, vmem_limit_bytes=vmem_mb << 20)


def _const_spec(shape):
    nd = len(shape)
    return pl.BlockSpec(shape, lambda *_: (0,) * nd)


def _norm_mod(x, g, shift, scale):
    ms = jnp.mean(x * x, axis=-1, keepdims=True)
    y = (x * lax.rsqrt(ms + NORM_EPS)) * g
    return y * (1.0 + scale) + shift


def _rmsnorm(x, g):
    ms = jnp.mean(x * x, axis=-1, keepdims=True)
    return (x * lax.rsqrt(ms + NORM_EPS)) * g


def _ada_kernel(cv_ref, w_ref, b_ref, o_ref):
    cv = cv_ref[...]
    s = cv * jax.nn.sigmoid(cv)
    o_ref[0] = jnp.dot(s.astype(BF16), w_ref[0].astype(BF16), preferred_element_type=F32) + b_ref[0]


def _ada_mods(cv, ada_w, ada_b):
    depth, d, n = ada_w.shape
    tn = 1536
    return pl.pallas_call(
        _ada_kernel,
        grid=(depth, n // tn),
        in_specs=[
            pl.BlockSpec((MOD_ROWS, d), lambda l, j: (0, 0)),
            pl.BlockSpec((1, d, tn), lambda l, j: (l, 0, j)),
            pl.BlockSpec((1, 1, tn), lambda l, j: (l, 0, j)),
        ],
        out_specs=pl.BlockSpec((1, MOD_ROWS, tn), lambda l, j: (l, 0, j)),
        out_shape=jax.ShapeDtypeStruct((depth, MOD_ROWS, n), F32),
        compiler_params=_cparams(("arbitrary", "arbitrary")),
        name="ada_mods",
    )(cv, ada_w, ada_b.reshape(depth, 1, n))


def _lru_in_kernel(ctx_ref, x_ref, g_ref, mod_ref, w_ref, gg_ref, u_ref, *, nct):
    xin = jnp.where(pl.program_id(0) < nct, ctx_ref[...], x_ref[...])
    h = _norm_mod(xin, g_ref[...], mod_ref[0, 0:1, :], mod_ref[0, 1:2, :])
    y = jnp.dot(h.astype(BF16), w_ref[...], preferred_element_type=F32)
    r = LRU_WIDTH
    gg_ref[...] = jax.nn.gelu(y[:, :r])
    u_ref[...] = y[:, r:]


def _lru_in(ctx2d, x2d, g, mods, w, *, nb, seq, tm):
    d = x2d.shape[1]
    n = ctx2d.shape[0] + x2d.shape[0]
    r = LRU_WIDTH
    nct = ctx2d.shape[0] // tm
    tpb = seq // tm

    def mod_map(i):
        return (jnp.where(i < nct, nb, (i - nct) // tpb), 0, 0)

    return pl.pallas_call(
        functools.partial(_lru_in_kernel, nct=nct),
        grid=(n // tm,),
        in_specs=[
            pl.BlockSpec((tm, d), lambda i: (jnp.minimum(i, nct - 1), 0)),
            pl.BlockSpec((tm, d), lambda i: (jnp.maximum(i - nct, 0), 0)),
            _const_spec((1, d)),
            pl.BlockSpec((1, 6, d), mod_map),
            _const_spec((d, 2 * r)),
        ],
        out_specs=[pl.BlockSpec((tm, r), lambda i: (i, 0)), pl.BlockSpec((tm, r), lambda i: (i, 0))],
        out_shape=[jax.ShapeDtypeStruct((n, r), F32), jax.ShapeDtypeStruct((n, r), F32)],
        compiler_params=_cparams(("arbitrary",)),
        name="lru_in",
    )(ctx2d, x2d, g, mods, w)


def _lru_coeffs(xc, gw, gbh, lam, a_s, b_s, between=None):
    xcb = xc.astype(BF16)
    neg = -lam[...]
    sp = jnp.maximum(neg, 0.0) + jnp.log1p(jnp.exp(-jnp.abs(neg)))
    c2 = (-0.5 * RG_C * LOG2_E) * sp
    for n in range(LRU_BLOCKS):
        sl = slice(n * LRU_BLOCK, (n + 1) * LRU_BLOCK)
        pre = jnp.dot(xcb[:, sl], gw[n], preferred_element_type=F32)
        t_r = jnp.tanh(pre[:, :LRU_BLOCK] + gbh[0:1, sl])
        t_i = jnp.tanh(pre[:, LRU_BLOCK:] + gbh[1:2, sl])
        a = jnp.exp2(c2[:, sl] * t_r + c2[:, sl])
        gap = 1.0 - a * a
        mult = gap * lax.rsqrt(jnp.maximum(gap, TINY))
        a_s[:, sl] = a
        b_s[:, sl] = (mult * (0.5 * t_i + 0.5)) * xc[:, sl]
        if between is not None:
            between()


def _lru_chunk_scan(a_s, b_s, h_s, hc, *, reverse, tc, between=None):
    r = LRU_WIDTH
    row = lax.broadcasted_iota(jnp.int32, (SUBLANES, r), 0)
    ngroups = tc // SUBLANES

    def group(gi, hprev):
        g0 = (ngroups - 1 - gi) if reverse else gi
        start = g0 * SUBLANES
        rows = pl.ds(start if isinstance(start, int) else pl.multiple_of(start, SUBLANES), SUBLANES)
        av = a_s[rows, :]
        bv = b_s[rows, :]
        for s in (1, 2, 4):
            if reverse:
                keep = row < (SUBLANES - s)
                shift = SUBLANES - s
            else:
                keep = row >= s
                shift = s
            a_sh = jnp.where(keep, pltpu.roll(av, shift, 0), 1.0)
            b_sh = jnp.where(keep, pltpu.roll(bv, shift, 0), 0.0)
            bv = av * b_sh + bv
            av = av * a_sh
        hrows = av * hprev + bv
        h_s[rows, :] = hrows
        edge = hrows[0:1, :] if reverse else hrows[SUBLANES - 1:SUBLANES, :]
        return jnp.broadcast_to(edge, (SUBLANES, r))

    if between is None:
        hc[...] = lax.fori_loop(0, ngroups, group, hc[...], unroll=4)
    else:
        h = hc[...]
        for gi in range(ngroups):
            h = group(gi, h)
            between(gi)
        hc[...] = h


def _scan_fwd_kernel(ucur, uprev, unext, cw, cb, gw, gbh, lam, wi32, wd32, hf, xc_out, wi16, wd16,
                     ext, a_s, b_s, hc, *, tc, nc, nl):
    wi16[...] = wi32[...].astype(BF16)
    wd16[...] = wd32[...].astype(BF16)
    j = pl.program_id(1)
    in_ctx = j < nc
    pos = jnp.where(in_ctx, j, j - nc)
    first = pos == 0
    last = pos == jnp.where(in_ctx, nc, nl) - 1

    @pl.when(j == 0)
    def _():
        hc[...] = jnp.zeros_like(hc)

    n_ext = tc + 2 * SUBLANES
    ext[0:SUBLANES, :] = jnp.where(first, 0.0, uprev[...])
    ext[SUBLANES:SUBLANES + tc, :] = ucur[...]
    ext[SUBLANES + tc:n_ext, :] = jnp.where(last, 0.0, unext[...])
    e = ext[...]
    mid = slice(SUBLANES, SUBLANES + tc)
    xc = cb[...] + cw[0:1, :] * pltpu.roll(e, 1, 0)[mid] + cw[1:2, :] * e[mid] \
        + cw[2:3, :] * pltpu.roll(e, n_ext - 1, 0)[mid] + cw[3:4, :] * pltpu.roll(e, n_ext - 2, 0)[mid]
    xc_out[...] = xc
    _lru_coeffs(xc, gw, gbh, lam, a_s, b_s)
    _lru_chunk_scan(a_s, b_s, hf, hc, reverse=False, tc=tc)


def _scan_bwd_ffn_kernel(xc, gw, gbh, lam, hf, gg, ctx_res, x_res, mod, wout, mod_prev, fg, wi, wd,
                         out, a_s, b_s, hc, hb_s, x1_prev, *, tc, nc, per_batch, nsteps):
    s = pl.program_id(0)
    j = jnp.minimum(s, nsteps - 1) % per_batch

    @pl.when(s == 0)
    def _():
        x1_prev[...] = jnp.zeros_like(x1_prev)

    @pl.when(j == 0)
    def _():
        hc[...] = jnp.zeros_like(hc)

    pieces, ffn_result = _ffn_pieces(x1_prev[...], mod_prev, fg, wi, wd, n_chunks=wd.shape[1] // MXU_DIM)
    todo = list(pieces)
    run_next = lambda: todo.pop(0)() if todo else None
    run_two = lambda: (run_next(), run_next())
    run_two()
    _lru_coeffs(xc[...], gw, gbh, lam, a_s, b_s, between=run_two)
    ngroups = tc // SUBLANES
    n_scan = len(todo)

    def between(gi):
        while todo and (n_scan - len(todo)) * ngroups < (gi + 1) * n_scan:
            run_next()

    _lru_chunk_scan(a_s, b_s, hb_s, hc, reverse=True, tc=tc, between=between)
    out[...] = ffn_result()
    z = ((hf[...] + hb_s[...]) * gg[...]).astype(BF16)
    res = jnp.where(j < nc, ctx_res[...], x_res[...])
    x1_prev[...] = res + mod[0, 2:3, :] * jnp.dot(z, wout[...], preferred_element_type=F32)


def _lru_scans_ffn(u, conv_w, conv_b, gw, gbh, lam, gg, ctx2d, x2d, mods, wout, ffn_g, ffn_wi32, ffn_wd32, *,
                   layer, nb, ctx_len, seq, tc):
    n, r = u.shape
    d = x2d.shape[1]
    nc = ctx_len // tc
    nl = seq // tc
    blk8 = tc // SUBLANES

    def cur_blk(b, j, reverse):
        jc = (nc - 1 - j) if reverse else j
        jl = (nl - 1 - (j - nc)) if reverse else (j - nc)
        return jnp.where(j < nc, b * nc + jc, nb * nc + b * nl + jl)

    fwd_map = lambda b, j: (cur_blk(b, j, False), 0)
    bwd_map = lambda b, j: (cur_blk(b, j, True), 0)
    prev_map = lambda b, j: (jnp.maximum(cur_blk(b, j, False) * blk8 - 1, 0), 0)
    next_map = lambda b, j: (jnp.minimum((cur_blk(b, j, False) + 1) * blk8, n // SUBLANES - 1), 0)
    gate_specs = [
        _const_spec((LRU_BLOCKS, LRU_BLOCK, 2 * LRU_BLOCK)),
        _const_spec((2, r)),
        _const_spec((1, r)),
    ]
    chunk = pltpu.VMEM((tc, r), F32)
    carry = pltpu.VMEM((SUBLANES, r), F32)
    per_batch = nc + nl
    nsteps = nb * per_batch
    depth, hid = ffn_wd32.shape[:2]
    wi32 = ffn_wi32.reshape(depth * d, 2 * hid)
    wd32 = ffn_wd32.reshape(depth * hid, d)

    def slab_spec(w):
        rows = w.shape[0]
        blk = next(b for b in range(BF16_ROWS, rows + 1, BF16_ROWS) if rows % b == 0 and rows // b <= nsteps)
        return pl.BlockSpec((blk, w.shape[1]), lambda b, j: (jnp.minimum(b * per_batch + j, rows // blk - 1), 0))

    hf, xc, ffn_wi, ffn_wd = pl.pallas_call(
        functools.partial(_scan_fwd_kernel, tc=tc, nc=nc, nl=nl),
        grid=(nb, nc + nl),
        in_specs=[
            pl.BlockSpec((tc, r), fwd_map),
            pl.BlockSpec((SUBLANES, r), prev_map),
            pl.BlockSpec((SUBLANES, r), next_map),
            _const_spec((CONV_W, r)),
            _const_spec((1, r)),
        ] + gate_specs + [slab_spec(wi32), slab_spec(wd32)],
        out_specs=[pl.BlockSpec((tc, r), fwd_map), pl.BlockSpec((tc, r), fwd_map), slab_spec(wi32), slab_spec(wd32)],
        out_shape=[jax.ShapeDtypeStruct((n, r), F32), jax.ShapeDtypeStruct((n, r), F32),
                   jax.ShapeDtypeStruct(wi32.shape, BF16), jax.ShapeDtypeStruct(wd32.shape, BF16)],
        scratch_shapes=[pltpu.VMEM((tc + 2 * SUBLANES, r), F32), chunk, chunk, carry],
        compiler_params=_cparams(("arbitrary", "arbitrary")),
        name="lru_scan_fwd",
    )(u, u, u, conv_w, conv_b, gw[0], gbh[0], lam[0], wi32, wd32)
    ffn_wi = ffn_wi.reshape(depth, d, 2 * hid)
    ffn_wd = ffn_wd.reshape(depth, hid, d)

    def at(step_map, lag):
        def index_map(s):
            sc = jnp.clip(s - lag, 0, nsteps - 1)
            return step_map(sc // per_batch, sc % per_batch)
        return index_map

    ctx_res_map = lambda b, j: (b * nc + jnp.maximum(nc - 1 - j, 0), 0)
    x_res_map = lambda b, j: (b * nl + nl - 1 - jnp.maximum(j - nc, 0), 0)
    mod_map = lambda b, j: (jnp.where(j < nc, nb, b), 0, 0)
    x2 = pl.pallas_call(
        functools.partial(_scan_bwd_ffn_kernel, tc=tc, nc=nc, per_batch=per_batch, nsteps=nsteps),
        grid=(nsteps + 1,),
        in_specs=[pl.BlockSpec((tc, r), at(bwd_map, 0))] + gate_specs + [
            pl.BlockSpec((tc, r), at(bwd_map, 0)),
            pl.BlockSpec((tc, r), at(bwd_map, 0)),
            pl.BlockSpec((tc, d), at(ctx_res_map, 0)),
            pl.BlockSpec((tc, d), at(x_res_map, 0)),
            pl.BlockSpec((1, 6, d), at(mod_map, 0)),
            _const_spec((r, d)),
            pl.BlockSpec((1, 6, d), at(mod_map, 1)),
            _const_spec((1, d)),
            pl.BlockSpec((1, d, 2 * hid), lambda s: (layer, 0, 0), pipeline_mode=pl.Buffered(1)),
            pl.BlockSpec((1, hid, d), lambda s: (layer, 0, 0), pipeline_mode=pl.Buffered(1)),
        ],
        out_specs=pl.BlockSpec((tc, d), at(bwd_map, 1)),
        out_shape=jax.ShapeDtypeStruct((n, d), F32),
        scratch_shapes=[chunk, chunk, carry, chunk, pltpu.VMEM((tc, d), F32)],
        compiler_params=_cparams(("arbitrary",)),
        name="lru_scan_bwd_ffn",
    )(xc, gw[1], gbh[1], lam[1], hf, gg, ctx2d, x2d, mods, wout, mods, ffn_g, ffn_wi, ffn_wd)
    return x2, ffn_wi, ffn_wd


def _ffn_pieces(x, mod_ref, g_ref, wi_ref, wd_ref, *, n_chunks):
    h = _norm_mod(x, g_ref[...], mod_ref[0, 3:4, :], mod_ref[0, 4:5, :]).astype(BF16)
    hid = wd_ref.shape[1]
    n_mxu = hid // MXU_DIM
    edges = [round(c * n_mxu / n_chunks) * MXU_DIM for c in range(n_chunks)] + [hid]
    acc = [jnp.zeros(x.shape, F32)]
    pending = []

    def down():
        act, e0, e1 = pending.pop(0)
        acc[0] = acc[0] + jnp.dot(act, wd_ref[0, e0:e1, :], preferred_element_type=F32)

    def piece(e0, e1):
        def run():
            gate = jnp.dot(h, wi_ref[0, :, e0:e1], preferred_element_type=F32)
            up = jnp.dot(h, wi_ref[0, :, hid + e0:hid + e1], preferred_element_type=F32)
            if len(pending) >= FFN_LAG:
                down()
            pending.append((((gate * jax.nn.sigmoid(gate)) * up).astype(BF16), e0, e1))
        return run

    def result():
        while pending:
            down()
        return x + mod_ref[0, 5:6, :] * acc[0]

    return [piece(e0, e1) for e0, e1 in zip(edges[:-1], edges[1:])], result


def _ffn_kernel(*refs, with_attn, final_norm, n_chunks):
    if with_attn:
        x_ref, o_ref, wo_ref, mod_ref, g_ref, wi_ref, wd_ref, fg_ref, out_ref = refs
    else:
        x_ref, mod_ref, g_ref, wi_ref, wd_ref, fg_ref, out_ref = refs
    x = x_ref[...]
    if with_attn:
        x = x + mod_ref[0, 2:3, :] * jnp.dot(o_ref[...], wo_ref[...], preferred_element_type=F32)
    pieces, result = _ffn_pieces(x, mod_ref, g_ref, wi_ref, wd_ref, n_chunks=n_chunks)
    for run in pieces:
        run()
    x = result()
    if final_norm:
        x = _rmsnorm(x, fg_ref[...])
    out_ref[...] = x


def _ffn(xall, mods, g, wi, wd, fg, *, layer, nb, n_ctx_rows, seq, tm, lat_only, attn=None, final_norm=False):
    n, d = xall.shape
    hid = wd.shape[1]
    nct = n_ctx_rows // tm
    tpb = seq // tm
    off = nct if lat_only else 0
    n_out = n - n_ctx_rows if lat_only else n

    def mod_map(i):
        ii = i + off
        return (jnp.where(ii < nct, nb, (ii - nct) // tpb), 0, 0)

    in_specs = [pl.BlockSpec((tm, d), lambda i: (i + off, 0))]
    args = [xall]
    if attn is not None:
        o, wo = attn
        in_specs += [pl.BlockSpec((tm, o.shape[1]), lambda i: (i, 0)), _const_spec(wo.shape)]
        args += [o, wo]
    in_specs += [
        pl.BlockSpec((1, 6, d), mod_map),
        _const_spec((1, d)),
        pl.BlockSpec((1, d, 2 * hid), lambda i: (layer, 0, 0), pipeline_mode=pl.Buffered(1)),
        pl.BlockSpec((1, hid, d), lambda i: (layer, 0, 0), pipeline_mode=pl.Buffered(1)),
        _const_spec((1, d)),
    ]
    args += [mods, g, wi, wd, fg]
    return pl.pallas_call(
        functools.partial(_ffn_kernel, with_attn=attn is not None, final_norm=final_norm, n_chunks=2),
        grid=(n_out // tm,),
        in_specs=in_specs,
        out_specs=pl.BlockSpec((tm, d), lambda i: (i, 0)),
        out_shape=jax.ShapeDtypeStruct((n_out, d), F32),
        compiler_params=_cparams(("arbitrary",)),
        name="ffn_attn" if attn is not None else "ffn",
    )(*args)


def _mla_proj_kernel(*refs, latent, n_sub):
    if latent:
        (x_ref, g_ref, mod_ref, win_ref, kvg_ref, wk_ref, wvt_ref, qg_ref, wq_ref, cos_ref, sin_ref,
         k_ref, v_ref, q_ref) = refs
    else:
        x_ref, g_ref, mod_ref, win_ref, kvg_ref, wk_ref, wvt_ref, k_ref, v_ref = refs
    ts = x_ref.shape[0] // n_sub
    subs = [slice(i * ts, (i + 1) * ts) for i in range(n_sub)]
    kv_lo = Q_LORA + KV_LORA
    lane = lax.broadcasted_iota(jnp.int32, (1, 128), 1)
    low = lane < QK_ROPE
    hs = [_norm_mod(x_ref[sl, :], g_ref[...], mod_ref[0, 0:1, :], mod_ref[0, 1:2, :]).astype(BF16) for sl in subs]
    c_alls = [jnp.dot(h, win_ref[...], preferred_element_type=F32) for h in hs]
    for sl, c_all in zip(subs, c_alls):
        cosv, sinv = (cos_ref[sl, :], sin_ref[sl, :]) if latent else (1.0, 0.0)
        kt = c_all[:, kv_lo:kv_lo + 128] * jnp.where(low, cosv, sinv)
        kt = kt + pltpu.roll(kt, QK_ROPE, 1)
        k_rope = (jnp.where(low, kt, 0.0).astype(BF16), jnp.where(low, 0.0, kt).astype(BF16))
        ckv = _rmsnorm(c_all[:, Q_LORA:kv_lo], kvg_ref[...])
        k_nope = jnp.dot(ckv.astype(BF16), wk_ref[...], preferred_element_type=F32)
        v_t = jnp.dot(wvt_ref[...], ckv.T.astype(BF16), preferred_element_type=F32)
        for hd in range(MLA_HEADS):
            hl = slice(hd * 128, (hd + 1) * 128)
            k_ref[0, hd, sl, 0:QK_NOPE] = k_nope[:, hl].astype(BF16)
            k_ref[0, hd, sl, QK_NOPE:HEAD_PAD] = k_rope[hd % 2]
            v_ref[0, hd, :, sl] = v_t[hl, :].astype(BF16)
    if latent:
        for sl, c_all in zip(subs, c_alls):
            cq = _rmsnorm(c_all[:, :Q_LORA], qg_ref[...]).astype(BF16)
            q_all = jnp.dot(cq, wq_ref[...], preferred_element_type=F32) * Q_PRESCALE
            hw = MLA_HEADS * QK_NOPE
            hr = MLA_HEADS * QK_ROPE
            for pair in range(MLA_HEADS // 2):
                ps = slice(hw + pair * 128, hw + (pair + 1) * 128)
                ss = slice(hw + hr + pair * 128, hw + hr + (pair + 1) * 128)
                q_rope = (q_all[:, ps] * cos_ref[sl, :] + q_all[:, ss] * sin_ref[sl, :]).astype(BF16)
                for hd in (2 * pair, 2 * pair + 1):
                    q_ref[0, hd, sl, 0:QK_NOPE] = q_all[:, hd * 128:(hd + 1) * 128].astype(BF16)
                    q_ref[0, hd, sl, QK_NOPE:HEAD_PAD] = q_rope


def _mla_proj(xall, g, mods, win, kvg, wk, wvt, *, nb, ctx_len, seq, tm, latent_args=None):
    n, d = xall.shape
    latent = latent_args is not None
    rows = seq if latent else ctx_len
    off = nb * ctx_len // tm if latent else 0
    tpb = rows // tm
    kv_map = lambda i: (i // tpb, 0, i % tpb, 0)
    vt_map = lambda i: (i // tpb, 0, 0, i % tpb)
    in_specs = [
        pl.BlockSpec((tm, d), lambda i: (i + off, 0)),
        _const_spec((1, d)),
        pl.BlockSpec((1, 6, d), lambda i: ((i // tpb) if latent else nb, 0, 0)),
        _const_spec(win.shape),
        _const_spec((1, KV_LORA)),
        _const_spec(wk.shape),
        _const_spec(wvt.shape),
    ]
    args = [xall, g, mods, win, kvg, wk, wvt]
    out_specs = [pl.BlockSpec((1, MLA_HEADS, tm, HEAD_PAD), kv_map), pl.BlockSpec((1, MLA_HEADS, V_HEAD, tm), vt_map)]
    out_shape = [jax.ShapeDtypeStruct((nb, MLA_HEADS, rows, HEAD_PAD), BF16),
                 jax.ShapeDtypeStruct((nb, MLA_HEADS, V_HEAD, rows), BF16)]
    if latent:
        qg, wq, cos_t, sin_t = latent_args
        tab_map = lambda i: (i % tpb, 0)
        in_specs += [_const_spec((1, Q_LORA)), _const_spec(wq.shape),
                     pl.BlockSpec((tm, 128), tab_map), pl.BlockSpec((tm, 128), tab_map)]
        args += [qg, wq, cos_t, sin_t]
        out_specs.append(pl.BlockSpec((1, MLA_HEADS, tm, HEAD_PAD), kv_map))
        out_shape.append(jax.ShapeDtypeStruct((nb, MLA_HEADS, rows, HEAD_PAD), BF16))
    return pl.pallas_call(
        functools.partial(_mla_proj_kernel, latent=latent, n_sub=max(1, tm // MXU_DIM)),
        grid=(nb * tpb,),
        in_specs=in_specs,
        out_specs=out_specs,
        out_shape=out_shape,
        compiler_params=_cparams(("arbitrary",)),
        name="mla_proj_lat" if latent else "mla_proj_ctx",
    )(*args)


def _attn_kernel(q_ref, k_ref, kc_ref, vt_ref, vtc_ref, o_ref, s0_ref, s1_ref, qt_ref, acc_ref, l_ref, *, tq, kc):
    n_lat = k_ref.shape[2]
    t_all = n_lat + kc_ref.shape[2]
    nkc = t_all // kc
    nq = q_ref.shape[2] // tq
    s_refs = (s0_ref, s1_ref)
    dims = (((1,), (1,)), ((), ()))

    def group_reduce(x, op):
        return op(x.reshape(x.shape[0] // SUBLANES, SUBLANES, tq), axis=0)

    def k_chunk(c):
        if c * kc < n_lat:
            return k_ref[0, 0, c * kc:(c + 1) * kc, :]
        return kc_ref[0, 0, c * kc - n_lat:(c + 1) * kc - n_lat, :]

    def vt_chunk(c):
        if c * kc < n_lat:
            return vt_ref[0, 0, :, c * kc:(c + 1) * kc]
        return vtc_ref[0, 0, :, c * kc - n_lat:(c + 1) * kc - n_lat]

    def scores(qt, c, s_ref):
        st = jnp.dot(k_chunk(c), qt, preferred_element_type=F32)
        s_ref[c * kc:(c + 1) * kc, :] = st
        return group_reduce(st, jnp.max)

    def q_tile_t(i):
        q = q_ref[0, 0, pl.ds(pl.multiple_of(i * tq, tq), tq), :]
        return q.astype(F32).T.astype(BF16)

    def finalize(i):
        l = jnp.sum(l_ref[...], axis=0, keepdims=True)
        rows = pl.ds(pl.multiple_of(i * tq, tq), tq)
        o_ref[0, rows, :] = (acc_ref[...] / l).T.astype(o_ref.dtype)

    def tile_step(i, slot, m8, with_next):
        s_cur, s_nxt = s_refs[slot], s_refs[1 - slot]
        m_next = jnp.full((SUBLANES, tq), -jnp.inf, F32)
        if with_next:
            qt_next = qt_ref[1 - slot]
        m = jnp.max(m8, axis=0, keepdims=True)
        l8 = jnp.zeros((SUBLANES, tq), F32)
        acc = jnp.zeros((V_HEAD, tq), F32)
        p_prev = None
        for c in range(nkc):
            if with_next:
                m_next = jnp.maximum(m_next, scores(qt_next, c, s_nxt))
            if p_prev is not None:
                acc = acc + jnp.dot(vt_chunk(c - 1), p_prev, preferred_element_type=F32)
            p = jnp.exp2(s_cur[c * kc:(c + 1) * kc, :] - m)
            l8 = l8 + group_reduce(p, jnp.sum)
            p_prev = p.astype(BF16)
            if c == 1:
                finalize(jnp.maximum(i - 1, 0))
            if c == nkc // 2 and with_next:
                qt_ref[slot] = q_tile_t(jnp.minimum(i + 2, nq - 1))
        acc_ref[...] = acc + jnp.dot(vt_chunk(nkc - 1), p_prev, preferred_element_type=F32)
        l_ref[...] = l8
        return m_next

    acc_ref[...] = jnp.zeros_like(acc_ref)
    l_ref[...] = jnp.ones_like(l_ref)
    qt0 = q_tile_t(0)
    qt_ref[1] = q_tile_t(min(1, nq - 1))
    m8 = scores(qt0, 0, s0_ref)
    for c in range(1, nkc):
        m8 = jnp.maximum(m8, scores(qt0, c, s0_ref))

    def step(i, m8):
        return lax.cond(i % 2 == 0,
                        lambda m: tile_step(i, 0, m, True),
                        lambda m: tile_step(i, 1, m, True), m8)

    m8 = lax.fori_loop(0, nq - 1, step, m8)
    tile_step(nq - 1, (nq - 1) % 2, m8, False)
    finalize(nq - 1)


def _attention(q, k, kc, vt, vtc, *, tq):
    nb, nh, seq, dp = q.shape
    ctx_len = kc.shape[2]
    t_all = seq + ctx_len
    chunk = int(np.gcd(np.gcd(MXU_DIM, seq), ctx_len))
    head = lambda b, h: (b, h, 0, 0)
    return pl.pallas_call(
        functools.partial(_attn_kernel, tq=tq, kc=chunk),
        grid=(nb, nh),
        scratch_shapes=[pltpu.VMEM((t_all, tq), F32), pltpu.VMEM((t_all, tq), F32),
                        pltpu.VMEM((2, dp, tq), BF16), pltpu.VMEM((V_HEAD, tq), F32), pltpu.VMEM((SUBLANES, tq), F32)],
        in_specs=[
            pl.BlockSpec((1, 1, seq, dp), head),
            pl.BlockSpec((1, 1, seq, dp), head),
            pl.BlockSpec((1, 1, ctx_len, dp), head),
            pl.BlockSpec((1, 1, V_HEAD, seq), head),
            pl.BlockSpec((1, 1, V_HEAD, ctx_len), head),
        ],
        out_specs=pl.BlockSpec((1, seq, V_HEAD), lambda b, h: (b, 0, h)),
        out_shape=jax.ShapeDtypeStruct((nb, seq, nh * V_HEAD), BF16),
        compiler_params=_cparams(("arbitrary", "arbitrary")),
        name="mla_attention",
    )(q, k, kc, vt, vtc)


def _rope_cols(w):
    qd = QK_ROPE // 4
    r0, r1, r2, r3 = (w[..., i * qd:(i + 1) * qd] for i in range(4))
    return jnp.concatenate([r0, r2, r1, r3], axis=-1), jnp.concatenate([-r1, -r3, r0, r2], axis=-1)


def _rope_tables(seq):
    rows = seq // GRID_W
    row_ids = np.repeat(np.arange(rows, dtype=np.float32), GRID_W)
    col_ids = np.tile(np.arange(GRID_W, dtype=np.float32), rows)
    axis_dim = QK_ROPE // 2
    expo = (np.arange(0, axis_dim, 2, dtype=np.float32) / np.float32(axis_dim)).astype(np.float32)
    inv_freq = (np.float32(1.0) / np.power(np.float32(ROPE_THETA), expo)).astype(np.float32)
    ang_r = (row_ids[:, None] * inv_freq).astype(np.float32)
    ang_c = (col_ids[:, None] * inv_freq).astype(np.float32)
    reps = 128 // (QK_ROPE // 2)
    cos_t = np.tile(np.concatenate([np.cos(ang_r), np.cos(ang_c)], axis=1), (1, reps))
    sin_t = np.tile(np.concatenate([np.sin(ang_r), np.sin(ang_c)], axis=1), (1, reps))
    return jnp.asarray(cos_t, F32), jnp.asarray(sin_t, F32)


def kernel(x, c, ctx, c_ctx, ada_w, ada_b, norm_mix_g, norm_ffn_g, ffn_w_in, ffn_w_out, lru_w_in, lru_conv_w,
           lru_conv_b, lru_gate_w, lru_gate_b, lru_lambda, lru_w_out, mla_w_in, mla_q_norm_g, mla_kv_norm_g,
           mla_w_uq, mla_w_ukv, mla_w_o, final_norm_g):
    nb, seq, d = x.shape
    ctx_len = ctx.shape[1]
    depth = ada_w.shape[0]
    assert depth == 2 and d == D_MODEL and nb + 1 <= MOD_ROWS
    n_ctx_rows = nb * ctx_len
    tile = lambda cap: int(np.gcd(np.gcd(cap, n_ctx_rows), seq))

    ctx2d = ctx.reshape(n_ctx_rows, d)
    x2d = x.reshape(nb * seq, d)

    cv = jnp.concatenate([c, c_ctx[None, :], jnp.zeros((MOD_ROWS - nb - 1, d), F32)], axis=0)
    mods = _ada_mods(cv, ada_w, ada_b).reshape(depth, MOD_ROWS, 6, d)

    tm = tile(1024)
    gg, u = _lru_in(ctx2d, x2d, norm_mix_g[0][None], mods[0], lru_w_in[0].astype(BF16), nb=nb, seq=seq, tm=tm)
    tc = int(np.gcd(np.gcd(256, ctx_len), seq))
    gw = lru_gate_w[0]
    gw = (0.5 * jnp.concatenate([gw[:, 0], gw[:, 1]], axis=-1)).astype(BF16)
    x2, ffn_wi, ffn_wd = _lru_scans_ffn(
        u, lru_conv_w[0], lru_conv_b[0][None], gw, 0.5 * lru_gate_b[0], lru_lambda[0][:, None, :], gg, ctx2d, x2d,
        mods[0], lru_w_out[0].astype(BF16), norm_ffn_g[0][None], ffn_w_in, ffn_w_out,
        layer=0, nb=nb, ctx_len=ctx_len, seq=seq, tc=tc)
    tf = tile(512)

    w_in = mla_w_in[0]
    win_p = jnp.concatenate((w_in[:, :Q_LORA + KV_LORA],) + _rope_cols(w_in[:, Q_LORA + KV_LORA:]),
                            axis=1).astype(BF16)
    wq3 = mla_w_uq[0].reshape(Q_LORA, MLA_HEADS, QK_NOPE + QK_ROPE)
    wq_all = jnp.concatenate([part.reshape(Q_LORA, -1)
                              for part in (wq3[:, :, :QK_NOPE],) + _rope_cols(wq3[:, :, QK_NOPE:])],
                             axis=1).astype(BF16)
    cos_t, sin_t = _rope_tables(seq)
    wkv3 = mla_w_ukv[0].reshape(KV_LORA, MLA_HEADS, QK_NOPE + V_HEAD)
    wk = wkv3[:, :, :QK_NOPE].reshape(KV_LORA, MLA_HEADS * QK_NOPE).astype(BF16)
    wvt = wkv3[:, :, QK_NOPE:].reshape(KV_LORA, MLA_HEADS * V_HEAD).T.astype(BF16)
    proj_args = (x2, norm_mix_g[1][None], mods[1], win_p, mla_kv_norm_g[0][None], wk, wvt)
    kc, vtc = _mla_proj(*proj_args, nb=nb, ctx_len=ctx_len, seq=seq, tm=int(np.gcd(256, ctx_len)))
    k, vt, q = _mla_proj(*proj_args, nb=nb, ctx_len=ctx_len, seq=seq, tm=tile(1024),
                         latent_args=(mla_q_norm_g[0][None], wq_all, cos_t, sin_t))
    o = _attention(q, k, kc, vt, vtc, tq=int(np.gcd(512, seq)))
    out = _ffn(x2, mods[1], norm_ffn_g[1][None], ffn_wi, ffn_wd, final_norm_g[None], layer=1,
               nb=nb, n_ctx_rows=n_ctx_rows, seq=seq, tm=tf, lat_only=True,
               attn=(o.reshape(nb * seq, MLA_HEADS * V_HEAD), mla_w_o[0].astype(BF16)), final_norm=True)
    return out.reshape(nb, seq, d)
```

```python
import functools

import numpy as np
import jax
import jax.numpy as jnp
from jax import lax
from jax.experimental import pallas as pl
from jax.experimental.pallas import tpu as pltpu

F32 = jnp.float32
BF16 = jnp.bfloat16

D_MODEL = 1024
GRID_W = 64
NORM_EPS = 1e-6
LRU_WIDTH = D_MODEL
LRU_BLOCK = 256
LRU_BLOCKS = LRU_WIDTH // LRU_BLOCK
CONV_W = 4
RG_C = 8.0
MLA_HEADS = 8
Q_LORA = 384
KV_LORA = 256
QK_NOPE = 128
QK_ROPE = 64
V_HEAD = 128
SM_SCALE = (QK_NOPE + QK_ROPE) ** -0.5
LOG2_E = float(np.log2(np.e))
Q_PRESCALE = SM_SCALE * LOG2_E
ROPE_THETA = 10000.0

MOD_ROWS = 8
SUBLANES = 8
BF16_ROWS = 16
MXU_DIM = 256
HEAD_PAD = 256
VMEM_LIMIT_MB = 56
N_SUB_FFN = 2
FFN_LAG = 2
TINY = 1e-30


def _cparams(sem, vmem_mb=VMEM_LIMIT_MB):
    return pltpu.CompilerParams(dimension_semantics=sem, vmem_limit_bytes=vmem_mb << 20)


def _const_spec(shape):
    nd = len(shape)
    return pl.BlockSpec(shape, lambda *_: (0,) * nd)


def _norm_mod(x, g, shift, scale):
    ms = jnp.mean(x * x, axis=-1, keepdims=True)
    y = (x * lax.rsqrt(ms + NORM_EPS)) * g
    return y * (1.0 + scale) + shift


def _rmsnorm(x, g):
    ms = jnp.mean(x * x, axis=-1, keepdims=True)
    return (x * lax.rsqrt(ms + NORM_EPS)) * g


def _ada_kernel(cv_ref, w_ref, b_ref, o_ref):
    cv = cv_ref[...]
    s = cv * jax.nn.sigmoid(cv)
    o_ref[0] = jnp.dot(s.astype(BF16), w_ref[0].astype(BF16), preferred_element_type=F32) + b_ref[0]


def _ada_mods(cv, ada_w, ada_b):
    depth, d, n = ada_w.shape
    tn = 1536
    return pl.pallas_call(
        _ada_kernel,
        grid=(depth, n // tn),
        in_specs=[
            pl.BlockSpec((MOD_ROWS, d), lambda l, j: (0, 0)),
            pl.BlockSpec((1, d, tn), lambda l, j: (l, 0, j)),
            pl.BlockSpec((1, 1, tn), lambda l, j: (l, 0, j)),
        ],
        out_specs=pl.BlockSpec((1, MOD_ROWS, tn), lambda l, j: (l, 0, j)),
        out_shape=jax.ShapeDtypeStruct((depth, MOD_ROWS, n), F32),
        compiler_params=_cparams(("arbitrary", "arbitrary")),
        name="ada_mods",
    )(cv, ada_w, ada_b.reshape(depth, 1, n))


def _lru_in_kernel(ctx_ref, x_ref, g_ref, mod_ref, w_ref, gg_ref, u_ref, *, nct, n_sub):
    is_ctx = pl.program_id(0) < nct
    ts = x_ref.shape[0] // n_sub
    subs = [slice(i * ts, (i + 1) * ts) for i in range(n_sub)]
    hs = [_norm_mod(jnp.where(is_ctx, ctx_ref[sl, :], x_ref[sl, :]), g_ref[...],
                    mod_ref[0, 0:1, :], mod_ref[0, 1:2, :]).astype(BF16) for sl in subs]
    ys = [jnp.dot(h, w_ref[...], preferred_element_type=F32) for h in hs]
    r = LRU_WIDTH
    for sl, y in zip(subs, ys):
        gg_ref[sl, :] = jax.nn.gelu(y[:, :r])
        u_ref[sl, :] = y[:, r:]


def _lru_in(ctx2d, x2d, g, mods, w, *, nb, seq, tm):
    d = x2d.shape[1]
    n = ctx2d.shape[0] + x2d.shape[0]
    r = LRU_WIDTH
    nct = ctx2d.shape[0] // tm
    tpb = seq // tm

    def mod_map(i):
        return (jnp.where(i < nct, nb, (i - nct) // tpb), 0, 0)

    return pl.pallas_call(
        functools.partial(_lru_in_kernel, nct=nct, n_sub=max(1, tm // MXU_DIM)),
        grid=(n // tm,),
        in_specs=[
            pl.BlockSpec((tm, d), lambda i: (jnp.minimum(i, nct - 1), 0)),
            pl.BlockSpec((tm, d), lambda i: (jnp.maximum(i - nct, 0), 0)),
            _const_spec((1, d)),
            pl.BlockSpec((1, 6, d), mod_map),
            _const_spec((d, 2 * r)),
        ],
        out_specs=[pl.BlockSpec((tm, r), lambda i: (i, 0)), pl.BlockSpec((tm, r), lambda i: (i, 0))],
        out_shape=[jax.ShapeDtypeStruct((n, r), F32), jax.ShapeDtypeStruct((n, r), F32)],
        compiler_params=_cparams(("arbitrary",)),
        name="lru_in",
    )(ctx2d, x2d, g, mods, w)


def _lru_coeffs(xc, gw, gbh, lam, a_s, b_s, between=None):
    xcb = xc.astype(BF16)
    neg = -lam[...]
    sp = jnp.maximum(neg, 0.0) + jnp.log1p(jnp.exp(-jnp.abs(neg)))
    c2 = (-0.5 * RG_C * LOG2_E) * sp
    for n in range(LRU_BLOCKS):
        sl = slice(n * LRU_BLOCK, (n + 1) * LRU_BLOCK)
        pre = jnp.dot(xcb[:, sl], gw[n], preferred_element_type=F32)
        t_r = jnp.tanh(pre[:, :LRU_BLOCK] + gbh[0:1, sl])
        t_i = jnp.tanh(pre[:, LRU_BLOCK:] + gbh[1:2, sl])
        a = jnp.exp2(c2[:, sl] * t_r + c2[:, sl])
        gap = 1.0 - a * a
        mult = gap * lax.rsqrt(jnp.maximum(gap, TINY))
        a_s[:, sl] = a
        b_s[:, sl] = (mult * (0.5 * t_i + 0.5)) * xc[:, sl]
        if between is not None:
            between()


def _lru_chunk_scan(a_s, b_s, h_s, hc, *, reverse, tc, between=None):
    r = LRU_WIDTH
    row = lax.broadcasted_iota(jnp.int32, (SUBLANES, r), 0)
    ngroups = tc // SUBLANES

    def group(gi, hprev):
        g0 = (ngroups - 1 - gi) if reverse else gi
        start = g0 * SUBLANES
        rows = pl.ds(start if isinstance(start, int) else pl.multiple_of(start, SUBLANES), SUBLANES)
        av = a_s[rows, :]
        bv = b_s[rows, :]
        for s in (1, 2, 4):
            if reverse:
                keep = row < (SUBLANES - s)
                shift = SUBLANES - s
            else:
                keep = row >= s
                shift = s
            a_sh = jnp.where(keep, pltpu.roll(av, shift, 0), 1.0)
            b_sh = jnp.where(keep, pltpu.roll(bv, shift, 0), 0.0)
            bv = av * b_sh + bv
            av = av * a_sh
        hrows = av * hprev + bv
        h_s[rows, :] = hrows
        edge = hrows[0:1, :] if reverse else hrows[SUBLANES - 1:SUBLANES, :]
        return jnp.broadcast_to(edge, (SUBLANES, r))

    if between is None:
        hc[...] = lax.fori_loop(0, ngroups, group, hc[...], unroll=4)
    else:
        h = hc[...]
        for gi in range(ngroups):
            h = group(gi, h)
            between(gi)
        hc[...] = h


def _scan_fwd_kernel(ucur, uprev, unext, cw, cb, gw, gbh, lam, wi32, wd32, hf, xc_out, wi16, wd16,
                     ext, a_s, b_s, hc, *, tc, nc, nl):
    wi16[...] = wi32[...].astype(BF16)
    wd16[...] = wd32[...].astype(BF16)
    j = pl.program_id(1)
    in_ctx = j < nc
    pos = jnp.where(in_ctx, j, j - nc)
    first = pos == 0
    last = pos == jnp.where(in_ctx, nc, nl) - 1

    @pl.when(j == 0)
    def _():
        hc[...] = jnp.zeros_like(hc)

    n_ext = tc + 2 * SUBLANES
    ext[0:SUBLANES, :] = jnp.where(first, 0.0, uprev[...])
    ext[SUBLANES:SUBLANES + tc, :] = ucur[...]
    ext[SUBLANES + tc:n_ext, :] = jnp.where(last, 0.0, unext[...])
    e = ext[...]
    mid = slice(SUBLANES, SUBLANES + tc)
    xc = cb[...] + cw[0:1, :] * pltpu.roll(e, 1, 0)[mid] + cw[1:2, :] * e[mid] \
        + cw[2:3, :] * pltpu.roll(e, n_ext - 1, 0)[mid] + cw[3:4, :] * pltpu.roll(e, n_ext - 2, 0)[mid]
    xc_out[...] = xc
    _lru_coeffs(xc, gw, gbh, lam, a_s, b_s)
    _lru_chunk_scan(a_s, b_s, hf, hc, reverse=False, tc=tc)


def _scan_bwd_ffn_kernel(xc, gw, gbh, lam, hf, gg, ctx_res, x_res, mod, wout, mod_prev, fg, wi, wd,
                         out, a_s, b_s, hc, hb_s, x1_prev, *, tc, nc, per_batch, nsteps):
    s = pl.program_id(0)
    j = jnp.minimum(s, nsteps - 1) % per_batch

    @pl.when(s == 0)
    def _():
        x1_prev[...] = jnp.zeros_like(x1_prev)

    @pl.when(j == 0)
    def _():
        hc[...] = jnp.zeros_like(hc)

    pieces, ffn_result = _ffn_pieces(x1_prev[...], mod_prev, fg, wi, wd, n_chunks=wd.shape[1] // MXU_DIM)
    todo = list(pieces)
    run_next = lambda: todo.pop(0)() if todo else None
    run_two = lambda: (run_next(), run_next())
    run_two()
    _lru_coeffs(xc[...], gw, gbh, lam, a_s, b_s, between=run_two)
    ngroups = tc // SUBLANES
    n_scan = len(todo)

    def between(gi):
        while todo and (n_scan - len(todo)) * ngroups < (gi + 1) * n_scan:
            run_next()

    _lru_chunk_scan(a_s, b_s, hb_s, hc, reverse=True, tc=tc, between=between)
    out[...] = ffn_result()
    z = ((hf[...] + hb_s[...]) * gg[...]).astype(BF16)
    res = jnp.where(j < nc, ctx_res[...], x_res[...])
    x1_prev[...] = res + mod[0, 2:3, :] * jnp.dot(z, wout[...], preferred_element_type=F32)


def _lru_scans_ffn(u, conv_w, conv_b, gw, gbh, lam, gg, ctx2d, x2d, mods, wout, ffn_g, ffn_wi32, ffn_wd32, *,
                   layer, nb, ctx_len, seq, tc):
    n, r = u.shape
    d = x2d.shape[1]
    nc = ctx_len // tc
    nl = seq // tc
    blk8 = tc // SUBLANES

    def cur_blk(b, j, reverse):
        jc = (nc - 1 - j) if reverse else j
        jl = (nl - 1 - (j - nc)) if reverse else (j - nc)
        return jnp.where(j < nc, b * nc + jc, nb * nc + b * nl + jl)

    fwd_map = lambda b, j: (cur_blk(b, j, False), 0)
    bwd_map = lambda b, j: (cur_blk(b, j, True), 0)
    prev_map = lambda b, j: (jnp.maximum(cur_blk(b, j, False) * blk8 - 1, 0), 0)
    next_map = lambda b, j: (jnp.minimum((cur_blk(b, j, False) + 1) * blk8, n // SUBLANES - 1), 0)
    gate_specs = [
        _const_spec((LRU_BLOCKS, LRU_BLOCK, 2 * LRU_BLOCK)),
        _const_spec((2, r)),
        _const_spec((1, r)),
    ]
    chunk = pltpu.VMEM((tc, r), F32)
    carry = pltpu.VMEM((SUBLANES, r), F32)
    per_batch = nc + nl
    nsteps = nb * per_batch
    depth, hid = ffn_wd32.shape[:2]
    wi32 = ffn_wi32.reshape(depth * d, 2 * hid)
    wd32 = ffn_wd32.reshape(depth * hid, d)

    def slab_spec(w):
        rows = w.shape[0]
        blk = next(b for b in range(BF16_ROWS, rows + 1, BF16_ROWS) if rows % b == 0 and rows // b <= nsteps)
        return pl.BlockSpec((blk, w.shape[1]), lambda b, j: (jnp.minimum(b * per_batch + j, rows // blk - 1), 0))

    hf, xc, ffn_wi, ffn_wd = pl.pallas_call(
        functools.partial(_scan_fwd_kernel, tc=tc, nc=nc, nl=nl),
        grid=(nb, nc + nl),
        in_specs=[
            pl.BlockSpec((tc, r), fwd_map),
            pl.BlockSpec((SUBLANES, r), prev_map),
            pl.BlockSpec((SUBLANES, r), next_map),
            _const_spec((CONV_W, r)),
            _const_spec((1, r)),
        ] + gate_specs + [slab_spec(wi32), slab_spec(wd32)],
        out_specs=[pl.BlockSpec((tc, r), fwd_map), pl.BlockSpec((tc, r), fwd_map), slab_spec(wi32), slab_spec(wd32)],
        out_shape=[jax.ShapeDtypeStruct((n, r), F32), jax.ShapeDtypeStruct((n, r), F32),
                   jax.ShapeDtypeStruct(wi32.shape, BF16), jax.ShapeDtypeStruct(wd32.shape, BF16)],
        scratch_shapes=[pltpu.VMEM((tc + 2 * SUBLANES, r), F32), chunk, chunk, carry],
        compiler_params=_cparams(("arbitrary", "arbitrary")),
        name="lru_scan_fwd",
    )(u, u, u, conv_w, conv_b, gw[0], gbh[0], lam[0], wi32, wd32)
    ffn_wi = ffn_wi.reshape(depth, d, 2 * hid)
    ffn_wd = ffn_wd.reshape(depth, hid, d)

    def at(step_map, lag):
        def index_map(s):
            sc = jnp.clip(s - lag, 0, nsteps - 1)
            return step_map(sc // per_batch, sc % per_batch)
        return index_map

    ctx_res_map = lambda b, j: (b * nc + jnp.maximum(nc - 1 - j, 0), 0)
    x_res_map = lambda b, j: (b * nl + nl - 1 - jnp.maximum(j - nc, 0), 0)
    mod_map = lambda b, j: (jnp.where(j < nc, nb, b), 0, 0)
    x2 = pl.pallas_call(
        functools.partial(_scan_bwd_ffn_kernel, tc=tc, nc=nc, per_batch=per_batch, nsteps=nsteps),
        grid=(nsteps + 1,),
        in_specs=[pl.BlockSpec((tc, r), at(bwd_map, 0))] + gate_specs + [
            pl.BlockSpec((tc, r), at(bwd_map, 0)),
            pl.BlockSpec((tc, r), at(bwd_map, 0)),
            pl.BlockSpec((tc, d), at(ctx_res_map, 0)),
            pl.BlockSpec((tc, d), at(x_res_map, 0)),
            pl.BlockSpec((1, 6, d), at(mod_map, 0)),
            _const_spec((r, d)),
            pl.BlockSpec((1, 6, d), at(mod_map, 1)),
            _const_spec((1, d)),
            pl.BlockSpec((1, d, 2 * hid), lambda s: (layer, 0, 0), pipeline_mode=pl.Buffered(1)),
            pl.BlockSpec((1, hid, d), lambda s: (layer, 0, 0), pipeline_mode=pl.Buffered(1)),
        ],
        out_specs=pl.BlockSpec((tc, d), at(bwd_map, 1)),
        out_shape=jax.ShapeDtypeStruct((n, d), F32),
        scratch_shapes=[chunk, chunk, carry, chunk, pltpu.VMEM((tc, d), F32)],
        compiler_params=_cparams(("arbitrary",)),
        name="lru_scan_bwd_ffn",
    )(xc, gw[1], gbh[1], lam[1], hf, gg, ctx2d, x2d, mods, wout, mods, ffn_g, ffn_wi, ffn_wd)
    return x2, ffn_wi, ffn_wd


def _ffn_pieces(x, mod_ref, g_ref, wi_ref, wd_ref, *, n_chunks):
    h = _norm_mod(x, g_ref[...], mod_ref[0, 3:4, :], mod_ref[0, 4:5, :]).astype(BF16)
    hid = wd_ref.shape[1]
    n_mxu = hid // MXU_DIM
    edges = [round(c * n_mxu / n_chunks) * MXU_DIM for c in range(n_chunks)] + [hid]
    acc = [jnp.zeros(x.shape, F32)]
    pending = []

    def down():
        act, e0, e1 = pending.pop(0)
        acc[0] = acc[0] + jnp.dot(act, wd_ref[0, e0:e1, :], preferred_element_type=F32)

    def piece(e0, e1):
        def run():
            gate = jnp.dot(h, wi_ref[0, :, e0:e1], preferred_element_type=F32)
            up = jnp.dot(h, wi_ref[0, :, hid + e0:hid + e1], preferred_element_type=F32)
            if len(pending) >= FFN_LAG:
                down()
            pending.append((((gate * jax.nn.sigmoid(gate)) * up).astype(BF16), e0, e1))
        return run

    def result():
        while pending:
            down()
        return x + mod_ref[0, 5:6, :] * acc[0]

    return [piece(e0, e1) for e0, e1 in zip(edges[:-1], edges[1:])], result


def _ffn_kernel(*refs, with_attn, final_norm, n_chunks):
    if with_attn:
        x_ref, o_ref, wo_ref, mod_ref, g_ref, wi_ref, wd_ref, fg_ref, out_ref = refs
    else:
        x_ref, mod_ref, g_ref, wi_ref, wd_ref, fg_ref, out_ref = refs
    ts = x_ref.shape[0] // N_SUB_FFN
    subs = [slice(i * ts, (i + 1) * ts) for i in range(N_SUB_FFN)]
    xs = [x_ref[sl, :] for sl in subs]
    if with_attn:
        xs = [x + mod_ref[0, 2:3, :] * jnp.dot(o_ref[sl, :], wo_ref[...], preferred_element_type=F32)
              for x, sl in zip(xs, subs)]
    ffns = [_ffn_pieces(x, mod_ref, g_ref, wi_ref, wd_ref, n_chunks=n_chunks) for x in xs]
    for c in range(n_chunks):
        for pieces, _ in ffns:
            pieces[c]()
    for sl, (_, result) in zip(subs, ffns):
        x = result()
        if final_norm:
            x = _rmsnorm(x, fg_ref[...])
        out_ref[sl, :] = x


def _ffn(xall, mods, g, wi, wd, fg, *, layer, nb, n_ctx_rows, seq, tm, lat_only, attn=None, final_norm=False):
    n, d = xall.shape
    hid = wd.shape[1]
    nct = n_ctx_rows // tm
    tpb = seq // tm
    off = nct if lat_only else 0
    n_out = n - n_ctx_rows if lat_only else n

    def mod_map(i):
        ii = i + off
        return (jnp.where(ii < nct, nb, (ii - nct) // tpb), 0, 0)

    in_specs = [pl.BlockSpec((tm, d), lambda i: (i + off, 0))]
    args = [xall]
    if attn is not None:
        o, wo = attn
        in_specs += [pl.BlockSpec((tm, o.shape[1]), lambda i: (i, 0)), _const_spec(wo.shape)]
        args += [o, wo]
    in_specs += [
        pl.BlockSpec((1, 6, d), mod_map),
        _const_spec((1, d)),
        pl.BlockSpec((1, d, 2 * hid), lambda i: (layer, 0, 0), pipeline_mode=pl.Buffered(1)),
        pl.BlockSpec((1, hid, d), lambda i: (layer, 0, 0), pipeline_mode=pl.Buffered(1)),
        _const_spec((1, d)),
    ]
    args += [mods, g, wi, wd, fg]
    return pl.pallas_call(
        functools.partial(_ffn_kernel, with_attn=attn is not None, final_norm=final_norm, n_chunks=2),
        grid=(n_out // tm,),
        in_specs=in_specs,
        out_specs=pl.BlockSpec((tm, d), lambda i: (i, 0)),
        out_shape=jax.ShapeDtypeStruct((n_out, d), F32),
        compiler_params=_cparams(("arbitrary",)),
        name="ffn_attn" if attn is not None else "ffn",
    )(*args)


def _mla_proj_kernel(*refs, latent, n_sub):
    if latent:
        (x_ref, g_ref, mod_ref, win_ref, kvg_ref, wk_ref, wvt_ref, qg_ref, wq_ref, cos_ref, sin_ref,
         k_ref, v_ref, q_ref) = refs
    else:
        x_ref, g_ref, mod_ref, win_ref, kvg_ref, wk_ref, wvt_ref, k_ref, v_ref = refs
    ts = x_ref.shape[0] // n_sub
    subs = [slice(i * ts, (i + 1) * ts) for i in range(n_sub)]
    kv_lo = Q_LORA + KV_LORA
    lane = lax.broadcasted_iota(jnp.int32, (1, 128), 1)
    low = lane < QK_ROPE
    hs = [_norm_mod(x_ref[sl, :], g_ref[...], mod_ref[0, 0:1, :], mod_ref[0, 1:2, :]).astype(BF16) for sl in subs]
    c_alls = [jnp.dot(h, win_ref[...], preferred_element_type=F32) for h in hs]
    for sl, c_all in zip(subs, c_alls):
        cosv, sinv = (cos_ref[sl, :], sin_ref[sl, :]) if latent else (1.0, 0.0)
        kt = c_all[:, kv_lo:kv_lo + 128] * jnp.where(low, cosv, sinv)
        kt = kt + pltpu.roll(kt, QK_ROPE, 1)
        k_rope = (jnp.where(low, kt, 0.0).astype(BF16), jnp.where(low, 0.0, kt).astype(BF16))
        ckv = _rmsnorm(c_all[:, Q_LORA:kv_lo], kvg_ref[...])
        k_nope = jnp.dot(ckv.astype(BF16), wk_ref[...], preferred_element_type=F32)
        v_t = jnp.dot(wvt_ref[...], ckv.T.astype(BF16), preferred_element_type=F32)
        for hd in range(MLA_HEADS):
            hl = slice(hd * 128, (hd + 1) * 128)
            k_ref[0, hd, sl, 0:QK_NOPE] = k_nope[:, hl].astype(BF16)
            k_ref[0, hd, sl, QK_NOPE:HEAD_PAD] = k_rope[hd % 2]
            v_ref[0, hd, :, sl] = v_t[hl, :].astype(BF16)
    if latent:
        for sl, c_all in zip(subs, c_alls):
            cq = _rmsnorm(c_all[:, :Q_LORA], qg_ref[...]).astype(BF16)
            q_all = jnp.dot(cq, wq_ref[...], preferred_element_type=F32) * Q_PRESCALE
            hw = MLA_HEADS * QK_NOPE
            hr = MLA_HEADS * QK_ROPE
            for pair in range(MLA_HEADS // 2):
                ps = slice(hw + pair * 128, hw + (pair + 1) * 128)
                ss = slice(hw + hr + pair * 128, hw + hr + (pair + 1) * 128)
                q_rope = (q_all[:, ps] * cos_ref[sl, :] + q_all[:, ss] * sin_ref[sl, :]).astype(BF16)
                for hd in (2 * pair, 2 * pair + 1):
                    q_ref[0, hd, sl, 0:QK_NOPE] = q_all[:, hd * 128:(hd + 1) * 128].astype(BF16)
                    q_ref[0, hd, sl, QK_NOPE:HEAD_PAD] = q_rope


def _mla_proj(xall, g, mods, win, kvg, wk, wvt, *, nb, ctx_len, seq, tm, latent_args=None):
    n, d = xall.shape
    latent = latent_args is not None
    rows = seq if latent else ctx_len
    off = nb * ctx_len // tm if latent else 0
    tpb = rows // tm
    kv_map = lambda i: (i // tpb, 0, i % tpb, 0)
    vt_map = lambda i: (i // tpb, 0, 0, i % tpb)
    in_specs = [
        pl.BlockSpec((tm, d), lambda i: (i + off, 0)),
        _const_spec((1, d)),
        pl.BlockSpec((1, 6, d), lambda i: ((i // tpb) if latent else nb, 0, 0)),
        _const_spec(win.shape),
        _const_spec((1, KV_LORA)),
        _const_spec(wk.shape),
        _const_spec(wvt.shape),
    ]
    args = [xall, g, mods, win, kvg, wk, wvt]
    out_specs = [pl.BlockSpec((1, MLA_HEADS, tm, HEAD_PAD), kv_map), pl.BlockSpec((1, MLA_HEADS, V_HEAD, tm), vt_map)]
    out_shape = [jax.ShapeDtypeStruct((nb, MLA_HEADS, rows, HEAD_PAD), BF16),
                 jax.ShapeDtypeStruct((nb, MLA_HEADS, V_HEAD, rows), BF16)]
    if latent:
        qg, wq, cos_t, sin_t = latent_args
        tab_map = lambda i: (i % tpb, 0)
        in_specs += [_const_spec((1, Q_LORA)), _const_spec(wq.shape),
                     pl.BlockSpec((tm, 128), tab_map), pl.BlockSpec((tm, 128), tab_map)]
        args += [qg, wq, cos_t, sin_t]
        out_specs.append(pl.BlockSpec((1, MLA_HEADS, tm, HEAD_PAD), kv_map))
        out_shape.append(jax.ShapeDtypeStruct((nb, MLA_HEADS, rows, HEAD_PAD), BF16))
    return pl.pallas_call(
        functools.partial(_mla_proj_kernel, latent=latent, n_sub=max(1, tm // MXU_DIM)),
        grid=(nb * tpb,),
        in_specs=in_specs,
        out_specs=out_specs,
        out_shape=out_shape,
        compiler_params=_cparams(("arbitrary",)),
        name="mla_proj_lat" if latent else "mla_proj_ctx",
    )(*args)


def _attn_kernel(q_ref, k_ref, kc_ref, vt_ref, vtc_ref, o_ref, s0_ref, s1_ref, qt_ref, acc_ref, l_ref, *, tq, kc):
    n_lat = k_ref.shape[2]
    t_all = n_lat + kc_ref.shape[2]
    nkc = t_all // kc
    nq = q_ref.shape[2] // tq
    s_refs = (s0_ref, s1_ref)
    dims = (((1,), (1,)), ((), ()))

    def group_reduce(x, op):
        return op(x.reshape(x.shape[0] // SUBLANES, SUBLANES, tq), axis=0)

    def k_chunk(c):
        if c * kc < n_lat:
            return k_ref[0, 0, c * kc:(c + 1) * kc, :]
        return kc_ref[0, 0, c * kc - n_lat:(c + 1) * kc - n_lat, :]

    def vt_chunk(c):
        if c * kc < n_lat:
            return vt_ref[0, 0, :, c * kc:(c + 1) * kc]
        return vtc_ref[0, 0, :, c * kc - n_lat:(c + 1) * kc - n_lat]

    def scores(qt, c, s_ref):
        st = jnp.dot(k_chunk(c), qt, preferred_element_type=F32)
        s_ref[c * kc:(c + 1) * kc, :] = st
        return group_reduce(st, jnp.max)

    def q_tile_t(i):
        q = q_ref[0, 0, pl.ds(pl.multiple_of(i * tq, tq), tq), :]
        return q.astype(F32).T.astype(BF16)

    def finalize(i):
        l = jnp.sum(l_ref[...], axis=0, keepdims=True)
        rows = pl.ds(pl.multiple_of(i * tq, tq), tq)
        o_ref[0, rows, :] = (acc_ref[...] / l).T.astype(o_ref.dtype)

    def tile_step(i, slot, m8, with_next):
        s_cur, s_nxt = s_refs[slot], s_refs[1 - slot]
        m_next = jnp.full((SUBLANES, tq), -jnp.inf, F32)
        if with_next:
            qt_next = qt_ref[1 - slot]
        m = jnp.max(m8, axis=0, keepdims=True)
        l8 = jnp.zeros((SUBLANES, tq), F32)
        acc = jnp.zeros((V_HEAD, tq), F32)
        p_prev = None
        for c in range(nkc):
            if with_next:
                m_next = jnp.maximum(m_next, scores(qt_next, c, s_nxt))
            if p_prev is not None:
                acc = acc + jnp.dot(vt_chunk(c - 1), p_prev, preferred_element_type=F32)
            p = jnp.exp2(s_cur[c * kc:(c + 1) * kc, :] - m)
            l8 = l8 + group_reduce(p, jnp.sum)
            p_prev = p.astype(BF16)
            if c == 1:
                finalize(jnp.maximum(i - 1, 0))
            if c == nkc // 2 and with_next:
                qt_ref[slot] = q_tile_t(jnp.minimum(i + 2, nq - 1))
        acc_ref[...] = acc + jnp.dot(vt_chunk(nkc - 1), p_prev, preferred_element_type=F32)
        l_ref[...] = l8
        return m_next

    acc_ref[...] = jnp.zeros_like(acc_ref)
    l_ref[...] = jnp.ones_like(l_ref)
    qt0 = q_tile_t(0)
    qt_ref[1] = q_tile_t(min(1, nq - 1))
    m8 = scores(qt0, 0, s0_ref)
    for c in range(1, nkc):
        m8 = jnp.maximum(m8, scores(qt0, c, s0_ref))

    def step(i, m8):
        return lax.cond(i % 2 == 0,
                        lambda m: tile_step(i, 0, m, True),
                        lambda m: tile_step(i, 1, m, True), m8)

    m8 = lax.fori_loop(0, nq - 1, step, m8)
    tile_step(nq - 1, (nq - 1) % 2, m8, False)
    finalize(nq - 1)


def _attention(q, k, kc, vt, vtc, *, tq):
    nb, nh, seq, dp = q.shape
    ctx_len = kc.shape[2]
    t_all = seq + ctx_len
    chunk = int(np.gcd(np.gcd(MXU_DIM, seq), ctx_len))
    head = lambda b, h: (b, h, 0, 0)
    return pl.pallas_call(
        functools.partial(_attn_kernel, tq=tq, kc=chunk),
        grid=(nb, nh),
        scratch_shapes=[pltpu.VMEM((t_all, tq), F32), pltpu.VMEM((t_all, tq), F32),
                        pltpu.VMEM((2, dp, tq), BF16), pltpu.VMEM((V_HEAD, tq), F32), pltpu.VMEM((SUBLANES, tq), F32)],
        in_specs=[
            pl.BlockSpec((1, 1, seq, dp), head),
            pl.BlockSpec((1, 1, seq, dp), head),
            pl.BlockSpec((1, 1, ctx_len, dp), head),
            pl.BlockSpec((1, 1, V_HEAD, seq), head),
            pl.BlockSpec((1, 1, V_HEAD, ctx_len), head),
        ],
        out_specs=pl.BlockSpec((1, seq, V_HEAD), lambda b, h: (b, 0, h)),
        out_shape=jax.ShapeDtypeStruct((nb, seq, nh * V_HEAD), BF16),
        compiler_params=_cparams(("arbitrary", "arbitrary")),
        name="mla_attention",
    )(q, k, kc, vt, vtc)


def _rope_cols(w):
    qd = QK_ROPE // 4
    r0, r1, r2, r3 = (w[..., i * qd:(i + 1) * qd] for i in range(4))
    return jnp.concatenate([r0, r2, r1, r3], axis=-1), jnp.concatenate([-r1, -r3, r0, r2], axis=-1)


def _rope_tables(seq):
    rows = seq // GRID_W
    row_ids = np.repeat(np.arange(rows, dtype=np.float32), GRID_W)
    col_ids = np.tile(np.arange(GRID_W, dtype=np.float32), rows)
    axis_dim = QK_ROPE // 2
    expo = (np.arange(0, axis_dim, 2, dtype=np.float32) / np.float32(axis_dim)).astype(np.float32)
    inv_freq = (np.float32(1.0) / np.power(np.float32(ROPE_THETA), expo)).astype(np.float32)
    ang_r = (row_ids[:, None] * inv_freq).astype(np.float32)
    ang_c = (col_ids[:, None] * inv_freq).astype(np.float32)
    reps = 128 // (QK_ROPE // 2)
    cos_t = np.tile(np.concatenate([np.cos(ang_r), np.cos(ang_c)], axis=1), (1, reps))
    sin_t = np.tile(np.concatenate([np.sin(ang_r), np.sin(ang_c)], axis=1), (1, reps))
    return jnp.asarray(cos_t, F32), jnp.asarray(sin_t, F32)


def kernel(x, c, ctx, c_ctx, ada_w, ada_b, norm_mix_g, norm_ffn_g, ffn_w_in, ffn_w_out, lru_w_in, lru_conv_w,
           lru_conv_b, lru_gate_w, lru_gate_b, lru_lambda, lru_w_out, mla_w_in, mla_q_norm_g, mla_kv_norm_g,
           mla_w_uq, mla_w_ukv, mla_w_o, final_norm_g):
    nb, seq, d = x.shape
    ctx_len = ctx.shape[1]
    depth = ada_w.shape[0]
    assert depth == 2 and d == D_MODEL and nb + 1 <= MOD_ROWS
    n_ctx_rows = nb * ctx_len
    tile = lambda cap: int(np.gcd(np.gcd(cap, n_ctx_rows), seq))

    ctx2d = ctx.reshape(n_ctx_rows, d)
    x2d = x.reshape(nb * seq, d)

    cv = jnp.concatenate([c, c_ctx[None, :], jnp.zeros((MOD_ROWS - nb - 1, d), F32)], axis=0)
    mods = _ada_mods(cv, ada_w, ada_b).reshape(depth, MOD_ROWS, 6, d)

    tm = tile(1024)
    gg, u = _lru_in(ctx2d, x2d, norm_mix_g[0][None], mods[0], lru_w_in[0].astype(BF16), nb=nb, seq=seq, tm=tm)
    tc = int(np.gcd(np.gcd(256, ctx_len), seq))
    gw = lru_gate_w[0]
    gw = (0.5 * jnp.concatenate([gw[:, 0], gw[:, 1]], axis=-1)).astype(BF16)
    x2, ffn_wi, ffn_wd = _lru_scans_ffn(
        u, lru_conv_w[0], lru_conv_b[0][None], gw, 0.5 * lru_gate_b[0], lru_lambda[0][:, None, :], gg, ctx2d, x2d,
        mods[0], lru_w_out[0].astype(BF16), norm_ffn_g[0][None], ffn_w_in, ffn_w_out,
        layer=0, nb=nb, ctx_len=ctx_len, seq=seq, tc=tc)
    tf = tile(512)

    w_in = mla_w_in[0]
    win_p = jnp.concatenate((w_in[:, :Q_LORA + KV_LORA],) + _rope_cols(w_in[:, Q_LORA + KV_LORA:]),
                            axis=1).astype(BF16)
    wq3 = mla_w_uq[0].reshape(Q_LORA, MLA_HEADS, QK_NOPE + QK_ROPE)
    wq_all = jnp.concatenate([part.reshape(Q_LORA, -1)
                              for part in (wq3[:, :, :QK_NOPE],) + _rope_cols(wq3[:, :, QK_NOPE:])],
                             axis=1).astype(BF16)
    cos_t, sin_t = _rope_tables(seq)
    wkv3 = mla_w_ukv[0].reshape(KV_LORA, MLA_HEADS, QK_NOPE + V_HEAD)
    wk = wkv3[:, :, :QK_NOPE].reshape(KV_LORA, MLA_HEADS * QK_NOPE).astype(BF16)
    wvt = wkv3[:, :, QK_NOPE:].reshape(KV_LORA, MLA_HEADS * V_HEAD).T.astype(BF16)
    proj_args = (x2, norm_mix_g[1][None], mods[1], win_p, mla_kv_norm_g[0][None], wk, wvt)
    kc, vtc = _mla_proj(*proj_args, nb=nb, ctx_len=ctx_len, seq=seq, tm=int(np.gcd(256, ctx_len)))
    k, vt, q = _mla_proj(*proj_args, nb=nb, ctx_len=ctx_len, seq=seq, tm=tile(1024),
                         latent_args=(mla_q_norm_g[0][None], wq_all, cos_t, sin_t))
    o = _attention(q, k, kc, vt, vtc, tq=int(np.gcd(512, seq)))
    out = _ffn(x2, mods[1], norm_ffn_g[1][None], ffn_wi, ffn_wd, final_norm_g[None], layer=1,
               nb=nb, n_ctx_rows=n_ctx_rows, seq=seq, tm=tf, lat_only=True,
               attn=(o.reshape(nb * seq, MLA_HEADS * V_HEAD), mla_w_o[0].astype(BF16)), final_norm=True)
    return out.reshape(nb, seq, d)
```

```python
import functools

import numpy as np
import jax
import jax.numpy as jnp
from jax import lax
from jax.experimental import pallas as pl
from jax.experimental.pallas import tpu as pltpu

F32 = jnp.float32
BF16 = jnp.bfloat16

D_MODEL = 1024
GRID_W = 64
NORM_EPS = 1e-6
LRU_WIDTH = D_MODEL
LRU_BLOCK = 256
LRU_BLOCKS = LRU_WIDTH // LRU_BLOCK
CONV_W = 4
RG_C = 8.0
MLA_HEADS = 8
Q_LORA = 384
KV_LORA = 256
QK_NOPE = 128
QK_ROPE = 64
V_HEAD = 128
SM_SCALE = (QK_NOPE + QK_ROPE) ** -0.5
LOG2_E = float(np.log2(np.e))
Q_PRESCALE = SM_SCALE * LOG2_E
ROPE_THETA = 10000.0

MOD_ROWS = 8
SUBLANES = 8
BF16_ROWS = 16
MXU_DIM = 256
HEAD_PAD = 256
VMEM_LIMIT_MB = 56
N_SUB_FFN = 2
FFN_CHUNKS = 2
FFN_LAG = 2
TINY = 1e-30


def _cparams(sem, vmem_mb=VMEM_LIMIT_MB):
    return pltpu.CompilerParams(dimension_semantics=sem, vmem_limit_bytes=vmem_mb << 20)


def _const_spec(shape):
    nd = len(shape)
    return pl.BlockSpec(shape, lambda *_: (0,) * nd)


def _norm_mod(x, g, shift, scale):
    ms = jnp.mean(x * x, axis=-1, keepdims=True)
    y = (x * lax.rsqrt(ms + NORM_EPS)) * g
    return y * (1.0 + scale) + shift


def _rmsnorm(x, g):
    ms = jnp.mean(x * x, axis=-1, keepdims=True)
    return (x * lax.rsqrt(ms + NORM_EPS)) * g


def _ada_kernel(cv_ref, w_ref, b_ref, o_ref):
    cv = cv_ref[...]
    s = cv * jax.nn.sigmoid(cv)
    o_ref[0] = jnp.dot(s.astype(BF16), w_ref[0].astype(BF16), preferred_element_type=F32) + b_ref[0]


def _ada_mods(cv, ada_w, ada_b):
    depth, d, n = ada_w.shape
    tn = 1536
    return pl.pallas_call(
        _ada_kernel,
        grid=(depth, n // tn),
        in_specs=[
            pl.BlockSpec((MOD_ROWS, d), lambda l, j: (0, 0)),
            pl.BlockSpec((1, d, tn), lambda l, j: (l, 0, j)),
            pl.BlockSpec((1, 1, tn), lambda l, j: (l, 0, j)),
        ],
        out_specs=pl.BlockSpec((1, MOD_ROWS, tn), lambda l, j: (l, 0, j)),
        out_shape=jax.ShapeDtypeStruct((depth, MOD_ROWS, n), F32),
        compiler_params=_cparams(("arbitrary", "arbitrary")),
        name="ada_mods",
    )(cv, ada_w, ada_b.reshape(depth, 1, n))


def _lru_in_kernel(ctx_ref, x_ref, g_ref, mod_ref, w_ref, gg_ref, u_ref, *, nct, n_sub):
    is_ctx = pl.program_id(0) < nct
    ts = x_ref.shape[0] // n_sub
    subs = [slice(i * ts, (i + 1) * ts) for i in range(n_sub)]
    hs = [_norm_mod(jnp.where(is_ctx, ctx_ref[sl, :], x_ref[sl, :]), g_ref[...],
                    mod_ref[0, 0:1, :], mod_ref[0, 1:2, :]).astype(BF16) for sl in subs]
    ys = [jnp.dot(h, w_ref[...], preferred_element_type=F32) for h in hs]
    r = LRU_WIDTH
    for sl, y in zip(subs, ys):
        gg_ref[sl, :] = jax.nn.gelu(y[:, :r])
        u_ref[sl, :] = y[:, r:]


def _lru_in(ctx2d, x2d, g, mods, w, *, nb, seq, tm):
    d = x2d.shape[1]
    n = ctx2d.shape[0] + x2d.shape[0]
    r = LRU_WIDTH
    nct = ctx2d.shape[0] // tm
    tpb = seq // tm

    def mod_map(i):
        return (jnp.where(i < nct, nb, (i - nct) // tpb), 0, 0)

    return pl.pallas_call(
        functools.partial(_lru_in_kernel, nct=nct, n_sub=max(1, tm // MXU_DIM)),
        grid=(n // tm,),
        in_specs=[
            pl.BlockSpec((tm, d), lambda i: (jnp.minimum(i, nct - 1), 0)),
            pl.BlockSpec((tm, d), lambda i: (jnp.maximum(i - nct, 0), 0)),
            _const_spec((1, d)),
            pl.BlockSpec((1, 6, d), mod_map),
            _const_spec((d, 2 * r)),
        ],
        out_specs=[pl.BlockSpec((tm, r), lambda i: (i, 0)), pl.BlockSpec((tm, r), lambda i: (i, 0))],
        out_shape=[jax.ShapeDtypeStruct((n, r), F32), jax.ShapeDtypeStruct((n, r), F32)],
        compiler_params=_cparams(("arbitrary",)),
        name="lru_in",
    )(ctx2d, x2d, g, mods, w)


def _lru_coeffs(xc, gw, gbh, lam, a_s, b_s, between=None):
    xcb = xc.astype(BF16)
    neg = -lam[...]
    sp = jnp.maximum(neg, 0.0) + jnp.log1p(jnp.exp(-jnp.abs(neg)))
    c2 = (-0.5 * RG_C * LOG2_E) * sp
    for n in range(LRU_BLOCKS):
        sl = slice(n * LRU_BLOCK, (n + 1) * LRU_BLOCK)
        pre = jnp.dot(xcb[:, sl], gw[n], preferred_element_type=F32)
        t_r = jnp.tanh(pre[:, :LRU_BLOCK] + gbh[0:1, sl])
        t_i = jnp.tanh(pre[:, LRU_BLOCK:] + gbh[1:2, sl])
        a = jnp.exp2(c2[:, sl] * t_r + c2[:, sl])
        gap = 1.0 - a * a
        mult = gap * lax.rsqrt(jnp.maximum(gap, TINY))
        a_s[:, sl] = a
        b_s[:, sl] = (mult * (0.5 * t_i + 0.5)) * xc[:, sl]
        if between is not None:
            between()


def _lru_chunk_scan(a_s, b_s, h_s, hc, *, reverse, tc, between=None):
    r = LRU_WIDTH
    row = lax.broadcasted_iota(jnp.int32, (SUBLANES, r), 0)
    ngroups = tc // SUBLANES

    def group(gi, hprev):
        g0 = (ngroups - 1 - gi) if reverse else gi
        start = g0 * SUBLANES
        rows = pl.ds(start if isinstance(start, int) else pl.multiple_of(start, SUBLANES), SUBLANES)
        av = a_s[rows, :]
        bv = b_s[rows, :]
        for s in (1, 2, 4):
            if reverse:
                keep = row < (SUBLANES - s)
                shift = SUBLANES - s
            else:
                keep = row >= s
                shift = s
            a_sh = jnp.where(keep, pltpu.roll(av, shift, 0), 1.0)
            b_sh = jnp.where(keep, pltpu.roll(bv, shift, 0), 0.0)
            bv = av * b_sh + bv
            av = av * a_sh
        hrows = av * hprev + bv
        h_s[rows, :] = hrows
        edge = hrows[0:1, :] if reverse else hrows[SUBLANES - 1:SUBLANES, :]
        return jnp.broadcast_to(edge, (SUBLANES, r))

    if between is None:
        hc[...] = lax.fori_loop(0, ngroups, group, hc[...], unroll=4)
    else:
        h = hc[...]
        for gi in range(ngroups):
            h = group(gi, h)
            between(gi)
        hc[...] = h


def _scan_fwd_kernel(ucur, uprev, unext, cw, cb, gw, gbh, lam, wi32, wd32, hf, xc_out, wi16, wd16,
                     ext, a_s, b_s, hc, *, tc, nc, nl):
    col = lax.broadcasted_iota(jnp.int32, (1, wi32.shape[1]), 1)
    wi16[...] = (wi32[...] * jnp.where(col < wi32.shape[1] // 2, 0.5, 1.0)).astype(BF16)
    wd16[...] = wd32[...].astype(BF16)
    j = pl.program_id(1)
    in_ctx = j < nc
    pos = jnp.where(in_ctx, j, j - nc)
    first = pos == 0
    last = pos == jnp.where(in_ctx, nc, nl) - 1

    @pl.when(j == 0)
    def _():
        hc[...] = jnp.zeros_like(hc)

    n_ext = tc + 2 * SUBLANES
    ext[0:SUBLANES, :] = jnp.where(first, 0.0, uprev[...])
    ext[SUBLANES:SUBLANES + tc, :] = ucur[...]
    ext[SUBLANES + tc:n_ext, :] = jnp.where(last, 0.0, unext[...])
    e = ext[...]
    mid = slice(SUBLANES, SUBLANES + tc)
    xc = cb[...] + cw[0:1, :] * pltpu.roll(e, 1, 0)[mid] + cw[1:2, :] * e[mid] \
        + cw[2:3, :] * pltpu.roll(e, n_ext - 1, 0)[mid] + cw[3:4, :] * pltpu.roll(e, n_ext - 2, 0)[mid]
    xc_out[...] = xc
    _lru_coeffs(xc, gw, gbh, lam, a_s, b_s)
    _lru_chunk_scan(a_s, b_s, hf, hc, reverse=False, tc=tc)


def _scan_bwd_ffn_kernel(xc, gw, gbh, lam, hf, gg, ctx_res, x_res, mod, wout, mod_prev, fg, wi, wd,
                         out, a_s, b_s, hc, hb_s, x1_prev, *, tc, nc, per_batch, nsteps):
    s = pl.program_id(0)
    j = jnp.minimum(s, nsteps - 1) % per_batch

    @pl.when(s == 0)
    def _():
        x1_prev[...] = jnp.zeros_like(x1_prev)

    @pl.when(j == 0)
    def _():
        hc[...] = jnp.zeros_like(hc)

    pieces, ffn_result = _ffn_pieces(x1_prev[...], mod_prev, fg, wi, wd, n_chunks=wd.shape[1] // MXU_DIM)
    todo = list(pieces)
    run_next = lambda: todo.pop(0)() if todo else None
    run_two = lambda: (run_next(), run_next())
    run_two()
    _lru_coeffs(xc[...], gw, gbh, lam, a_s, b_s, between=run_two)
    ngroups = tc // SUBLANES
    n_scan = len(todo)

    def between(gi):
        while todo and (n_scan - len(todo)) * ngroups < (gi + 1) * n_scan:
            run_next()

    _lru_chunk_scan(a_s, b_s, hb_s, hc, reverse=True, tc=tc, between=between)
    out[...] = ffn_result()
    z = ((hf[...] + hb_s[...]) * gg[...]).astype(BF16)
    res = jnp.where(j < nc, ctx_res[...], x_res[...])
    x1_prev[...] = res + mod[0, 2:3, :] * jnp.dot(z, wout[...], preferred_element_type=F32)


def _lru_scans_ffn(u, conv_w, conv_b, gw, gbh, lam, gg, ctx2d, x2d, mods, wout, ffn_g, ffn_wi32, ffn_wd32, *,
                   layer, nb, ctx_len, seq, tc):
    n, r = u.shape
    d = x2d.shape[1]
    nc = ctx_len // tc
    nl = seq // tc
    blk8 = tc // SUBLANES

    def cur_blk(b, j, reverse):
        jc = (nc - 1 - j) if reverse else j
        jl = (nl - 1 - (j - nc)) if reverse else (j - nc)
        return jnp.where(j < nc, b * nc + jc, nb * nc + b * nl + jl)

    fwd_map = lambda b, j: (cur_blk(b, j, False), 0)
    bwd_map = lambda b, j: (cur_blk(b, j, True), 0)
    prev_map = lambda b, j: (jnp.maximum(cur_blk(b, j, False) * blk8 - 1, 0), 0)
    next_map = lambda b, j: (jnp.minimum((cur_blk(b, j, False) + 1) * blk8, n // SUBLANES - 1), 0)
    gate_specs = [
        _const_spec((LRU_BLOCKS, LRU_BLOCK, 2 * LRU_BLOCK)),
        _const_spec((2, r)),
        _const_spec((1, r)),
    ]
    chunk = pltpu.VMEM((tc, r), F32)
    carry = pltpu.VMEM((SUBLANES, r), F32)
    per_batch = nc + nl
    nsteps = nb * per_batch
    depth, hid = ffn_wd32.shape[:2]
    wi32 = ffn_wi32.reshape(depth * d, 2 * hid)
    wd32 = ffn_wd32.reshape(depth * hid, d)

    def slab_spec(w):
        rows = w.shape[0]
        blk = next(b for b in range(BF16_ROWS, rows + 1, BF16_ROWS) if rows % b == 0 and rows // b <= nsteps)
        return pl.BlockSpec((blk, w.shape[1]), lambda b, j: (jnp.minimum(b * per_batch + j, rows // blk - 1), 0))

    hf, xc, ffn_wi, ffn_wd = pl.pallas_call(
        functools.partial(_scan_fwd_kernel, tc=tc, nc=nc, nl=nl),
        grid=(nb, nc + nl),
        in_specs=[
            pl.BlockSpec((tc, r), fwd_map),
            pl.BlockSpec((SUBLANES, r), prev_map),
            pl.BlockSpec((SUBLANES, r), next_map),
            _const_spec((CONV_W, r)),
            _const_spec((1, r)),
        ] + gate_specs + [slab_spec(wi32), slab_spec(wd32)],
        out_specs=[pl.BlockSpec((tc, r), fwd_map), pl.BlockSpec((tc, r), fwd_map), slab_spec(wi32), slab_spec(wd32)],
        out_shape=[jax.ShapeDtypeStruct((n, r), F32), jax.ShapeDtypeStruct((n, r), F32),
                   jax.ShapeDtypeStruct(wi32.shape, BF16), jax.ShapeDtypeStruct(wd32.shape, BF16)],
        scratch_shapes=[pltpu.VMEM((tc + 2 * SUBLANES, r), F32), chunk, chunk, carry],
        compiler_params=_cparams(("arbitrary", "arbitrary")),
        name="lru_scan_fwd",
    )(u, u, u, conv_w, conv_b, gw[0], gbh[0], lam[0], wi32, wd32)
    ffn_wi = ffn_wi.reshape(depth, d, 2 * hid)
    ffn_wd = ffn_wd.reshape(depth, hid, d)

    def at(step_map, lag):
        def index_map(s):
            sc = jnp.clip(s - lag, 0, nsteps - 1)
            return step_map(sc // per_batch, sc % per_batch)
        return index_map

    ctx_res_map = lambda b, j: (b * nc + jnp.maximum(nc - 1 - j, 0), 0)
    x_res_map = lambda b, j: (b * nl + nl - 1 - jnp.maximum(j - nc, 0), 0)
    mod_map = lambda b, j: (jnp.where(j < nc, nb, b), 0, 0)
    x2 = pl.pallas_call(
        functools.partial(_scan_bwd_ffn_kernel, tc=tc, nc=nc, per_batch=per_batch, nsteps=nsteps),
        grid=(nsteps + 1,),
        in_specs=[pl.BlockSpec((tc, r), at(bwd_map, 0))] + gate_specs + [
            pl.BlockSpec((tc, r), at(bwd_map, 0)),
            pl.BlockSpec((tc, r), at(bwd_map, 0)),
            pl.BlockSpec((tc, d), at(ctx_res_map, 0)),
            pl.BlockSpec((tc, d), at(x_res_map, 0)),
            pl.BlockSpec((1, 6, d), at(mod_map, 0)),
            _const_spec((r, d)),
            pl.BlockSpec((1, 6, d), at(mod_map, 1)),
            _const_spec((1, d)),
            pl.BlockSpec((1, d, 2 * hid), lambda s: (layer, 0, 0), pipeline_mode=pl.Buffered(1)),
            pl.BlockSpec((1, hid, d), lambda s: (layer, 0, 0), pipeline_mode=pl.Buffered(1)),
        ],
        out_specs=pl.BlockSpec((tc, d), at(bwd_map, 1)),
        out_shape=jax.ShapeDtypeStruct((n, d), F32),
        scratch_shapes=[chunk, chunk, carry, chunk, pltpu.VMEM((tc, d), F32)],
        compiler_params=_cparams(("arbitrary",)),
        name="lru_scan_bwd_ffn",
    )(xc, gw[1], gbh[1], lam[1], hf, gg, ctx2d, x2d, mods, wout, mods, ffn_g, ffn_wi, ffn_wd)
    return x2, ffn_wi, ffn_wd


def _ffn_pieces(x, mod_ref, g_ref, wi_ref, wd_ref, *, n_chunks):
    h = _norm_mod(x, g_ref[...], mod_ref[0, 3:4, :], mod_ref[0, 4:5, :]).astype(BF16)
    hid = wd_ref.shape[1]
    n_mxu = hid // MXU_DIM
    edges = [round(c * n_mxu / n_chunks) * MXU_DIM for c in range(n_chunks)] + [hid]
    acc = [jnp.zeros(x.shape, F32)]
    pending = []

    def down():
        act, e0, e1 = pending.pop(0)
        acc[0] = acc[0] + jnp.dot(act, wd_ref[0, e0:e1, :], preferred_element_type=F32)

    def piece(e0, e1):
        def run():
            half_gate = jnp.dot(h, wi_ref[0, :, e0:e1], preferred_element_type=F32)
            up = jnp.dot(h, wi_ref[0, :, hid + e0:hid + e1], preferred_element_type=F32)
            if len(pending) >= FFN_LAG:
                down()
            pending.append(((((jnp.tanh(half_gate) + 1.0) * half_gate) * up).astype(BF16), e0, e1))
        return run

    def result():
        while pending:
            down()
        return x + mod_ref[0, 5:6, :] * acc[0]

    return [piece(e0, e1) for e0, e1 in zip(edges[:-1], edges[1:])], result


def _attn_out_ffn_kernel(x_ref, o_ref, wo_ref, mod_ref, g_ref, wi_ref, wd_ref, fg_ref, out_ref):
    ts = x_ref.shape[0] // N_SUB_FFN
    subs = [slice(i * ts, (i + 1) * ts) for i in range(N_SUB_FFN)]
    xs = [x_ref[sl, :] + mod_ref[0, 2:3, :] * jnp.dot(o_ref[sl, :], wo_ref[...], preferred_element_type=F32)
          for sl in subs]
    ffns = [_ffn_pieces(x, mod_ref, g_ref, wi_ref, wd_ref, n_chunks=FFN_CHUNKS) for x in xs]
    for c in range(FFN_CHUNKS):
        for pieces, _ in ffns:
            pieces[c]()
    for sl, (_, result) in zip(subs, ffns):
        out_ref[sl, :] = _rmsnorm(result(), fg_ref[...])


def _attn_out_ffn(xall, o, wo, mods, g, wi, wd, fg, *, layer, n_ctx_rows, seq, tm):
    n, d = xall.shape
    hid = wd.shape[1]
    off = n_ctx_rows // tm
    tpb = seq // tm
    return pl.pallas_call(
        _attn_out_ffn_kernel,
        grid=((n - n_ctx_rows) // tm,),
        in_specs=[
            pl.BlockSpec((tm, d), lambda i: (i + off, 0)),
            pl.BlockSpec((tm, o.shape[1]), lambda i: (i, 0)),
            _const_spec(wo.shape),
            pl.BlockSpec((1, 6, d), lambda i: (i // tpb, 0, 0)),
            _const_spec((1, d)),
            pl.BlockSpec((1, d, 2 * hid), lambda i: (layer, 0, 0), pipeline_mode=pl.Buffered(1)),
            pl.BlockSpec((1, hid, d), lambda i: (layer, 0, 0), pipeline_mode=pl.Buffered(1)),
            _const_spec((1, d)),
        ],
        out_specs=pl.BlockSpec((tm, d), lambda i: (i, 0)),
        out_shape=jax.ShapeDtypeStruct((n - n_ctx_rows, d), F32),
        compiler_params=_cparams(("arbitrary",)),
        name="attn_out_ffn",
    )(xall, o, wo, mods, g, wi, wd, fg)


def _mla_proj_kernel(*refs, latent, n_sub):
    if latent:
        (x_ref, g_ref, mod_ref, win_ref, kvg_ref, wk_ref, wvt_ref, qg_ref, wq_ref, cos_ref, sin_ref,
         k_ref, v_ref, q_ref) = refs
    else:
        x_ref, g_ref, mod_ref, win_ref, kvg_ref, wk_ref, wvt_ref, k_ref, v_ref = refs
    ts = x_ref.shape[0] // n_sub
    subs = [slice(i * ts, (i + 1) * ts) for i in range(n_sub)]
    kv_lo = Q_LORA + KV_LORA
    lane = lax.broadcasted_iota(jnp.int32, (1, 128), 1)
    low = lane < QK_ROPE
    hs = [_norm_mod(x_ref[sl, :], g_ref[...], mod_ref[0, 0:1, :], mod_ref[0, 1:2, :]).astype(BF16) for sl in subs]
    c_alls = [jnp.dot(h, win_ref[...], preferred_element_type=F32) for h in hs]
    for sl, c_all in zip(subs, c_alls):
        cosv, sinv = (cos_ref[sl, :], sin_ref[sl, :]) if latent else (1.0, 0.0)
        kt = c_all[:, kv_lo:kv_lo + 128] * jnp.where(low, cosv, sinv)
        kt = kt + pltpu.roll(kt, QK_ROPE, 1)
        k_rope = (jnp.where(low, kt, 0.0).astype(BF16), jnp.where(low, 0.0, kt).astype(BF16))
        ckv = _rmsnorm(c_all[:, Q_LORA:kv_lo], kvg_ref[...])
        k_nope = jnp.dot(ckv.astype(BF16), wk_ref[...], preferred_element_type=F32)
        v_t = jnp.dot(wvt_ref[...], ckv.T.astype(BF16), preferred_element_type=F32)
        for hd in range(MLA_HEADS):
            hl = slice(hd * 128, (hd + 1) * 128)
            k_ref[0, hd, sl, 0:QK_NOPE] = k_nope[:, hl].astype(BF16)
            k_ref[0, hd, sl, QK_NOPE:HEAD_PAD] = k_rope[hd % 2]
            v_ref[0, hd, :, sl] = v_t[hl, :].astype(BF16)
    if latent:
        for sl, c_all in zip(subs, c_alls):
            cq = _rmsnorm(c_all[:, :Q_LORA], qg_ref[...]).astype(BF16)
            q_all = jnp.dot(cq, wq_ref[...], preferred_element_type=F32) * Q_PRESCALE
            hw = MLA_HEADS * QK_NOPE
            hr = MLA_HEADS * QK_ROPE
            for pair in range(MLA_HEADS // 2):
                ps = slice(hw + pair * 128, hw + (pair + 1) * 128)
                ss = slice(hw + hr + pair * 128, hw + hr + (pair + 1) * 128)
                q_rope = (q_all[:, ps] * cos_ref[sl, :] + q_all[:, ss] * sin_ref[sl, :]).astype(BF16)
                for hd in (2 * pair, 2 * pair + 1):
                    q_ref[0, hd, sl, 0:QK_NOPE] = q_all[:, hd * 128:(hd + 1) * 128].astype(BF16)
                    q_ref[0, hd, sl, QK_NOPE:HEAD_PAD] = q_rope


def _mla_proj(xall, g, mods, win, kvg, wk, wvt, *, nb, ctx_len, seq, tm, latent_args=None):
    n, d = xall.shape
    latent = latent_args is not None
    rows = seq if latent else ctx_len
    off = nb * ctx_len // tm if latent else 0
    tpb = rows // tm
    kv_map = lambda i: (i // tpb, 0, i % tpb, 0)
    vt_map = lambda i: (i // tpb, 0, 0, i % tpb)
    in_specs = [
        pl.BlockSpec((tm, d), lambda i: (i + off, 0)),
        _const_spec((1, d)),
        pl.BlockSpec((1, 6, d), lambda i: ((i // tpb) if latent else nb, 0, 0)),
        _const_spec(win.shape),
        _const_spec((1, KV_LORA)),
        _const_spec(wk.shape),
        _const_spec(wvt.shape),
    ]
    args = [xall, g, mods, win, kvg, wk, wvt]
    out_specs = [pl.BlockSpec((1, MLA_HEADS, tm, HEAD_PAD), kv_map), pl.BlockSpec((1, MLA_HEADS, V_HEAD, tm), vt_map)]
    out_shape = [jax.ShapeDtypeStruct((nb, MLA_HEADS, rows, HEAD_PAD), BF16),
                 jax.ShapeDtypeStruct((nb, MLA_HEADS, V_HEAD, rows), BF16)]
    if latent:
        qg, wq, cos_t, sin_t = latent_args
        tab_map = lambda i: (i % tpb, 0)
        in_specs += [_const_spec((1, Q_LORA)), _const_spec(wq.shape),
                     pl.BlockSpec((tm, 128), tab_map), pl.BlockSpec((tm, 128), tab_map)]
        args += [qg, wq, cos_t, sin_t]
        out_specs.append(pl.BlockSpec((1, MLA_HEADS, tm, HEAD_PAD), kv_map))
        out_shape.append(jax.ShapeDtypeStruct((nb, MLA_HEADS, rows, HEAD_PAD), BF16))
    return pl.pallas_call(
        functools.partial(_mla_proj_kernel, latent=latent, n_sub=max(1, tm // MXU_DIM)),
        grid=(nb * tpb,),
        in_specs=in_specs,
        out_specs=out_specs,
        out_shape=out_shape,
        compiler_params=_cparams(("arbitrary",)),
        name="mla_proj_lat" if latent else "mla_proj_ctx",
    )(*args)


def _attn_kernel(q_ref, k_ref, kc_ref, vt_ref, vtc_ref, o_ref, s0_ref, s1_ref, qt_ref, acc_ref, l_ref, *, tq, kc):
    n_lat = k_ref.shape[2]
    t_all = n_lat + kc_ref.shape[2]
    nkc = t_all // kc
    nq = q_ref.shape[2] // tq
    s_refs = (s0_ref, s1_ref)

    def group_reduce(x, op):
        return op(x.reshape(x.shape[0] // SUBLANES, SUBLANES, tq), axis=0)

    def k_chunk(c):
        if c * kc < n_lat:
            return k_ref[0, 0, c * kc:(c + 1) * kc, :]
        return kc_ref[0, 0, c * kc - n_lat:(c + 1) * kc - n_lat, :]

    def vt_chunk(c):
        if c * kc < n_lat:
            return vt_ref[0, 0, :, c * kc:(c + 1) * kc]
        return vtc_ref[0, 0, :, c * kc - n_lat:(c + 1) * kc - n_lat]

    def scores(qt, c, s_ref):
        st = jnp.dot(k_chunk(c), qt, preferred_element_type=F32)
        s_ref[c * kc:(c + 1) * kc, :] = st
        return group_reduce(st, jnp.max)

    def q_tile_t(i):
        q = q_ref[0, 0, pl.ds(pl.multiple_of(i * tq, tq), tq), :]
        return q.astype(F32).T.astype(BF16)

    def finalize(i):
        l = jnp.sum(l_ref[...], axis=0, keepdims=True)
        rows = pl.ds(pl.multiple_of(i * tq, tq), tq)
        o_ref[0, rows, :] = (acc_ref[...] / l).T.astype(o_ref.dtype)

    def tile_step(i, slot, m8, with_next):
        s_cur, s_nxt = s_refs[slot], s_refs[1 - slot]
        m_next = jnp.full((SUBLANES, tq), -jnp.inf, F32)
        if with_next:
            qt_next = qt_ref[1 - slot]
        m = jnp.max(m8, axis=0, keepdims=True)
        l8 = jnp.zeros((SUBLANES, tq), F32)
        acc = jnp.zeros((V_HEAD, tq), F32)
        p_prev = None
        for c in range(nkc):
            if with_next:
                m_next = jnp.maximum(m_next, scores(qt_next, c, s_nxt))
            if p_prev is not None:
                acc = acc + jnp.dot(vt_chunk(c - 1), p_prev, preferred_element_type=F32)
            p = jnp.exp2(s_cur[c * kc:(c + 1) * kc, :] - m)
            l8 = l8 + group_reduce(p, jnp.sum)
            p_prev = p.astype(BF16)
            if c == 1:
                finalize(jnp.maximum(i - 1, 0))
            if c == nkc // 2 and with_next:
                qt_ref[slot] = q_tile_t(jnp.minimum(i + 2, nq - 1))
        acc_ref[...] = acc + jnp.dot(vt_chunk(nkc - 1), p_prev, preferred_element_type=F32)
        l_ref[...] = l8
        return m_next

    acc_ref[...] = jnp.zeros_like(acc_ref)
    l_ref[...] = jnp.ones_like(l_ref)
    qt0 = q_tile_t(0)
    qt_ref[1] = q_tile_t(min(1, nq - 1))
    m8 = scores(qt0, 0, s0_ref)
    for c in range(1, nkc):
        m8 = jnp.maximum(m8, scores(qt0, c, s0_ref))

    def step(i, m8):
        return lax.cond(i % 2 == 0,
                        lambda m: tile_step(i, 0, m, True),
                        lambda m: tile_step(i, 1, m, True), m8)

    m8 = lax.fori_loop(0, nq - 1, step, m8)
    tile_step(nq - 1, (nq - 1) % 2, m8, False)
    finalize(nq - 1)


def _attention(q, k, kc, vt, vtc, *, tq):
    nb, nh, seq, dp = q.shape
    ctx_len = kc.shape[2]
    t_all = seq + ctx_len
    chunk = int(np.gcd(np.gcd(MXU_DIM, seq), ctx_len))
    head = lambda b, h: (b, h, 0, 0)
    return pl.pallas_call(
        functools.partial(_attn_kernel, tq=tq, kc=chunk),
        grid=(nb, nh),
        scratch_shapes=[pltpu.VMEM((t_all, tq), F32), pltpu.VMEM((t_all, tq), F32),
                        pltpu.VMEM((2, dp, tq), BF16), pltpu.VMEM((V_HEAD, tq), F32), pltpu.VMEM((SUBLANES, tq), F32)],
        in_specs=[
            pl.BlockSpec((1, 1, seq, dp), head),
            pl.BlockSpec((1, 1, seq, dp), head),
            pl.BlockSpec((1, 1, ctx_len, dp), head),
            pl.BlockSpec((1, 1, V_HEAD, seq), head),
            pl.BlockSpec((1, 1, V_HEAD, ctx_len), head),
        ],
        out_specs=pl.BlockSpec((1, seq, V_HEAD), lambda b, h: (b, 0, h)),
        out_shape=jax.ShapeDtypeStruct((nb, seq, nh * V_HEAD), BF16),
        compiler_params=_cparams(("arbitrary", "arbitrary")),
        name="mla_attention",
    )(q, k, kc, vt, vtc)


def _rope_cols(w):
    qd = QK_ROPE // 4
    r0, r1, r2, r3 = (w[..., i * qd:(i + 1) * qd] for i in range(4))
    return jnp.concatenate([r0, r2, r1, r3], axis=-1), jnp.concatenate([-r1, -r3, r0, r2], axis=-1)


def _rope_tables(seq):
    rows = seq // GRID_W
    row_ids = np.repeat(np.arange(rows, dtype=np.float32), GRID_W)
    col_ids = np.tile(np.arange(GRID_W, dtype=np.float32), rows)
    axis_dim = QK_ROPE // 2
    expo = (np.arange(0, axis_dim, 2, dtype=np.float32) / np.float32(axis_dim)).astype(np.float32)
    inv_freq = (np.float32(1.0) / np.power(np.float32(ROPE_THETA), expo)).astype(np.float32)
    ang_r = (row_ids[:, None] * inv_freq).astype(np.float32)
    ang_c = (col_ids[:, None] * inv_freq).astype(np.float32)
    reps = 128 // (QK_ROPE // 2)
    cos_t = np.tile(np.concatenate([np.cos(ang_r), np.cos(ang_c)], axis=1), (1, reps))
    sin_t = np.tile(np.concatenate([np.sin(ang_r), np.sin(ang_c)], axis=1), (1, reps))
    return jnp.asarray(cos_t, F32), jnp.asarray(sin_t, F32)


def kernel(x, c, ctx, c_ctx, ada_w, ada_b, norm_mix_g, norm_ffn_g, ffn_w_in, ffn_w_out, lru_w_in, lru_conv_w,
           lru_conv_b, lru_gate_w, lru_gate_b, lru_lambda, lru_w_out, mla_w_in, mla_q_norm_g, mla_kv_norm_g,
           mla_w_uq, mla_w_ukv, mla_w_o, final_norm_g):
    nb, seq, d = x.shape
    ctx_len = ctx.shape[1]
    depth = ada_w.shape[0]
    assert depth == 2 and d == D_MODEL and nb + 1 <= MOD_ROWS
    n_ctx_rows = nb * ctx_len
    tile = lambda cap: int(np.gcd(np.gcd(cap, n_ctx_rows), seq))

    ctx2d = ctx.reshape(n_ctx_rows, d)
    x2d = x.reshape(nb * seq, d)

    cv = jnp.concatenate([c, c_ctx[None, :], jnp.zeros((MOD_ROWS - nb - 1, d), F32)], axis=0)
    mods = _ada_mods(cv, ada_w, ada_b).reshape(depth, MOD_ROWS, 6, d)

    tm = tile(1024)
    gg, u = _lru_in(ctx2d, x2d, norm_mix_g[0][None], mods[0], lru_w_in[0].astype(BF16), nb=nb, seq=seq, tm=tm)
    tc = int(np.gcd(np.gcd(256, ctx_len), seq))
    gw = lru_gate_w[0]
    gw = (0.5 * jnp.concatenate([gw[:, 0], gw[:, 1]], axis=-1)).astype(BF16)
    x2, ffn_wi, ffn_wd = _lru_scans_ffn(
        u, lru_conv_w[0], lru_conv_b[0][None], gw, 0.5 * lru_gate_b[0], lru_lambda[0][:, None, :], gg, ctx2d, x2d,
        mods[0], lru_w_out[0].astype(BF16), norm_ffn_g[0][None], ffn_w_in, ffn_w_out,
        layer=0, nb=nb, ctx_len=ctx_len, seq=seq, tc=tc)

    w_in = mla_w_in[0]
    win_p = jnp.concatenate((w_in[:, :Q_LORA + KV_LORA],) + _rope_cols(w_in[:, Q_LORA + KV_LORA:]),
                            axis=1).astype(BF16)
    wq3 = mla_w_uq[0].reshape(Q_LORA, MLA_HEADS, QK_NOPE + QK_ROPE)
    wq_all = jnp.concatenate([part.reshape(Q_LORA, -1)
                              for part in (wq3[:, :, :QK_NOPE],) + _rope_cols(wq3[:, :, QK_NOPE:])],
                             axis=1).astype(BF16)
    cos_t, sin_t = _rope_tables(seq)
    wkv3 = mla_w_ukv[0].reshape(KV_LORA, MLA_HEADS, QK_NOPE + V_HEAD)
    wk = wkv3[:, :, :QK_NOPE].reshape(KV_LORA, MLA_HEADS * QK_NOPE).astype(BF16)
    wvt = wkv3[:, :, QK_NOPE:].reshape(KV_LORA, MLA_HEADS * V_HEAD).T.astype(BF16)
    proj_args = (x2, norm_mix_g[1][None], mods[1], win_p, mla_kv_norm_g[0][None], wk, wvt)
    kc, vtc = _mla_proj(*proj_args, nb=nb, ctx_len=ctx_len, seq=seq, tm=int(np.gcd(256, ctx_len)))
    k, vt, q = _mla_proj(*proj_args, nb=nb, ctx_len=ctx_len, seq=seq, tm=tile(1024),
                         latent_args=(mla_q_norm_g[0][None], wq_all, cos_t, sin_t))
    o = _attention(q, k, kc, vt, vtc, tq=int(np.gcd(512, seq)))
    out = _attn_out_ffn(x2, o.reshape(nb * seq, MLA_HEADS * V_HEAD), mla_w_o[0].astype(BF16), mods[1],
                        norm_ffn_g[1][None], ffn_wi, ffn_wd, final_norm_g[None], layer=1,
                        n_ctx_rows=n_ctx_rows, seq=seq, tm=tile(512))
    return out.reshape(nb, seq, d)
```

```python
import functools

import numpy as np
import jax
import jax.numpy as jnp
from jax import lax
from jax.experimental import pallas as pl
from jax.experimental.pallas import tpu as pltpu

F32 = jnp.float32
BF16 = jnp.bfloat16

D_MODEL = 1024
GRID_W = 64
NORM_EPS = 1e-6
LRU_WIDTH = D_MODEL
LRU_BLOCK = 256
LRU_BLOCKS = LRU_WIDTH // LRU_BLOCK
CONV_W = 4
RG_C = 8.0
MLA_HEADS = 8
Q_LORA = 384
KV_LORA = 256
QK_NOPE = 128
QK_ROPE = 64
V_HEAD = 128
SM_SCALE = (QK_NOPE + QK_ROPE) ** -0.5
LOG2_E = float(np.log2(np.e))
Q_PRESCALE = SM_SCALE * LOG2_E
ROPE_THETA = 10000.0

MOD_ROWS = 8
SUBLANES = 8
BF16_ROWS = 16
MXU_DIM = 256
HEAD_PAD = 256
VMEM_LIMIT_MB = 56
N_SUB_FFN = 2
FFN_CHUNKS = 2
FFN_LAG = 2
TINY = 1e-30


def _cparams(sem, vmem_mb=VMEM_LIMIT_MB):
    return pltpu.CompilerParams(dimension_semantics=sem, vmem_limit_bytes=vmem_mb << 20)


def _const_spec(shape):
    nd = len(shape)
    return pl.BlockSpec(shape, lambda *_: (0,) * nd)


def _norm_mod(x, g, shift, scale):
    ms = jnp.mean(x * x, axis=-1, keepdims=True)
    y = (x * lax.rsqrt(ms + NORM_EPS)) * g
    return y * (1.0 + scale) + shift


def _rmsnorm(x, g):
    ms = jnp.mean(x * x, axis=-1, keepdims=True)
    return (x * lax.rsqrt(ms + NORM_EPS)) * g


def _ada_kernel(cv_ref, w_ref, b_ref, o_ref):
    cv = cv_ref[...]
    s = cv * jax.nn.sigmoid(cv)
    o_ref[0] = jnp.dot(s.astype(BF16), w_ref[0].astype(BF16), preferred_element_type=F32) + b_ref[0]


def _ada_mods(cv, ada_w, ada_b):
    depth, d, n = ada_w.shape
    tn = 1536
    return pl.pallas_call(
        _ada_kernel,
        grid=(depth, n // tn),
        in_specs=[
            pl.BlockSpec((MOD_ROWS, d), lambda l, j: (0, 0)),
            pl.BlockSpec((1, d, tn), lambda l, j: (l, 0, j)),
            pl.BlockSpec((1, 1, tn), lambda l, j: (l, 0, j)),
        ],
        out_specs=pl.BlockSpec((1, MOD_ROWS, tn), lambda l, j: (l, 0, j)),
        out_shape=jax.ShapeDtypeStruct((depth, MOD_ROWS, n), F32),
        compiler_params=_cparams(("arbitrary", "arbitrary")),
        name="ada_mods",
    )(cv, ada_w, ada_b.reshape(depth, 1, n))


def _lru_in_kernel(ctx_ref, x_ref, g_ref, mod_ref, w_ref, gg_ref, u_ref, *, nct, n_sub):
    is_ctx = pl.program_id(0) < nct
    ts = x_ref.shape[0] // n_sub
    subs = [slice(i * ts, (i + 1) * ts) for i in range(n_sub)]
    hs = [_norm_mod(jnp.where(is_ctx, ctx_ref[sl, :], x_ref[sl, :]), g_ref[...],
                    mod_ref[0, 0:1, :], mod_ref[0, 1:2, :]).astype(BF16) for sl in subs]
    ys = [jnp.dot(h, w_ref[...], preferred_element_type=F32) for h in hs]
    r = LRU_WIDTH
    for sl, y in zip(subs, ys):
        gg_ref[sl, :] = jax.nn.gelu(y[:, :r])
        u_ref[sl, :] = y[:, r:]


def _lru_in(ctx2d, x2d, g, mods, w, *, nb, seq, tm):
    d = x2d.shape[1]
    n = ctx2d.shape[0] + x2d.shape[0]
    r = LRU_WIDTH
    nct = ctx2d.shape[0] // tm
    tpb = seq // tm

    def mod_map(i):
        return (jnp.where(i < nct, nb, (i - nct) // tpb), 0, 0)

    return pl.pallas_call(
        functools.partial(_lru_in_kernel, nct=nct, n_sub=max(1, tm // MXU_DIM)),
        grid=(n // tm,),
        in_specs=[
            pl.BlockSpec((tm, d), lambda i: (jnp.minimum(i, nct - 1), 0)),
            pl.BlockSpec((tm, d), lambda i: (jnp.maximum(i - nct, 0), 0)),
            _const_spec((1, d)),
            pl.BlockSpec((1, 6, d), mod_map),
            _const_spec((d, 2 * r)),
        ],
        out_specs=[pl.BlockSpec((tm, r), lambda i: (i, 0)), pl.BlockSpec((tm, r), lambda i: (i, 0))],
        out_shape=[jax.ShapeDtypeStruct((n, r), F32), jax.ShapeDtypeStruct((n, r), F32)],
        compiler_params=_cparams(("arbitrary",)),
        name="lru_in",
    )(ctx2d, x2d, g, mods, w)


def _lru_coeffs(xc, gw, gbh, lam, a_s, b_s, between=None):
    xcb = xc.astype(BF16)
    neg = -lam[...]
    sp = jnp.maximum(neg, 0.0) + jnp.log1p(jnp.exp(-jnp.abs(neg)))
    c2 = (-0.5 * RG_C * LOG2_E) * sp
    for n in range(LRU_BLOCKS):
        sl = slice(n * LRU_BLOCK, (n + 1) * LRU_BLOCK)
        pre = jnp.dot(xcb[:, sl], gw[n], preferred_element_type=F32)
        t_r = jnp.tanh(pre[:, :LRU_BLOCK] + gbh[0:1, sl])
        t_i = jnp.tanh(pre[:, LRU_BLOCK:] + gbh[1:2, sl])
        a = jnp.exp2(c2[:, sl] * t_r + c2[:, sl])
        gap = 1.0 - a * a
        mult = gap * lax.rsqrt(jnp.maximum(gap, TINY))
        a_s[:, sl] = a
        b_s[:, sl] = (mult * (0.5 * t_i + 0.5)) * xc[:, sl]
        if between is not None:
            between()


def _lru_chunk_scan(a_s, b_s, h_s, hc, *, reverse, tc, between=None):
    r = LRU_WIDTH
    row = lax.broadcasted_iota(jnp.int32, (SUBLANES, r), 0)
    ngroups = tc // SUBLANES

    def group(gi, hprev):
        g0 = (ngroups - 1 - gi) if reverse else gi
        start = g0 * SUBLANES
        rows = pl.ds(start if isinstance(start, int) else pl.multiple_of(start, SUBLANES), SUBLANES)
        av = a_s[rows, :]
        bv = b_s[rows, :]
        for s in (1, 2, 4):
            if reverse:
                keep = row < (SUBLANES - s)
                shift = SUBLANES - s
            else:
                keep = row >= s
                shift = s
            a_sh = jnp.where(keep, pltpu.roll(av, shift, 0), 1.0)
            b_sh = jnp.where(keep, pltpu.roll(bv, shift, 0), 0.0)
            bv = av * b_sh + bv
            av = av * a_sh
        hrows = av * hprev + bv
        h_s[rows, :] = hrows
        edge = hrows[0:1, :] if reverse else hrows[SUBLANES - 1:SUBLANES, :]
        return jnp.broadcast_to(edge, (SUBLANES, r))

    if between is None:
        hc[...] = lax.fori_loop(0, ngroups, group, hc[...], unroll=4)
    else:
        h = hc[...]
        for gi in range(ngroups):
            h = group(gi, h)
            between(gi)
        hc[...] = h


def _scan_fwd_kernel(ucur, uprev, unext, cw, cb, gw, gbh, lam, wi32, wd32, hf, xc_out, wi16, wd16,
                     ext, a_s, b_s, hc, *, tc, nc, nl):
    col = lax.broadcasted_iota(jnp.int32, (1, wi32.shape[1]), 1)
    wi16[...] = (wi32[...] * jnp.where(col < wi32.shape[1] // 2, 0.5, 1.0)).astype(BF16)
    wd16[...] = wd32[...].astype(BF16)
    j = pl.program_id(1)
    in_ctx = j < nc
    pos = jnp.where(in_ctx, j, j - nc)
    first = pos == 0
    last = pos == jnp.where(in_ctx, nc, nl) - 1

    @pl.when(j == 0)
    def _():
        hc[...] = jnp.zeros_like(hc)

    n_ext = tc + 2 * SUBLANES
    ext[0:SUBLANES, :] = jnp.where(first, 0.0, uprev[...])
    ext[SUBLANES:SUBLANES + tc, :] = ucur[...]
    ext[SUBLANES + tc:n_ext, :] = jnp.where(last, 0.0, unext[...])
    e = ext[...]
    mid = slice(SUBLANES, SUBLANES + tc)
    xc = cb[...] + cw[0:1, :] * pltpu.roll(e, 1, 0)[mid] + cw[1:2, :] * e[mid] \
        + cw[2:3, :] * pltpu.roll(e, n_ext - 1, 0)[mid] + cw[3:4, :] * pltpu.roll(e, n_ext - 2, 0)[mid]
    xc_out[...] = xc
    _lru_coeffs(xc, gw, gbh, lam, a_s, b_s)
    _lru_chunk_scan(a_s, b_s, hf, hc, reverse=False, tc=tc)


def _scan_bwd_ffn_kernel(xc, gw, gbh, lam, hf, gg, ctx_res, x_res, mod, wout, mod_prev, fg, wi, wd,
                         out, a_s, b_s, hc, hb_s, x1_prev, *, tc, nc, per_batch, nsteps):
    s = pl.program_id(0)
    j = jnp.minimum(s, nsteps - 1) % per_batch

    @pl.when(s == 0)
    def _():
        x1_prev[...] = jnp.zeros_like(x1_prev)

    @pl.when(j == 0)
    def _():
        hc[...] = jnp.zeros_like(hc)

    pieces, ffn_result = _ffn_pieces(x1_prev[...], mod_prev, fg, wi, wd, n_chunks=wd.shape[1] // MXU_DIM)
    todo = list(pieces)
    run_next = lambda: todo.pop(0)() if todo else None
    run_two = lambda: (run_next(), run_next())
    _lru_coeffs(xc[...], gw, gbh, lam, a_s, b_s, between=run_two)
    ngroups = tc // SUBLANES
    n_scan = len(todo)

    def between(gi):
        while todo and (n_scan - len(todo)) * ngroups < (gi + 1) * n_scan:
            run_next()

    _lru_chunk_scan(a_s, b_s, hb_s, hc, reverse=True, tc=tc, between=between)
    out[...] = ffn_result()
    z = ((hf[...] + hb_s[...]) * gg[...]).astype(BF16)
    res = jnp.where(j < nc, ctx_res[...], x_res[...])
    x1_prev[...] = res + mod[0, 2:3, :] * jnp.dot(z, wout[...], preferred_element_type=F32)


def _lru_scans_ffn(u, conv_w, conv_b, gw, gbh, lam, gg, ctx2d, x2d, mods, wout, ffn_g, ffn_wi32, ffn_wd32, *,
                   layer, nb, ctx_len, seq, tc):
    n, r = u.shape
    d = x2d.shape[1]
    nc = ctx_len // tc
    nl = seq // tc
    blk8 = tc // SUBLANES

    def cur_blk(b, j, reverse):
        jc = (nc - 1 - j) if reverse else j
        jl = (nl - 1 - (j - nc)) if reverse else (j - nc)
        return jnp.where(j < nc, b * nc + jc, nb * nc + b * nl + jl)

    fwd_map = lambda b, j: (cur_blk(b, j, False), 0)
    bwd_map = lambda b, j: (cur_blk(b, j, True), 0)
    prev_map = lambda b, j: (jnp.maximum(cur_blk(b, j, False) * blk8 - 1, 0), 0)
    next_map = lambda b, j: (jnp.minimum((cur_blk(b, j, False) + 1) * blk8, n // SUBLANES - 1), 0)
    gate_specs = [
        _const_spec((LRU_BLOCKS, LRU_BLOCK, 2 * LRU_BLOCK)),
        _const_spec((2, r)),
        _const_spec((1, r)),
    ]
    chunk = pltpu.VMEM((tc, r), F32)
    carry = pltpu.VMEM((SUBLANES, r), F32)
    per_batch = nc + nl
    nsteps = nb * per_batch
    depth, hid = ffn_wd32.shape[:2]
    wi32 = ffn_wi32.reshape(depth * d, 2 * hid)
    wd32 = ffn_wd32.reshape(depth * hid, d)

    def slab_spec(w):
        rows = w.shape[0]
        blk = next(b for b in range(BF16_ROWS, rows + 1, BF16_ROWS) if rows % b == 0 and rows // b <= nsteps)
        return pl.BlockSpec((blk, w.shape[1]), lambda b, j: (jnp.minimum(b * per_batch + j, rows // blk - 1), 0))

    hf, xc, ffn_wi, ffn_wd = pl.pallas_call(
        functools.partial(_scan_fwd_kernel, tc=tc, nc=nc, nl=nl),
        grid=(nb, nc + nl),
        in_specs=[
            pl.BlockSpec((tc, r), fwd_map),
            pl.BlockSpec((SUBLANES, r), prev_map),
            pl.BlockSpec((SUBLANES, r), next_map),
            _const_spec((CONV_W, r)),
            _const_spec((1, r)),
        ] + gate_specs + [slab_spec(wi32), slab_spec(wd32)],
        out_specs=[pl.BlockSpec((tc, r), fwd_map), pl.BlockSpec((tc, r), fwd_map), slab_spec(wi32), slab_spec(wd32)],
        out_shape=[jax.ShapeDtypeStruct((n, r), F32), jax.ShapeDtypeStruct((n, r), F32),
                   jax.ShapeDtypeStruct(wi32.shape, BF16), jax.ShapeDtypeStruct(wd32.shape, BF16)],
        scratch_shapes=[pltpu.VMEM((tc + 2 * SUBLANES, r), F32), chunk, chunk, carry],
        compiler_params=_cparams(("arbitrary", "arbitrary")),
        name="lru_scan_fwd",
    )(u, u, u, conv_w, conv_b, gw[0], gbh[0], lam[0], wi32, wd32)
    ffn_wi = ffn_wi.reshape(depth, d, 2 * hid)
    ffn_wd = ffn_wd.reshape(depth, hid, d)

    def at(step_map, lag):
        def index_map(s):
            sc = jnp.clip(s - lag, 0, nsteps - 1)
            return step_map(sc // per_batch, sc % per_batch)
        return index_map

    ctx_res_map = lambda b, j: (b * nc + jnp.maximum(nc - 1 - j, 0), 0)
    x_res_map = lambda b, j: (b * nl + nl - 1 - jnp.maximum(j - nc, 0), 0)
    mod_map = lambda b, j: (jnp.where(j < nc, nb, b), 0, 0)
    x2 = pl.pallas_call(
        functools.partial(_scan_bwd_ffn_kernel, tc=tc, nc=nc, per_batch=per_batch, nsteps=nsteps),
        grid=(nsteps + 1,),
        in_specs=[pl.BlockSpec((tc, r), at(bwd_map, 0))] + gate_specs + [
            pl.BlockSpec((tc, r), at(bwd_map, 0)),
            pl.BlockSpec((tc, r), at(bwd_map, 0)),
            pl.BlockSpec((tc, d), at(ctx_res_map, 0)),
            pl.BlockSpec((tc, d), at(x_res_map, 0)),
            pl.BlockSpec((1, 6, d), at(mod_map, 0)),
            _const_spec((r, d)),
            pl.BlockSpec((1, 6, d), at(mod_map, 1)),
            _const_spec((1, d)),
            pl.BlockSpec((1, d, 2 * hid), lambda s: (layer, 0, 0), pipeline_mode=pl.Buffered(1)),
            pl.BlockSpec((1, hid, d), lambda s: (layer, 0, 0), pipeline_mode=pl.Buffered(1)),
        ],
        out_specs=pl.BlockSpec((tc, d), at(bwd_map, 1)),
        out_shape=jax.ShapeDtypeStruct((n, d), F32),
        scratch_shapes=[chunk, chunk, carry, chunk, pltpu.VMEM((tc, d), F32)],
        compiler_params=_cparams(("arbitrary",)),
        name="lru_scan_bwd_ffn",
    )(xc, gw[1], gbh[1], lam[1], hf, gg, ctx2d, x2d, mods, wout, mods, ffn_g, ffn_wi, ffn_wd)
    return x2, ffn_wi, ffn_wd


def _ffn_pieces(x, mod_ref, g_ref, wi_ref, wd_ref, *, n_chunks):
    h = _norm_mod(x, g_ref[...], mod_ref[0, 3:4, :], mod_ref[0, 4:5, :]).astype(BF16)
    hid = wd_ref.shape[1]
    n_mxu = hid // MXU_DIM
    edges = [round(c * n_mxu / n_chunks) * MXU_DIM for c in range(n_chunks)] + [hid]
    acc = [jnp.zeros(x.shape, F32)]
    pending = []

    def down():
        act, e0, e1 = pending.pop(0)
        acc[0] = acc[0] + jnp.dot(act, wd_ref[0, e0:e1, :], preferred_element_type=F32)

    def piece(e0, e1):
        def run():
            half_gate = jnp.dot(h, wi_ref[0, :, e0:e1], preferred_element_type=F32)
            up = jnp.dot(h, wi_ref[0, :, hid + e0:hid + e1], preferred_element_type=F32)
            if len(pending) >= FFN_LAG:
                down()
            pending.append(((((jnp.tanh(half_gate) + 1.0) * half_gate) * up).astype(BF16), e0, e1))
        return run

    def result():
        while pending:
            down()
        return x + mod_ref[0, 5:6, :] * acc[0]

    return [piece(e0, e1) for e0, e1 in zip(edges[:-1], edges[1:])], result


def _attn_out_ffn_kernel(x_ref, o_ref, wo_ref, mod_ref, g_ref, wi_ref, wd_ref, fg_ref, out_ref):
    ts = x_ref.shape[0] // N_SUB_FFN
    subs = [slice(i * ts, (i + 1) * ts) for i in range(N_SUB_FFN)]
    xs = [x_ref[sl, :] + mod_ref[0, 2:3, :] * jnp.dot(o_ref[sl, :], wo_ref[...], preferred_element_type=F32)
          for sl in subs]
    ffns = [_ffn_pieces(x, mod_ref, g_ref, wi_ref, wd_ref, n_chunks=FFN_CHUNKS) for x in xs]
    for c in range(FFN_CHUNKS):
        for pieces, _ in ffns:
            pieces[c]()
    for sl, (_, result) in zip(subs, ffns):
        out_ref[sl, :] = _rmsnorm(result(), fg_ref[...])


def _attn_out_ffn(xall, o, wo, mods, g, wi, wd, fg, *, layer, n_ctx_rows, seq, tm):
    n, d = xall.shape
    hid = wd.shape[1]
    off = n_ctx_rows // tm
    tpb = seq // tm
    return pl.pallas_call(
        _attn_out_ffn_kernel,
        grid=((n - n_ctx_rows) // tm,),
        in_specs=[
            pl.BlockSpec((tm, d), lambda i: (i + off, 0)),
            pl.BlockSpec((tm, o.shape[1]), lambda i: (i, 0)),
            _const_spec(wo.shape),
            pl.BlockSpec((1, 6, d), lambda i: (i // tpb, 0, 0)),
            _const_spec((1, d)),
            pl.BlockSpec((1, d, 2 * hid), lambda i: (layer, 0, 0), pipeline_mode=pl.Buffered(1)),
            pl.BlockSpec((1, hid, d), lambda i: (layer, 0, 0), pipeline_mode=pl.Buffered(1)),
            _const_spec((1, d)),
        ],
        out_specs=pl.BlockSpec((tm, d), lambda i: (i, 0)),
        out_shape=jax.ShapeDtypeStruct((n - n_ctx_rows, d), F32),
        compiler_params=_cparams(("arbitrary",)),
        name="attn_out_ffn",
    )(xall, o, wo, mods, g, wi, wd, fg)


def _mla_proj_kernel(*refs, latent, n_sub):
    if latent:
        (x_ref, g_ref, mod_ref, win_ref, kvg_ref, wk_ref, wvt_ref, qg_ref, wq_ref, cos_ref, sin_ref,
         k_ref, v_ref, q_ref) = refs
    else:
        x_ref, g_ref, mod_ref, win_ref, kvg_ref, wk_ref, wvt_ref, k_ref, v_ref = refs
    ts = x_ref.shape[0] // n_sub
    subs = [slice(i * ts, (i + 1) * ts) for i in range(n_sub)]
    kv_lo = Q_LORA + KV_LORA
    lane = lax.broadcasted_iota(jnp.int32, (1, 128), 1)
    low = lane < QK_ROPE
    hs = [_norm_mod(x_ref[sl, :], g_ref[...], mod_ref[0, 0:1, :], mod_ref[0, 1:2, :]).astype(BF16) for sl in subs]
    c_alls = [jnp.dot(h, win_ref[...], preferred_element_type=F32) for h in hs]
    for sl, c_all in zip(subs, c_alls):
        cosv, sinv = (cos_ref[sl, :], sin_ref[sl, :]) if latent else (1.0, 0.0)
        kt = c_all[:, kv_lo:kv_lo + 128] * jnp.where(low, cosv, sinv)
        kt = kt + pltpu.roll(kt, QK_ROPE, 1)
        k_rope = (jnp.where(low, kt, 0.0).astype(BF16), jnp.where(low, 0.0, kt).astype(BF16))
        ckv = _rmsnorm(c_all[:, Q_LORA:kv_lo], kvg_ref[...])
        k_nope = jnp.dot(ckv.astype(BF16), wk_ref[...], preferred_element_type=F32)
        v_t = jnp.dot(wvt_ref[...], ckv.T.astype(BF16), preferred_element_type=F32)
        for hd in range(MLA_HEADS):
            hl = slice(hd * 128, (hd + 1) * 128)
            k_ref[0, hd, sl, 0:QK_NOPE] = k_nope[:, hl].astype(BF16)
            k_ref[0, hd, sl, QK_NOPE:HEAD_PAD] = k_rope[hd % 2]
            v_ref[0, hd, :, sl] = v_t[hl, :].astype(BF16)
    if latent:
        for sl, c_all in zip(subs, c_alls):
            cq = _rmsnorm(c_all[:, :Q_LORA], qg_ref[...]).astype(BF16)
            q_all = jnp.dot(cq, wq_ref[...], preferred_element_type=F32) * Q_PRESCALE
            hw = MLA_HEADS * QK_NOPE
            hr = MLA_HEADS * QK_ROPE
            for pair in range(MLA_HEADS // 2):
                ps = slice(hw + pair * 128, hw + (pair + 1) * 128)
                ss = slice(hw + hr + pair * 128, hw + hr + (pair + 1) * 128)
                q_rope = (q_all[:, ps] * cos_ref[sl, :] + q_all[:, ss] * sin_ref[sl, :]).astype(BF16)
                for hd in (2 * pair, 2 * pair + 1):
                    q_ref[0, hd, sl, 0:QK_NOPE] = q_all[:, hd * 128:(hd + 1) * 128].astype(BF16)
                    q_ref[0, hd, sl, QK_NOPE:HEAD_PAD] = q_rope


def _mla_proj(xall, g, mods, win, kvg, wk, wvt, *, nb, ctx_len, seq, tm, latent_args=None):
    n, d = xall.shape
    latent = latent_args is not None
    rows = seq if latent else ctx_len
    off = nb * ctx_len // tm if latent else 0
    tpb = rows // tm
    kv_map = lambda i: (i // tpb, 0, i % tpb, 0)
    vt_map = lambda i: (i // tpb, 0, 0, i % tpb)
    in_specs = [
        pl.BlockSpec((tm, d), lambda i: (i + off, 0)),
        _const_spec((1, d)),
        pl.BlockSpec((1, 6, d), lambda i: ((i // tpb) if latent else nb, 0, 0)),
        _const_spec(win.shape),
        _const_spec((1, KV_LORA)),
        _const_spec(wk.shape),
        _const_spec(wvt.shape),
    ]
    args = [xall, g, mods, win, kvg, wk, wvt]
    out_specs = [pl.BlockSpec((1, MLA_HEADS, tm, HEAD_PAD), kv_map), pl.BlockSpec((1, MLA_HEADS, V_HEAD, tm), vt_map)]
    out_shape = [jax.ShapeDtypeStruct((nb, MLA_HEADS, rows, HEAD_PAD), BF16),
                 jax.ShapeDtypeStruct((nb, MLA_HEADS, V_HEAD, rows), BF16)]
    if latent:
        qg, wq, cos_t, sin_t = latent_args
        tab_map = lambda i: (i % tpb, 0)
        in_specs += [_const_spec((1, Q_LORA)), _const_spec(wq.shape),
                     pl.BlockSpec((tm, 128), tab_map), pl.BlockSpec((tm, 128), tab_map)]
        args += [qg, wq, cos_t, sin_t]
        out_specs.append(pl.BlockSpec((1, MLA_HEADS, tm, HEAD_PAD), kv_map))
        out_shape.append(jax.ShapeDtypeStruct((nb, MLA_HEADS, rows, HEAD_PAD), BF16))
    return pl.pallas_call(
        functools.partial(_mla_proj_kernel, latent=latent, n_sub=max(1, tm // MXU_DIM)),
        grid=(nb * tpb,),
        in_specs=in_specs,
        out_specs=out_specs,
        out_shape=out_shape,
        compiler_params=_cparams(("arbitrary",)),
        name="mla_proj_lat" if latent else "mla_proj_ctx",
    )(*args)


def _attn_kernel(q_ref, k_ref, kc_ref, vt_ref, vtc_ref, o_ref, s0_ref, s1_ref, qt_ref, acc_ref, l_ref, *, tq, kc):
    n_lat = k_ref.shape[2]
    t_all = n_lat + kc_ref.shape[2]
    nkc = t_all // kc
    nq = q_ref.shape[2] // tq
    s_refs = (s0_ref, s1_ref)

    def group_reduce(x, op):
        return op(x.reshape(x.shape[0] // SUBLANES, SUBLANES, tq), axis=0)

    def k_chunk(c):
        if c * kc < n_lat:
            return k_ref[0, 0, c * kc:(c + 1) * kc, :]
        return kc_ref[0, 0, c * kc - n_lat:(c + 1) * kc - n_lat, :]

    def vt_chunk(c):
        if c * kc < n_lat:
            return vt_ref[0, 0, :, c * kc:(c + 1) * kc]
        return vtc_ref[0, 0, :, c * kc - n_lat:(c + 1) * kc - n_lat]

    def scores(qt, c, s_ref):
        st = jnp.dot(k_chunk(c), qt, preferred_element_type=F32)
        s_ref[c * kc:(c + 1) * kc, :] = st
        return group_reduce(st, jnp.max)

    def q_tile_t(i):
        q = q_ref[0, 0, pl.ds(pl.multiple_of(i * tq, tq), tq), :]
        return q.astype(F32).T.astype(BF16)

    def finalize(i):
        l = jnp.sum(l_ref[...], axis=0, keepdims=True)
        rows = pl.ds(pl.multiple_of(i * tq, tq), tq)
        o_ref[0, rows, :] = (acc_ref[...] / l).T.astype(o_ref.dtype)

    def tile_step(i, slot, m8, with_next):
        s_cur, s_nxt = s_refs[slot], s_refs[1 - slot]
        m_next = jnp.full((SUBLANES, tq), -jnp.inf, F32)
        if with_next:
            qt_next = qt_ref[1 - slot]
        m = jnp.max(m8, axis=0, keepdims=True)
        l8 = jnp.zeros((SUBLANES, tq), F32)
        acc = jnp.zeros((V_HEAD, tq), F32)
        p_prev = None
        for c in range(nkc):
            if with_next:
                m_next = jnp.maximum(m_next, scores(qt_next, c, s_nxt))
            if p_prev is not None:
                acc = acc + jnp.dot(vt_chunk(c - 1), p_prev, preferred_element_type=F32)
            p = jnp.exp2(s_cur[c * kc:(c + 1) * kc, :] - m)
            l8 = l8 + group_reduce(p, jnp.sum)
            p_prev = p.astype(BF16)
            if c == 1:
                finalize(jnp.maximum(i - 1, 0))
            if c == nkc // 2 and with_next:
                qt_ref[slot] = q_tile_t(jnp.minimum(i + 2, nq - 1))
        acc_ref[...] = acc + jnp.dot(vt_chunk(nkc - 1), p_prev, preferred_element_type=F32)
        l_ref[...] = l8
        return m_next

    acc_ref[...] = jnp.zeros_like(acc_ref)
    l_ref[...] = jnp.ones_like(l_ref)
    qt0 = q_tile_t(0)
    qt_ref[1] = q_tile_t(min(1, nq - 1))
    m8 = scores(qt0, 0, s0_ref)
    for c in range(1, nkc):
        m8 = jnp.maximum(m8, scores(qt0, c, s0_ref))

    def step(i, m8):
        return lax.cond(i % 2 == 0,
                        lambda m: tile_step(i, 0, m, True),
                        lambda m: tile_step(i, 1, m, True), m8)

    m8 = lax.fori_loop(0, nq - 1, step, m8)
    tile_step(nq - 1, (nq - 1) % 2, m8, False)
    finalize(nq - 1)


def _attention(q, k, kc, vt, vtc, *, tq):
    nb, nh, seq, dp = q.shape
    ctx_len = kc.shape[2]
    t_all = seq + ctx_len
    chunk = int(np.gcd(np.gcd(MXU_DIM, seq), ctx_len))
    head = lambda b, h: (b, h, 0, 0)
    return pl.pallas_call(
        functools.partial(_attn_kernel, tq=tq, kc=chunk),
        grid=(nb, nh),
        scratch_shapes=[pltpu.VMEM((t_all, tq), F32), pltpu.VMEM((t_all, tq), F32),
                        pltpu.VMEM((2, dp, tq), BF16), pltpu.VMEM((V_HEAD, tq), F32), pltpu.VMEM((SUBLANES, tq), F32)],
        in_specs=[
            pl.BlockSpec((1, 1, seq, dp), head),
            pl.BlockSpec((1, 1, seq, dp), head),
            pl.BlockSpec((1, 1, ctx_len, dp), head),
            pl.BlockSpec((1, 1, V_HEAD, seq), head),
            pl.BlockSpec((1, 1, V_HEAD, ctx_len), head),
        ],
        out_specs=pl.BlockSpec((1, seq, V_HEAD), lambda b, h: (b, 0, h)),
        out_shape=jax.ShapeDtypeStruct((nb, seq, nh * V_HEAD), BF16),
        compiler_params=_cparams(("arbitrary", "arbitrary")),
        name="mla_attention",
    )(q, k, kc, vt, vtc)


def _rope_cols(w):
    qd = QK_ROPE // 4
    r0, r1, r2, r3 = (w[..., i * qd:(i + 1) * qd] for i in range(4))
    return jnp.concatenate([r0, r2, r1, r3], axis=-1), jnp.concatenate([-r1, -r3, r0, r2], axis=-1)


def _rope_tables(seq):
    rows = seq // GRID_W
    row_ids = np.repeat(np.arange(rows, dtype=np.float32), GRID_W)
    col_ids = np.tile(np.arange(GRID_W, dtype=np.float32), rows)
    axis_dim = QK_ROPE // 2
    expo = (np.arange(0, axis_dim, 2, dtype=np.float32) / np.float32(axis_dim)).astype(np.float32)
    inv_freq = (np.float32(1.0) / np.power(np.float32(ROPE_THETA), expo)).astype(np.float32)
    ang_r = (row_ids[:, None] * inv_freq).astype(np.float32)
    ang_c = (col_ids[:, None] * inv_freq).astype(np.float32)
    reps = 128 // (QK_ROPE // 2)
    cos_t = np.tile(np.concatenate([np.cos(ang_r), np.cos(ang_c)], axis=1), (1, reps))
    sin_t = np.tile(np.concatenate([np.sin(ang_r), np.sin(ang_c)], axis=1), (1, reps))
    return jnp.asarray(cos_t, F32), jnp.asarray(sin_t, F32)


def kernel(x, c, ctx, c_ctx, ada_w, ada_b, norm_mix_g, norm_ffn_g, ffn_w_in, ffn_w_out, lru_w_in, lru_conv_w,
           lru_conv_b, lru_gate_w, lru_gate_b, lru_lambda, lru_w_out, mla_w_in, mla_q_norm_g, mla_kv_norm_g,
           mla_w_uq, mla_w_ukv, mla_w_o, final_norm_g):
    nb, seq, d = x.shape
    ctx_len = ctx.shape[1]
    depth = ada_w.shape[0]
    assert depth == 2 and d == D_MODEL and nb + 1 <= MOD_ROWS
    n_ctx_rows = nb * ctx_len
    tile = lambda cap: int(np.gcd(np.gcd(cap, n_ctx_rows), seq))

    ctx2d = ctx.reshape(n_ctx_rows, d)
    x2d = x.reshape(nb * seq, d)

    cv = jnp.concatenate([c, c_ctx[None, :], jnp.zeros((MOD_ROWS - nb - 1, d), F32)], axis=0)
    mods = _ada_mods(cv, ada_w, ada_b).reshape(depth, MOD_ROWS, 6, d)

    tm = tile(1024)
    gg, u = _lru_in(ctx2d, x2d, norm_mix_g[0][None], mods[0], lru_w_in[0].astype(BF16), nb=nb, seq=seq, tm=tm)
    tc = int(np.gcd(np.gcd(256, ctx_len), seq))
    gw = lru_gate_w[0]
    gw = (0.5 * jnp.concatenate([gw[:, 0], gw[:, 1]], axis=-1)).astype(BF16)
    x2, ffn_wi, ffn_wd = _lru_scans_ffn(
        u, lru_conv_w[0], lru_conv_b[0][None], gw, 0.5 * lru_gate_b[0], lru_lambda[0][:, None, :], gg, ctx2d, x2d,
        mods[0], lru_w_out[0].astype(BF16), norm_ffn_g[0][None], ffn_w_in, ffn_w_out,
        layer=0, nb=nb, ctx_len=ctx_len, seq=seq, tc=tc)

    w_in = mla_w_in[0]
    win_p = jnp.concatenate((w_in[:, :Q_LORA + KV_LORA],) + _rope_cols(w_in[:, Q_LORA + KV_LORA:]),
                            axis=1).astype(BF16)
    wq3 = mla_w_uq[0].reshape(Q_LORA, MLA_HEADS, QK_NOPE + QK_ROPE)
    wq_all = jnp.concatenate([part.reshape(Q_LORA, -1)
                              for part in (wq3[:, :, :QK_NOPE],) + _rope_cols(wq3[:, :, QK_NOPE:])],
                             axis=1).astype(BF16)
    cos_t, sin_t = _rope_tables(seq)
    wkv3 = mla_w_ukv[0].reshape(KV_LORA, MLA_HEADS, QK_NOPE + V_HEAD)
    wk = wkv3[:, :, :QK_NOPE].reshape(KV_LORA, MLA_HEADS * QK_NOPE).astype(BF16)
    wvt = wkv3[:, :, QK_NOPE:].reshape(KV_LORA, MLA_HEADS * V_HEAD).T.astype(BF16)
    proj_args = (x2, norm_mix_g[1][None], mods[1], win_p, mla_kv_norm_g[0][None], wk, wvt)
    kc, vtc = _mla_proj(*proj_args, nb=nb, ctx_len=ctx_len, seq=seq, tm=int(np.gcd(256, ctx_len)))
    k, vt, q = _mla_proj(*proj_args, nb=nb, ctx_len=ctx_len, seq=seq, tm=tile(1024),
                         latent_args=(mla_q_norm_g[0][None], wq_all, cos_t, sin_t))
    o = _attention(q, k, kc, vt, vtc, tq=int(np.gcd(512, seq)))
    out = _attn_out_ffn(x2, o.reshape(nb * seq, MLA_HEADS * V_HEAD), mla_w_o[0].astype(BF16), mods[1],
                        norm_ffn_g[1][None], ffn_wi, ffn_wd, final_norm_g[None], layer=1,
                        n_ctx_rows=n_ctx_rows, seq=seq, tm=tile(512))
    return out.reshape(nb, seq, d)
```

```python
import functools

import numpy as np
import jax
import jax.numpy as jnp
from jax import lax
from jax.experimental import pallas as pl
from jax.experimental.pallas import tpu as pltpu

F32 = jnp.float32
BF16 = jnp.bfloat16

D_MODEL = 1024
GRID_W = 64
NORM_EPS = 1e-6
LRU_WIDTH = D_MODEL
LRU_BLOCK = 256
LRU_BLOCKS = LRU_WIDTH // LRU_BLOCK
CONV_W = 4
RG_C = 8.0
MLA_HEADS = 8
Q_LORA = 384
KV_LORA = 256
QK_NOPE = 128
QK_ROPE = 64
V_HEAD = 128
SM_SCALE = (QK_NOPE + QK_ROPE) ** -0.5
LOG2_E = float(np.log2(np.e))
Q_PRESCALE = SM_SCALE * LOG2_E
ROPE_THETA = 10000.0

MOD_ROWS = 8
SUBLANES = 8
BF16_ROWS = 16
MXU_DIM = 256
HEAD_PAD = 256
VMEM_LIMIT_MB = 56
N_SUB_FFN = 4
FFN_CHUNKS = 2
FFN_LAG = 2
TINY = 1e-30


def _cparams(sem, vmem_mb=VMEM_LIMIT_MB):
    return pltpu.CompilerParams(dimension_semantics=sem, vmem_limit_bytes=vmem_mb << 20)


def _const_spec(shape):
    nd = len(shape)
    return pl.BlockSpec(shape, lambda *_: (0,) * nd)


def _norm_mod(x, g, shift, scale):
    ms = jnp.mean(x * x, axis=-1, keepdims=True)
    y = (x * lax.rsqrt(ms + NORM_EPS)) * g
    return y * (1.0 + scale) + shift


def _rmsnorm(x, g):
    ms = jnp.mean(x * x, axis=-1, keepdims=True)
    return (x * lax.rsqrt(ms + NORM_EPS)) * g


def _ada_kernel(cv_ref, w_ref, b_ref, o_ref):
    cv = cv_ref[...]
    s = cv * jax.nn.sigmoid(cv)
    o_ref[0] = jnp.dot(s.astype(BF16), w_ref[0].astype(BF16), preferred_element_type=F32) + b_ref[0]


def _ada_mods(cv, ada_w, ada_b):
    depth, d, n = ada_w.shape
    tn = 1536
    return pl.pallas_call(
        _ada_kernel,
        grid=(depth, n // tn),
        in_specs=[
            pl.BlockSpec((MOD_ROWS, d), lambda l, j: (0, 0)),
            pl.BlockSpec((1, d, tn), lambda l, j: (l, 0, j)),
            pl.BlockSpec((1, 1, tn), lambda l, j: (l, 0, j)),
        ],
        out_specs=pl.BlockSpec((1, MOD_ROWS, tn), lambda l, j: (l, 0, j)),
        out_shape=jax.ShapeDtypeStruct((depth, MOD_ROWS, n), F32),
        compiler_params=_cparams(("arbitrary", "arbitrary")),
        name="ada_mods",
    )(cv, ada_w, ada_b.reshape(depth, 1, n))


def _lru_in_kernel(ctx_ref, x_ref, g_ref, mod_ref, w_ref, gg_ref, u_ref, *, nct, n_sub):
    is_ctx = pl.program_id(0) < nct
    ts = x_ref.shape[0] // n_sub
    subs = [slice(i * ts, (i + 1) * ts) for i in range(n_sub)]
    hs = [_norm_mod(jnp.where(is_ctx, ctx_ref[sl, :], x_ref[sl, :]), g_ref[...],
                    mod_ref[0, 0:1, :], mod_ref[0, 1:2, :]).astype(BF16) for sl in subs]
    ys = [jnp.dot(h, w_ref[...], preferred_element_type=F32) for h in hs]
    r = LRU_WIDTH
    for sl, y in zip(subs, ys):
        gg_ref[sl, :] = jax.nn.gelu(y[:, :r])
        u_ref[sl, :] = y[:, r:]


def _lru_in(ctx2d, x2d, g, mods, w, *, nb, seq, tm):
    d = x2d.shape[1]
    n = ctx2d.shape[0] + x2d.shape[0]
    r = LRU_WIDTH
    nct = ctx2d.shape[0] // tm
    tpb = seq // tm

    def mod_map(i):
        return (jnp.where(i < nct, nb, (i - nct) // tpb), 0, 0)

    return pl.pallas_call(
        functools.partial(_lru_in_kernel, nct=nct, n_sub=max(1, tm // MXU_DIM)),
        grid=(n // tm,),
        in_specs=[
            pl.BlockSpec((tm, d), lambda i: (jnp.minimum(i, nct - 1), 0)),
            pl.BlockSpec((tm, d), lambda i: (jnp.maximum(i - nct, 0), 0)),
            _const_spec((1, d)),
            pl.BlockSpec((1, 6, d), mod_map),
            _const_spec((d, 2 * r)),
        ],
        out_specs=[pl.BlockSpec((tm, r), lambda i: (i, 0)), pl.BlockSpec((tm, r), lambda i: (i, 0))],
        out_shape=[jax.ShapeDtypeStruct((n, r), F32), jax.ShapeDtypeStruct((n, r), F32)],
        compiler_params=_cparams(("arbitrary",)),
        name="lru_in",
    )(ctx2d, x2d, g, mods, w)


def _lru_coeffs(xc, gw, gbh, lam, a_s, b_s, between=None):
    xcb = xc.astype(BF16)
    neg = -lam[...]
    sp = jnp.maximum(neg, 0.0) + jnp.log1p(jnp.exp(-jnp.abs(neg)))
    c2 = (-0.5 * RG_C * LOG2_E) * sp
    for n in range(LRU_BLOCKS):
        sl = slice(n * LRU_BLOCK, (n + 1) * LRU_BLOCK)
        pre = jnp.dot(xcb[:, sl], gw[n], preferred_element_type=F32)
        t_r = jnp.tanh(pre[:, :LRU_BLOCK] + gbh[0:1, sl])
        t_i = jnp.tanh(pre[:, LRU_BLOCK:] + gbh[1:2, sl])
        a = jnp.exp2(c2[:, sl] * t_r + c2[:, sl])
        gap = 1.0 - a * a
        mult = gap * lax.rsqrt(jnp.maximum(gap, TINY))
        a_s[:, sl] = a
        b_s[:, sl] = (mult * (0.5 * t_i + 0.5)) * xc[:, sl]
        if between is not None:
            between()


def _lru_chunk_scan(a_s, b_s, h_s, hc, *, reverse, tc, between=None):
    r = LRU_WIDTH
    row = lax.broadcasted_iota(jnp.int32, (SUBLANES, r), 0)
    ngroups = tc // SUBLANES

    def group(gi, hprev):
        g0 = (ngroups - 1 - gi) if reverse else gi
        start = g0 * SUBLANES
        rows = pl.ds(start if isinstance(start, int) else pl.multiple_of(start, SUBLANES), SUBLANES)
        av = a_s[rows, :]
        bv = b_s[rows, :]
        for s in (1, 2, 4):
            if reverse:
                keep = row < (SUBLANES - s)
                shift = SUBLANES - s
            else:
                keep = row >= s
                shift = s
            a_sh = jnp.where(keep, pltpu.roll(av, shift, 0), 1.0)
            b_sh = jnp.where(keep, pltpu.roll(bv, shift, 0), 0.0)
            bv = av * b_sh + bv
            av = av * a_sh
        hrows = av * hprev + bv
        h_s[rows, :] = hrows
        edge = hrows[0:1, :] if reverse else hrows[SUBLANES - 1:SUBLANES, :]
        return jnp.broadcast_to(edge, (SUBLANES, r))

    if between is None:
        hc[...] = lax.fori_loop(0, ngroups, group, hc[...], unroll=4)
    else:
        h = hc[...]
        for gi in range(ngroups):
            h = group(gi, h)
            between(gi)
        hc[...] = h


def _scan_fwd_kernel(ucur, uprev, unext, cw, cb, gw, gbh, lam, wi32, wd32, hf, xc_out, wi16, wd16,
                     ext, a_s, b_s, hc, *, tc, nc, nl):
    col = lax.broadcasted_iota(jnp.int32, (1, wi32.shape[1]), 1)
    wi16[...] = (wi32[...] * jnp.where(col < wi32.shape[1] // 2, 0.5, 1.0)).astype(BF16)
    wd16[...] = wd32[...].astype(BF16)
    j = pl.program_id(1)
    in_ctx = j < nc
    pos = jnp.where(in_ctx, j, j - nc)
    first = pos == 0
    last = pos == jnp.where(in_ctx, nc, nl) - 1

    @pl.when(j == 0)
    def _():
        hc[...] = jnp.zeros_like(hc)

    n_ext = tc + 2 * SUBLANES
    ext[0:SUBLANES, :] = jnp.where(first, 0.0, uprev[...])
    ext[SUBLANES:SUBLANES + tc, :] = ucur[...]
    ext[SUBLANES + tc:n_ext, :] = jnp.where(last, 0.0, unext[...])
    e = ext[...]
    mid = slice(SUBLANES, SUBLANES + tc)
    xc = cb[...] + cw[0:1, :] * pltpu.roll(e, 1, 0)[mid] + cw[1:2, :] * e[mid] \
        + cw[2:3, :] * pltpu.roll(e, n_ext - 1, 0)[mid] + cw[3:4, :] * pltpu.roll(e, n_ext - 2, 0)[mid]
    xc_out[...] = xc
    _lru_coeffs(xc, gw, gbh, lam, a_s, b_s)
    _lru_chunk_scan(a_s, b_s, hf, hc, reverse=False, tc=tc)


def _scan_bwd_ffn_kernel(xc, gw, gbh, lam, hf, gg, ctx_res, x_res, mod, wout, mod_prev, fg, wi, wd,
                         out, a_s, b_s, hc, hb_s, x1_prev, *, tc, nc, per_batch, nsteps):
    s = pl.program_id(0)
    j = jnp.minimum(s, nsteps - 1) % per_batch

    @pl.when(s == 0)
    def _():
        x1_prev[...] = jnp.zeros_like(x1_prev)

    @pl.when(j == 0)
    def _():
        hc[...] = jnp.zeros_like(hc)

    pieces, ffn_result = _ffn_pieces(x1_prev[...], mod_prev, fg, wi, wd, n_chunks=wd.shape[1] // MXU_DIM)
    todo = list(pieces)
    run_next = lambda: todo.pop(0)() if todo else None
    run_two = lambda: (run_next(), run_next())
    _lru_coeffs(xc[...], gw, gbh, lam, a_s, b_s, between=run_two)
    ngroups = tc // SUBLANES
    n_scan = len(todo)

    def between(gi):
        while todo and (n_scan - len(todo)) * ngroups < (gi + 1) * n_scan:
            run_next()

    _lru_chunk_scan(a_s, b_s, hb_s, hc, reverse=True, tc=tc, between=between)
    out[...] = ffn_result()
    z = ((hf[...] + hb_s[...]) * gg[...]).astype(BF16)
    res = jnp.where(j < nc, ctx_res[...], x_res[...])
    x1_prev[...] = res + mod[0, 2:3, :] * jnp.dot(z, wout[...], preferred_element_type=F32)


def _lru_scans_ffn(u, conv_w, conv_b, gw, gbh, lam, gg, ctx2d, x2d, mods, wout, ffn_g, ffn_wi32, ffn_wd32, *,
                   layer, nb, ctx_len, seq, tc):
    n, r = u.shape
    d = x2d.shape[1]
    nc = ctx_len // tc
    nl = seq // tc
    blk8 = tc // SUBLANES

    def cur_blk(b, j, reverse):
        jc = (nc - 1 - j) if reverse else j
        jl = (nl - 1 - (j - nc)) if reverse else (j - nc)
        return jnp.where(j < nc, b * nc + jc, nb * nc + b * nl + jl)

    fwd_map = lambda b, j: (cur_blk(b, j, False), 0)
    bwd_map = lambda b, j: (cur_blk(b, j, True), 0)
    prev_map = lambda b, j: (jnp.maximum(cur_blk(b, j, False) * blk8 - 1, 0), 0)
    next_map = lambda b, j: (jnp.minimum((cur_blk(b, j, False) + 1) * blk8, n // SUBLANES - 1), 0)
    gate_specs = [
        _const_spec((LRU_BLOCKS, LRU_BLOCK, 2 * LRU_BLOCK)),
        _const_spec((2, r)),
        _const_spec((1, r)),
    ]
    chunk = pltpu.VMEM((tc, r), F32)
    carry = pltpu.VMEM((SUBLANES, r), F32)
    per_batch = nc + nl
    nsteps = nb * per_batch
    depth, hid = ffn_wd32.shape[:2]
    wi32 = ffn_wi32.reshape(depth * d, 2 * hid)
    wd32 = ffn_wd32.reshape(depth * hid, d)

    def slab_spec(w):
        rows = w.shape[0]
        blk = next(b for b in range(BF16_ROWS, rows + 1, BF16_ROWS) if rows % b == 0 and rows // b <= nsteps)
        return pl.BlockSpec((blk, w.shape[1]), lambda b, j: (jnp.minimum(b * per_batch + j, rows // blk - 1), 0))

    hf, xc, ffn_wi, ffn_wd = pl.pallas_call(
        functools.partial(_scan_fwd_kernel, tc=tc, nc=nc, nl=nl),
        grid=(nb, nc + nl),
        in_specs=[
            pl.BlockSpec((tc, r), fwd_map),
            pl.BlockSpec((SUBLANES, r), prev_map),
            pl.BlockSpec((SUBLANES, r), next_map),
            _const_spec((CONV_W, r)),
            _const_spec((1, r)),
        ] + gate_specs + [slab_spec(wi32), slab_spec(wd32)],
        out_specs=[pl.BlockSpec((tc, r), fwd_map), pl.BlockSpec((tc, r), fwd_map), slab_spec(wi32), slab_spec(wd32)],
        out_shape=[jax.ShapeDtypeStruct((n, r), F32), jax.ShapeDtypeStruct((n, r), F32),
                   jax.ShapeDtypeStruct(wi32.shape, BF16), jax.ShapeDtypeStruct(wd32.shape, BF16)],
        scratch_shapes=[pltpu.VMEM((tc + 2 * SUBLANES, r), F32), chunk, chunk, carry],
        compiler_params=_cparams(("arbitrary", "arbitrary")),
        name="lru_scan_fwd",
    )(u, u, u, conv_w, conv_b, gw[0], gbh[0], lam[0], wi32, wd32)
    ffn_wi = ffn_wi.reshape(depth, d, 2 * hid)
    ffn_wd = ffn_wd.reshape(depth, hid, d)

    def at(step_map, lag):
        def index_map(s):
            sc = jnp.clip(s - lag, 0, nsteps - 1)
            return step_map(sc // per_batch, sc % per_batch)
        return index_map

    ctx_res_map = lambda b, j: (b * nc + jnp.maximum(nc - 1 - j, 0), 0)
    x_res_map = lambda b, j: (b * nl + nl - 1 - jnp.maximum(j - nc, 0), 0)
    mod_map = lambda b, j: (jnp.where(j < nc, nb, b), 0, 0)
    x2 = pl.pallas_call(
        functools.partial(_scan_bwd_ffn_kernel, tc=tc, nc=nc, per_batch=per_batch, nsteps=nsteps),
        grid=(nsteps + 1,),
        in_specs=[pl.BlockSpec((tc, r), at(bwd_map, 0))] + gate_specs + [
            pl.BlockSpec((tc, r), at(bwd_map, 0)),
            pl.BlockSpec((tc, r), at(bwd_map, 0)),
            pl.BlockSpec((tc, d), at(ctx_res_map, 0)),
            pl.BlockSpec((tc, d), at(x_res_map, 0)),
            pl.BlockSpec((1, 6, d), at(mod_map, 0)),
            _const_spec((r, d)),
            pl.BlockSpec((1, 6, d), at(mod_map, 1)),
            _const_spec((1, d)),
            pl.BlockSpec((1, d, 2 * hid), lambda s: (layer, 0, 0), pipeline_mode=pl.Buffered(1)),
            pl.BlockSpec((1, hid, d), lambda s: (layer, 0, 0), pipeline_mode=pl.Buffered(1)),
        ],
        out_specs=pl.BlockSpec((tc, d), at(bwd_map, 1)),
        out_shape=jax.ShapeDtypeStruct((n, d), F32),
        scratch_shapes=[chunk, chunk, carry, chunk, pltpu.VMEM((tc, d), F32)],
        compiler_params=_cparams(("arbitrary",)),
        name="lru_scan_bwd_ffn",
    )(xc, gw[1], gbh[1], lam[1], hf, gg, ctx2d, x2d, mods, wout, mods, ffn_g, ffn_wi, ffn_wd)
    return x2, ffn_wi, ffn_wd


def _ffn_pieces(x, mod_ref, g_ref, wi_ref, wd_ref, *, n_chunks):
    h = _norm_mod(x, g_ref[...], mod_ref[0, 3:4, :], mod_ref[0, 4:5, :]).astype(BF16)
    hid = wd_ref.shape[1]
    n_mxu = hid // MXU_DIM
    edges = [round(c * n_mxu / n_chunks) * MXU_DIM for c in range(n_chunks)] + [hid]
    acc = [jnp.zeros(x.shape, F32)]
    pending = []

    def down():
        act, e0, e1 = pending.pop(0)
        acc[0] = acc[0] + jnp.dot(act, wd_ref[0, e0:e1, :], preferred_element_type=F32)

    def piece(e0, e1):
        def run():
            half_gate = jnp.dot(h, wi_ref[0, :, e0:e1], preferred_element_type=F32)
            up = jnp.dot(h, wi_ref[0, :, hid + e0:hid + e1], preferred_element_type=F32)
            if len(pending) >= FFN_LAG:
                down()
            pending.append(((((jnp.tanh(half_gate) + 1.0) * half_gate) * up).astype(BF16), e0, e1))
        return run

    def result():
        while pending:
            down()
        return x + mod_ref[0, 5:6, :] * acc[0]

    return [piece(e0, e1) for e0, e1 in zip(edges[:-1], edges[1:])], result


def _attn_out_ffn_kernel(x_ref, o_ref, wo_ref, mod_ref, g_ref, wi_ref, wd_ref, fg_ref, out_ref):
    ts = x_ref.shape[0] // N_SUB_FFN
    subs = [slice(i * ts, (i + 1) * ts) for i in range(N_SUB_FFN)]
    xs = [x_ref[sl, :] + mod_ref[0, 2:3, :] * jnp.dot(o_ref[sl, :], wo_ref[...], preferred_element_type=F32)
          for sl in subs]
    ffns = [_ffn_pieces(x, mod_ref, g_ref, wi_ref, wd_ref, n_chunks=FFN_CHUNKS) for x in xs]
    for c in range(FFN_CHUNKS):
        for pieces, _ in ffns:
            pieces[c]()
    for sl, (_, result) in zip(subs, ffns):
        out_ref[sl, :] = _rmsnorm(result(), fg_ref[...])


def _attn_out_ffn(xall, o, wo, mods, g, wi, wd, fg, *, layer, n_ctx_rows, seq, tm):
    n, d = xall.shape
    hid = wd.shape[1]
    off = n_ctx_rows // tm
    tpb = seq // tm
    return pl.pallas_call(
        _attn_out_ffn_kernel,
        grid=((n - n_ctx_rows) // tm,),
        in_specs=[
            pl.BlockSpec((tm, d), lambda i: (i + off, 0)),
            pl.BlockSpec((tm, o.shape[1]), lambda i: (i, 0)),
            _const_spec(wo.shape),
            pl.BlockSpec((1, 6, d), lambda i: (i // tpb, 0, 0)),
            _const_spec((1, d)),
            pl.BlockSpec((1, d, 2 * hid), lambda i: (layer, 0, 0), pipeline_mode=pl.Buffered(1)),
            pl.BlockSpec((1, hid, d), lambda i: (layer, 0, 0), pipeline_mode=pl.Buffered(1)),
            _const_spec((1, d)),
        ],
        out_specs=pl.BlockSpec((tm, d), lambda i: (i, 0)),
        out_shape=jax.ShapeDtypeStruct((n - n_ctx_rows, d), F32),
        compiler_params=_cparams(("arbitrary",)),
        name="attn_out_ffn",
    )(xall, o, wo, mods, g, wi, wd, fg)


def _mla_proj_kernel(*refs, latent, n_sub):
    if latent:
        (x_ref, g_ref, mod_ref, win_ref, kvg_ref, wk_ref, wvt_ref, qg_ref, wq_ref, cos_ref, sin_ref,
         k_ref, v_ref, q_ref) = refs
    else:
        x_ref, g_ref, mod_ref, win_ref, kvg_ref, wk_ref, wvt_ref, k_ref, v_ref = refs
    ts = x_ref.shape[0] // n_sub
    subs = [slice(i * ts, (i + 1) * ts) for i in range(n_sub)]
    kv_lo = Q_LORA + KV_LORA
    lane = lax.broadcasted_iota(jnp.int32, (1, 128), 1)
    low = lane < QK_ROPE
    hs = [_norm_mod(x_ref[sl, :], g_ref[...], mod_ref[0, 0:1, :], mod_ref[0, 1:2, :]).astype(BF16) for sl in subs]
    c_alls = [jnp.dot(h, win_ref[...], preferred_element_type=F32) for h in hs]
    for sl, c_all in zip(subs, c_alls):
        cosv, sinv = (cos_ref[sl, :], sin_ref[sl, :]) if latent else (1.0, 0.0)
        kt = c_all[:, kv_lo:kv_lo + 128] * jnp.where(low, cosv, sinv)
        kt = kt + pltpu.roll(kt, QK_ROPE, 1)
        k_rope = (jnp.where(low, kt, 0.0).astype(BF16), jnp.where(low, 0.0, kt).astype(BF16))
        ckv = _rmsnorm(c_all[:, Q_LORA:kv_lo], kvg_ref[...])
        k_nope = jnp.dot(ckv.astype(BF16), wk_ref[...], preferred_element_type=F32)
        v_t = jnp.dot(wvt_ref[...], ckv.T.astype(BF16), preferred_element_type=F32)
        for hd in range(MLA_HEADS):
            hl = slice(hd * 128, (hd + 1) * 128)
            k_ref[0, hd, sl, 0:QK_NOPE] = k_nope[:, hl].astype(BF16)
            k_ref[0, hd, sl, QK_NOPE:HEAD_PAD] = k_rope[hd % 2]
            v_ref[0, hd, :, sl] = v_t[hl, :].astype(BF16)
    if latent:
        for sl, c_all in zip(subs, c_alls):
            cq = _rmsnorm(c_all[:, :Q_LORA], qg_ref[...]).astype(BF16)
            q_all = jnp.dot(cq, wq_ref[...], preferred_element_type=F32) * Q_PRESCALE
            hw = MLA_HEADS * QK_NOPE
            hr = MLA_HEADS * QK_ROPE
            for pair in range(MLA_HEADS // 2):
                ps = slice(hw + pair * 128, hw + (pair + 1) * 128)
                ss = slice(hw + hr + pair * 128, hw + hr + (pair + 1) * 128)
                q_rope = (q_all[:, ps] * cos_ref[sl, :] + q_all[:, ss] * sin_ref[sl, :]).astype(BF16)
                for hd in (2 * pair, 2 * pair + 1):
                    q_ref[0, hd, sl, 0:QK_NOPE] = q_all[:, hd * 128:(hd + 1) * 128].astype(BF16)
                    q_ref[0, hd, sl, QK_NOPE:HEAD_PAD] = q_rope


def _mla_proj(xall, g, mods, win, kvg, wk, wvt, *, nb, ctx_len, seq, tm, latent_args=None):
    n, d = xall.shape
    latent = latent_args is not None
    rows = seq if latent else ctx_len
    off = nb * ctx_len // tm if latent else 0
    tpb = rows // tm
    kv_map = lambda i: (i // tpb, 0, i % tpb, 0)
    vt_map = lambda i: (i // tpb, 0, 0, i % tpb)
    in_specs = [
        pl.BlockSpec((tm, d), lambda i: (i + off, 0)),
        _const_spec((1, d)),
        pl.BlockSpec((1, 6, d), lambda i: ((i // tpb) if latent else nb, 0, 0)),
        _const_spec(win.shape),
        _const_spec((1, KV_LORA)),
        _const_spec(wk.shape),
        _const_spec(wvt.shape),
    ]
    args = [xall, g, mods, win, kvg, wk, wvt]
    out_specs = [pl.BlockSpec((1, MLA_HEADS, tm, HEAD_PAD), kv_map), pl.BlockSpec((1, MLA_HEADS, V_HEAD, tm), vt_map)]
    out_shape = [jax.ShapeDtypeStruct((nb, MLA_HEADS, rows, HEAD_PAD), BF16),
                 jax.ShapeDtypeStruct((nb, MLA_HEADS, V_HEAD, rows), BF16)]
    if latent:
        qg, wq, cos_t, sin_t = latent_args
        tab_map = lambda i: (i % tpb, 0)
        in_specs += [_const_spec((1, Q_LORA)), _const_spec(wq.shape),
                     pl.BlockSpec((tm, 128), tab_map), pl.BlockSpec((tm, 128), tab_map)]
        args += [qg, wq, cos_t, sin_t]
        out_specs.append(pl.BlockSpec((1, MLA_HEADS, tm, HEAD_PAD), kv_map))
        out_shape.append(jax.ShapeDtypeStruct((nb, MLA_HEADS, rows, HEAD_PAD), BF16))
    return pl.pallas_call(
        functools.partial(_mla_proj_kernel, latent=latent, n_sub=max(1, tm // MXU_DIM)),
        grid=(nb * tpb,),
        in_specs=in_specs,
        out_specs=out_specs,
        out_shape=out_shape,
        compiler_params=_cparams(("arbitrary",)),
        name="mla_proj_lat" if latent else "mla_proj_ctx",
    )(*args)


def _attn_kernel(q_ref, qn_ref, k_ref, kc_ref, kn_ref, kcn_ref, vt_ref, vtc_ref, o_ref,
                 s0_ref, s1_ref, qt_ref, acc_ref, l_ref, m_ref, *, tq, kc):
    n_lat = k_ref.shape[2]
    t_all = n_lat + kc_ref.shape[2]
    nkc = t_all // kc
    nq = q_ref.shape[2] // tq
    assert nq >= 2 and nq % 2 == 0
    s_refs = (s0_ref, s1_ref)

    def group_reduce(x, op):
        return op(x.reshape(x.shape[0] // SUBLANES, SUBLANES, tq), axis=0)

    def k_chunk(c, next_head):
        lat, ctx = (kn_ref, kcn_ref) if next_head else (k_ref, kc_ref)
        if c * kc < n_lat:
            return lat[0, 0, c * kc:(c + 1) * kc, :]
        return ctx[0, 0, c * kc - n_lat:(c + 1) * kc - n_lat, :]

    def vt_chunk(c):
        if c * kc < n_lat:
            return vt_ref[0, 0, :, c * kc:(c + 1) * kc]
        return vtc_ref[0, 0, :, c * kc - n_lat:(c + 1) * kc - n_lat]

    def scores(qt, c, s_ref, next_head=False):
        st = jnp.dot(k_chunk(c, next_head), qt, preferred_element_type=F32)
        s_ref[c * kc:(c + 1) * kc, :] = st
        return group_reduce(st, jnp.max)

    def transposed(q):
        return q.astype(F32).T.astype(BF16)

    def q_tile_t(i):
        return transposed(q_ref[0, 0, pl.ds(pl.multiple_of(i * tq, tq), tq), :])

    def finalize(i):
        l = jnp.sum(l_ref[...], axis=0, keepdims=True)
        rows = pl.ds(pl.multiple_of(i * tq, tq), tq)
        o_ref[0, rows, :] = (acc_ref[...] / l).T.astype(o_ref.dtype)

    def tile_step(i, slot, m8, qt_later, next_head=False):
        s_cur, s_nxt = s_refs[slot], s_refs[1 - slot]
        m_next = jnp.full((SUBLANES, tq), -jnp.inf, F32)
        qt_next = qt_ref[1 - slot]
        m = jnp.max(m8, axis=0, keepdims=True)
        l8 = jnp.zeros((SUBLANES, tq), F32)
        acc = jnp.zeros((V_HEAD, tq), F32)
        p_prev = None
        for c in range(nkc):
            m_next = jnp.maximum(m_next, scores(qt_next, c, s_nxt, next_head))
            if p_prev is not None:
                acc = acc + jnp.dot(vt_chunk(c - 1), p_prev, preferred_element_type=F32)
            p = jnp.exp2(s_cur[c * kc:(c + 1) * kc, :] - m)
            l8 = l8 + group_reduce(p, jnp.sum)
            p_prev = p.astype(BF16)
            if c == 1:
                finalize(jnp.maximum(i - 1, 0))
            if c == nkc // 2:
                qt_ref[slot] = qt_later()
        acc_ref[...] = acc + jnp.dot(vt_chunk(nkc - 1), p_prev, preferred_element_type=F32)
        l_ref[...] = l8
        return m_next

    @pl.when((pl.program_id(0) == 0) & (pl.program_id(1) == 0))
    def _():
        qt0 = q_tile_t(0)
        qt_ref[1] = q_tile_t(1)
        m0 = scores(qt0, 0, s0_ref)
        for c in range(1, nkc):
            m0 = jnp.maximum(m0, scores(qt0, c, s0_ref))
        m_ref[...] = m0

    acc_ref[...] = jnp.zeros_like(acc_ref)
    l_ref[...] = jnp.ones_like(l_ref)

    def step(i, m8):
        later = lambda: q_tile_t(i + 2)
        return lax.cond(i % 2 == 0,
                        lambda m: tile_step(i, 0, m, later),
                        lambda m: tile_step(i, 1, m, later), m8)

    m8 = lax.fori_loop(0, nq - 2, step, m_ref[...])
    m8 = tile_step(nq - 2, 0, m8, lambda: transposed(qn_ref[0, 0, 0:tq, :]))
    m_ref[...] = tile_step(nq - 1, 1, m8, lambda: transposed(qn_ref[0, 0, tq:2 * tq, :]), next_head=True)
    finalize(nq - 1)


def _attention(q, k, kc, vt, vtc, *, tq):
    nb, nh, seq, dp = q.shape
    ctx_len = kc.shape[2]
    t_all = seq + ctx_len
    chunk = int(np.gcd(np.gcd(MXU_DIM, seq), ctx_len))
    head = lambda b, h: (b, h, 0, 0)

    def next_head(b, h):
        idx = jnp.minimum(b * nh + h + 1, nb * nh - 1)
        return (idx // nh, idx % nh, 0, 0)

    return pl.pallas_call(
        functools.partial(_attn_kernel, tq=tq, kc=chunk),
        grid=(nb, nh),
        scratch_shapes=[pltpu.VMEM((t_all, tq), F32), pltpu.VMEM((t_all, tq), F32),
                        pltpu.VMEM((2, dp, tq), BF16), pltpu.VMEM((V_HEAD, tq), F32),
                        pltpu.VMEM((SUBLANES, tq), F32), pltpu.VMEM((SUBLANES, tq), F32)],
        in_specs=[
            pl.BlockSpec((1, 1, seq, dp), head),
            pl.BlockSpec((1, 1, 2 * tq, dp), next_head),
            pl.BlockSpec((1, 1, seq, dp), head),
            pl.BlockSpec((1, 1, ctx_len, dp), head),
            pl.BlockSpec((1, 1, seq, dp), next_head),
            pl.BlockSpec((1, 1, ctx_len, dp), next_head),
            pl.BlockSpec((1, 1, V_HEAD, seq), head),
            pl.BlockSpec((1, 1, V_HEAD, ctx_len), head),
        ],
        out_specs=pl.BlockSpec((1, seq, V_HEAD), lambda b, h: (b, 0, h)),
        out_shape=jax.ShapeDtypeStruct((nb, seq, nh * V_HEAD), BF16),
        compiler_params=_cparams(("arbitrary", "arbitrary")),
        name="mla_attention",
    )(q, q, k, kc, k, kc, vt, vtc)


def _rope_cols(w):
    qd = QK_ROPE // 4
    r0, r1, r2, r3 = (w[..., i * qd:(i + 1) * qd] for i in range(4))
    return jnp.concatenate([r0, r2, r1, r3], axis=-1), jnp.concatenate([-r1, -r3, r0, r2], axis=-1)


def _rope_tables(seq):
    rows = seq // GRID_W
    row_ids = np.repeat(np.arange(rows, dtype=np.float32), GRID_W)
    col_ids = np.tile(np.arange(GRID_W, dtype=np.float32), rows)
    axis_dim = QK_ROPE // 2
    expo = (np.arange(0, axis_dim, 2, dtype=np.float32) / np.float32(axis_dim)).astype(np.float32)
    inv_freq = (np.float32(1.0) / np.power(np.float32(ROPE_THETA), expo)).astype(np.float32)
    ang_r = (row_ids[:, None] * inv_freq).astype(np.float32)
    ang_c = (col_ids[:, None] * inv_freq).astype(np.float32)
    reps = 128 // (QK_ROPE // 2)
    cos_t = np.tile(np.concatenate([np.cos(ang_r), np.cos(ang_c)], axis=1), (1, reps))
    sin_t = np.tile(np.concatenate([np.sin(ang_r), np.sin(ang_c)], axis=1), (1, reps))
    return jnp.asarray(cos_t, F32), jnp.asarray(sin_t, F32)


def kernel(x, c, ctx, c_ctx, ada_w, ada_b, norm_mix_g, norm_ffn_g, ffn_w_in, ffn_w_out, lru_w_in, lru_conv_w,
           lru_conv_b, lru_gate_w, lru_gate_b, lru_lambda, lru_w_out, mla_w_in, mla_q_norm_g, mla_kv_norm_g,
           mla_w_uq, mla_w_ukv, mla_w_o, final_norm_g):
    nb, seq, d = x.shape
    ctx_len = ctx.shape[1]
    depth = ada_w.shape[0]
    assert depth == 2 and d == D_MODEL and nb + 1 <= MOD_ROWS
    n_ctx_rows = nb * ctx_len
    tile = lambda cap: int(np.gcd(np.gcd(cap, n_ctx_rows), seq))

    ctx2d = ctx.reshape(n_ctx_rows, d)
    x2d = x.reshape(nb * seq, d)

    cv = jnp.concatenate([c, c_ctx[None, :], jnp.zeros((MOD_ROWS - nb - 1, d), F32)], axis=0)
    mods = _ada_mods(cv, ada_w, ada_b).reshape(depth, MOD_ROWS, 6, d)

    tm = tile(1024)
    gg, u = _lru_in(ctx2d, x2d, norm_mix_g[0][None], mods[0], lru_w_in[0].astype(BF16), nb=nb, seq=seq, tm=tm)
    tc = int(np.gcd(np.gcd(256, ctx_len), seq))
    gw = lru_gate_w[0]
    gw = (0.5 * jnp.concatenate([gw[:, 0], gw[:, 1]], axis=-1)).astype(BF16)
    x2, ffn_wi, ffn_wd = _lru_scans_ffn(
        u, lru_conv_w[0], lru_conv_b[0][None], gw, 0.5 * lru_gate_b[0], lru_lambda[0][:, None, :], gg, ctx2d, x2d,
        mods[0], lru_w_out[0].astype(BF16), norm_ffn_g[0][None], ffn_w_in, ffn_w_out,
        layer=0, nb=nb, ctx_len=ctx_len, seq=seq, tc=tc)

    w_in = mla_w_in[0]
    win_p = jnp.concatenate((w_in[:, :Q_LORA + KV_LORA],) + _rope_cols(w_in[:, Q_LORA + KV_LORA:]),
                            axis=1).astype(BF16)
    wq3 = mla_w_uq[0].reshape(Q_LORA, MLA_HEADS, QK_NOPE + QK_ROPE)
    wq_all = jnp.concatenate([part.reshape(Q_LORA, -1)
                              for part in (wq3[:, :, :QK_NOPE],) + _rope_cols(wq3[:, :, QK_NOPE:])],
                             axis=1).astype(BF16)
    cos_t, sin_t = _rope_tables(seq)
    wkv3 = mla_w_ukv[0].reshape(KV_LORA, MLA_HEADS, QK_NOPE + V_HEAD)
    wk = wkv3[:, :, :QK_NOPE].reshape(KV_LORA, MLA_HEADS * QK_NOPE).astype(BF16)
    wvt = wkv3[:, :, QK_NOPE:].reshape(KV_LORA, MLA_HEADS * V_HEAD).T.astype(BF16)
    proj_args = (x2, norm_mix_g[1][None], mods[1], win_p, mla_kv_norm_g[0][None], wk, wvt)
    kc, vtc = _mla_proj(*proj_args, nb=nb, ctx_len=ctx_len, seq=seq, tm=int(np.gcd(256, ctx_len)))
    k, vt, q = _mla_proj(*proj_args, nb=nb, ctx_len=ctx_len, seq=seq, tm=tile(1024),
                         latent_args=(mla_q_norm_g[0][None], wq_all, cos_t, sin_t))
    o = _attention(q, k, kc, vt, vtc, tq=int(np.gcd(512, seq)))
    out = _attn_out_ffn(x2, o.reshape(nb * seq, MLA_HEADS * V_HEAD), mla_w_o[0].astype(BF16), mods[1],
                        norm_ffn_g[1][None], ffn_wi, ffn_wd, final_norm_g[None], layer=1,
                        n_ctx_rows=n_ctx_rows, seq=seq, tm=tile(1024))
    return out.reshape(nb, seq, d)
```
